```python
import math
import jax, jax.numpy as jnp
from jax import lax
import numpy as np

D_MODEL = 1024
BATCH = 2
SEQ = 8192
DEPTH = 2

CHUNK = 64
D_HEAD = 64
QBLOCK = 128
ROPE_THETA = 10000.0
EPS = 1e-6

A_HEADS = 4
A_DV = 2 * D_HEAD
B_HEADS = 8
B_LOOKBACK = 8
B_BAND = (B_LOOKBACK + 1) * CHUNK
B_MAX_REL = 256
B_REL_SIZE = CHUNK + B_MAX_REL
C_HEADS = 4
C_DV = 2 * D_HEAD

N_BRANCH = 3
D_FF = 2816

A_QK = A_HEADS * 2 * D_HEAD
A_V = A_HEADS * A_DV
B_QKV = B_HEADS * D_HEAD
C_QK = C_HEADS * D_HEAD
C_V = C_HEADS * C_DV
SPLITS = (A_QK, A_QK, A_V, B_QKV, B_QKV, B_QKV, C_QK, C_QK, C_V, C_V, N_BRANCH * D_MODEL)
IN_COLS = sum(SPLITS)

kernel_name = "hybrid_gated_diffattn_chunkattn_retention_macaron"


def _rms_f32(x, gain):
    xf = x.astype(jnp.float32)
    y = xf * lax.rsqrt(jnp.mean(xf * xf, axis=-1, keepdims=True) + EPS)
    return y * gain.astype(jnp.float32)


def rms_norm(x, gain):
    return _rms_f32(x, gain).astype(x.dtype)


def rope(x):
    S, d = x.shape[1], x.shape[-1]
    half = d // 2
    inv = ROPE_THETA ** (-jnp.arange(half, dtype=jnp.float32) / half)
    ang = jnp.arange(S, dtype=jnp.float32)[:, None] * inv[None, :]
    shape = (1, S) + (1,) * (x.ndim - 3) + (half,)
    cos, sin = jnp.cos(ang).reshape(shape), jnp.sin(ang).reshape(shape)
    x1, x2 = x[..., :half], x[..., half:]
    return jnp.concatenate([x1 * cos - x2 * sin, x2 * cos + x1 * sin], axis=-1)


def swiglu_ffn(x, norm, w_gate, w_up, w_down):
    h = rms_norm(x, norm)
    return (jax.nn.silu(h @ w_gate) * (h @ w_up)) @ w_down


def diff_attention(q, k, v, lam, lam_init, subln):
    bsz, S = q.shape[0], q.shape[1]
    nb = S // QBLOCK
    scale = D_HEAD ** -0.5
    qb = jnp.moveaxis(q.reshape(bsz, nb, QBLOCK, A_HEADS, 2, D_HEAD), 1, 0)
    key_chunk = jnp.arange(S) // CHUNK

    def block(args):
        i, qi = args
        q_chunk = (i * QBLOCK + jnp.arange(QBLOCK)) // CHUNK
        visible = key_chunk[None, :] <= q_chunk[:, None]
        s = jnp.einsum('bqhcd,bkhcd->bhcqk', qi, k) * scale
        p = jax.nn.softmax(jnp.where(visible, s, -jnp.inf), axis=-1)
        w = p[:, :, 0] - lam * p[:, :, 1]
        return jnp.einsum('bhqk,bkhe->bqhe', w, v)

    o = lax.map(block, (jnp.arange(nb), qb))
    o = jnp.moveaxis(o, 0, 1).reshape(bsz, S, A_HEADS, A_DV)
    o = _rms_f32(o, subln) * (1.0 - lam_init)
    return o.reshape(bsz, S, A_HEADS * A_DV)


def chunk_attention(q, k, v, rel_bias):
    bsz, S = q.shape[0], q.shape[1]
    nc = S // CHUNK
    scale = D_HEAD ** -0.5
    qc = q.reshape(bsz, nc, CHUNK, B_HEADS, D_HEAD)

    def band(t):
        tc = t.reshape(bsz, nc, CHUNK, B_HEADS, D_HEAD)
        tp = jnp.pad(tc, ((0, 0), (B_LOOKBACK, 0), (0, 0), (0, 0), (0, 0)))
        return jnp.concatenate([tp[:, j:j + nc] for j in range(B_LOOKBACK + 1)], axis=2)

    kb, vb = band(k), band(v)
    qi = jnp.arange(CHUNK)
    kj = jnp.arange(B_BAND)
    rel = qi[:, None] + B_LOOKBACK * CHUNK - kj[None, :]
    idx = jnp.clip(rel, -(CHUNK - 1), B_MAX_REL) + (CHUNK - 1)
    bias = rel_bias.astype(jnp.float32)[:, idx]
    valid = (jnp.arange(nc)[:, None] - B_LOOKBACK + (kj // CHUNK)[None, :]) >= 0
    s = jnp.einsum('bnqhd,bnkhd->bnhqk', qc, kb) * scale + bias
    s = jnp.where(valid[None, :, None, None, :], s, -jnp.inf)
    p = jax.nn.softmax(s, axis=-1)
    o = jnp.einsum('bnhqk,bnkhd->bnqhd', p, vb)
    return o.reshape(bsz, S, B_HEADS * D_HEAD)


def retention(q, k, v, g, out_norm):
    bsz, S = q.shape[0], q.shape[1]
    nc = S // CHUNK
    log_gamma = jnp.log(1.0 - 2.0 ** (-5.0 - jnp.arange(C_HEADS, dtype=jnp.float32)))
    pos = jnp.arange(CHUNK, dtype=jnp.float32)
    diff = pos[:, None] - pos[None, :]
    decay = jnp.where(diff >= 0, jnp.exp(log_gamma[:, None, None] * jnp.maximum(diff, 0.0)), 0.0)
    zeta = jnp.exp(log_gamma[:, None] * (CHUNK - 1 - pos)[None, :])
    xi = jnp.exp(log_gamma[:, None] * (pos + 1.0)[None, :])
    chunk_decay = jnp.exp(log_gamma * CHUNK)

    qc = q.reshape(bsz, nc, CHUNK, C_HEADS, D_HEAD)
    kc = k.reshape(bsz, nc, CHUNK, C_HEADS, D_HEAD)
    vc = v.reshape(bsz, nc, CHUNK, C_HEADS, C_DV)
    inner = jnp.einsum('bnqhd,bnkhd->bnhqk', qc, kc) * decay
    inner = jnp.einsum('bnhqk,bnkhe->bnqhe', inner, vc)
    kv = jnp.einsum('bnkhd,bnkhe,hk->bnhde', kc, vc, zeta)

    def step(state, kv_n):
        return state * chunk_decay[None, :, None, None] + kv_n, state

    init = jnp.zeros((bsz, C_HEADS, D_HEAD, C_DV), jnp.float32)
    _, prev = lax.scan(step, init, jnp.moveaxis(kv, 1, 0))
    prev = jnp.moveaxis(prev, 0, 1)
    cross = jnp.einsum('bnqhd,bnhde->bnqhe', qc, prev) * xi.T[None, None, :, :, None]
    o = (inner + cross).reshape(bsz, S, C_HEADS, C_DV)
    o = _rms_f32(o, out_norm).reshape(bsz, S, C_HEADS * C_DV)
    return jax.nn.silu(g) * o


def setup_inputs(seed: int = 0) -> dict:
    key = jax.random.key(seed)
    ks = iter(jax.random.split(key, 32))
    f32 = jnp.float32

    def nrm(shape, scale):
        return jax.random.normal(next(ks), shape, f32) * scale

    def gain(shape):
        return 1.0 + 0.02 * jax.random.normal(next(ks), shape, f32)

    L = DEPTH
    return {
        "x": jax.random.normal(next(ks), (BATCH, SEQ, D_MODEL), f32),
        "ffn1_norm": gain((L, D_MODEL)),
        "ffn1_w_gate": nrm((L, D_MODEL, D_FF), D_MODEL ** -0.5),
        "ffn1_w_up": nrm((L, D_MODEL, D_FF), D_MODEL ** -0.5),
        "ffn1_w_down": nrm((L, D_FF, D_MODEL), D_FF ** -0.5),
        "mix_norm": gain((L, D_MODEL)),
        "w_in": nrm((L, D_MODEL, IN_COLS), D_MODEL ** -0.5),
        "a_q_norm": gain((L, D_HEAD)),
        "a_k_norm": gain((L, D_HEAD)),
        "a_lambda_q1": nrm((L, D_HEAD), 0.1),
        "a_lambda_k1": nrm((L, D_HEAD), 0.1),
        "a_lambda_q2": nrm((L, D_HEAD), 0.1),
        "a_lambda_k2": nrm((L, D_HEAD), 0.1),
        "a_subln": gain((L, A_DV)),
        "b_q_norm": gain((L, D_HEAD)),
        "b_k_norm": gain((L, D_HEAD)),
        "b_rel_bias": nrm((L, B_HEADS, B_REL_SIZE), 0.5),
        "c_out_norm": gain((L, C_HEADS, C_DV)),
        "w_branch_a": nrm((L, A_V, D_MODEL), A_V ** -0.5),
        "w_branch_b": nrm((L, B_QKV, D_MODEL), B_QKV ** -0.5),
        "w_branch_c": nrm((L, C_V, D_MODEL), C_V ** -0.5),
        "w_out": nrm((L, D_MODEL, D_MODEL), D_MODEL ** -0.5),
        "ffn2_norm": gain((L, D_MODEL)),
        "ffn2_w_gate": nrm((L, D_MODEL, D_FF), D_MODEL ** -0.5),
        "ffn2_w_up": nrm((L, D_MODEL, D_FF), D_MODEL ** -0.5),
        "ffn2_w_down": nrm((L, D_FF, D_MODEL), D_FF ** -0.5),
    }


def reference(x, ffn1_norm, ffn1_w_gate, ffn1_w_up, ffn1_w_down, mix_norm, w_in,
              a_q_norm, a_k_norm, a_lambda_q1, a_lambda_k1, a_lambda_q2, a_lambda_k2, a_subln,
              b_q_norm, b_k_norm, b_rel_bias, c_out_norm,
              w_branch_a, w_branch_b, w_branch_c, w_out,
              ffn2_norm, ffn2_w_gate, ffn2_w_up, ffn2_w_down):
    f32 = jnp.float32
    bsz, S = x.shape[0], x.shape[1]
    split_points = [int(p) for p in np.cumsum(SPLITS)[:-1]]
    for l in range(DEPTH):
        x = x + 0.5 * swiglu_ffn(x, ffn1_norm[l], ffn1_w_gate[l], ffn1_w_up[l], ffn1_w_down[l])

        h = rms_norm(x, mix_norm[l])
        aq, ak, av, bq, bk, bv, cq, ck, cv, cg, gates = jnp.split(h @ w_in[l], split_points, axis=-1)

        lam_init = 0.8 - 0.6 * math.exp(-0.3 * l)
        lam = (jnp.exp(jnp.sum(a_lambda_q1[l].astype(f32) * a_lambda_k1[l].astype(f32)))
               - jnp.exp(jnp.sum(a_lambda_q2[l].astype(f32) * a_lambda_k2[l].astype(f32))) + lam_init)
        qa = rope(_rms_f32(aq.reshape(bsz, S, A_HEADS, 2, D_HEAD), a_q_norm[l]))
        ka = rope(_rms_f32(ak.reshape(bsz, S, A_HEADS, 2, D_HEAD), a_k_norm[l]))
        va = av.reshape(bsz, S, A_HEADS, A_DV).astype(f32)
        y_a = diff_attention(qa, ka, va, lam, lam_init, a_subln[l])

        qb = _rms_f32(bq.reshape(bsz, S, B_HEADS, D_HEAD), b_q_norm[l])
        kb = _rms_f32(bk.reshape(bsz, S, B_HEADS, D_HEAD), b_k_norm[l])
        vb = bv.reshape(bsz, S, B_HEADS, D_HEAD).astype(f32)
        y_b = chunk_attention(qb, kb, vb, b_rel_bias[l])

        qc = rope(cq.reshape(bsz, S, C_HEADS, D_HEAD).astype(f32))
        kc = rope(ck.reshape(bsz, S, C_HEADS, D_HEAD).astype(f32)) * (D_HEAD ** -0.5)
        vc = cv.reshape(bsz, S, C_HEADS, C_DV).astype(f32)
        y_c = retention(qc, kc, vc, cg.astype(f32), c_out_norm[l])

        g = jax.nn.sigmoid(gates.astype(f32)).reshape(bsz, S, N_BRANCH, D_MODEL)
        merged = (g[:, :, 0] * (y_a.astype(x.dtype) @ w_branch_a[l]).astype(f32)
                  + g[:, :, 1] * (y_b.astype(x.dtype) @ w_branch_b[l]).astype(f32)
                  + g[:, :, 2] * (y_c.astype(x.dtype) @ w_branch_c[l]).astype(f32))
        x = x + merged.astype(x.dtype) @ w_out[l]

        x = x + 0.5 * swiglu_ffn(x, ffn2_norm[l], ffn2_w_gate[l], ffn2_w_up[l], ffn2_w_down[l])
    return x
```

```python
import functools
import math

import numpy as np
import jax
import jax.numpy as jnp
from jax import lax
from jax.experimental import pallas as pl
from jax.experimental.pallas import tpu as pltpu

D_MODEL = 1024
DEPTH = 2
CHUNK = 64
D_HEAD = 64
ROPE_THETA = 10000.0
EPS = 1e-6
A_HEADS = 4
A_DV = 2 * D_HEAD
B_HEADS = 8
B_LOOKBACK = 8
B_MAX_REL = 256
C_HEADS = 4
C_DV = 2 * D_HEAD
N_BRANCH = 3
D_FF = 2816

A_QK = A_HEADS * 2 * D_HEAD
A_V = A_HEADS * A_DV
B_QKV = B_HEADS * D_HEAD
C_QK = C_HEADS * D_HEAD
C_V = C_HEADS * C_DV
SPLITS = (A_QK, A_QK, A_V, B_QKV, B_QKV, B_QKV, C_QK, C_QK, C_V, C_V, N_BRANCH * D_MODEL)
IN_COLS = sum(SPLITS)
_OFFS = tuple(int(v) for v in np.cumsum((0,) + SPLITS))

LANES = 128
MXU_DIM = 256
VMEM_BYTES_V7X = 64 * 1024 * 1024
VMEM_HEADROOM = 6 * 1024 * 1024

TOKEN_TILE = 512
FF_CHUNK = MXU_DIM
PROJ_CHUNK = 512
ATTN_BLOCK = 512
B_QTILE = 256
B_WINDOW = B_QTILE + B_LOOKBACK * CHUNK
RET_BLOCK = 512
NEG_BIG = -1e30

_BF = jnp.bfloat16
_F32 = jnp.float32
_NT = (((1,), (1,)), ((), ()))


def _vmem_limit(estimate_bytes):
    return int(min(VMEM_BYTES_V7X - VMEM_HEADROOM, max(32 * 1024 * 1024, estimate_bytes)))


def _resident(shape, index_map):
    return pl.BlockSpec(shape, index_map, pipeline_mode=pl.Buffered(1))


def _rms_rows(x, gain):
    ms = jnp.mean(x * x, axis=-1, keepdims=True)
    return x * lax.rsqrt(ms + EPS) * gain


def _ffn_kernel(x_ref, gain_ref, wg_ref, wu_ref, wd_ref, o_ref, acc_ref):
    x = x_ref[...]
    h = _rms_rows(x, gain_ref[...]).astype(_BF)
    for c in range(D_FF // FF_CHUNK):
        sl = slice(c * FF_CHUNK, (c + 1) * FF_CHUNK)
        g = jnp.dot(h, wg_ref[:, sl], preferred_element_type=_F32)
        u = jnp.dot(h, wu_ref[:, sl], preferred_element_type=_F32)
        a = (g * jax.nn.sigmoid(g) * u).astype(_BF)
        part = jnp.dot(a, wd_ref[sl, :], preferred_element_type=_F32)
        if c == 0:
            acc_ref[...] = part
        else:
            acc_ref[...] += part
    o_ref[...] = x + 0.5 * acc_ref[...]


def _ffn(x, gain, wg, wu, wd):
    t = x.shape[0]
    tm = TOKEN_TILE
    est = 3 * D_MODEL * D_FF * 2 + 5 * tm * D_MODEL * 4 + 4 * tm * FF_CHUNK * 4 + (8 << 20)
    return pl.pallas_call(
        _ffn_kernel,
        grid=(t // tm,),
        in_specs=[
            pl.BlockSpec((tm, D_MODEL), lambda i: (i, 0)),
            _resident((1, D_MODEL), lambda i: (0, 0)),
            _resident((D_MODEL, D_FF), lambda i: (0, 0)),
            _resident((D_MODEL, D_FF), lambda i: (0, 0)),
            _resident((D_FF, D_MODEL), lambda i: (0, 0)),
        ],
        out_specs=pl.BlockSpec((tm, D_MODEL), lambda i: (i, 0)),
        out_shape=jax.ShapeDtypeStruct((t, D_MODEL), _F32),
        scratch_shapes=[pltpu.VMEM((tm, D_MODEL), _F32)],
        compiler_params=pltpu.CompilerParams(
            dimension_semantics=("parallel",), vmem_limit_bytes=_vmem_limit(est)),
        name="ffn",
    )(x, gain, wg, wu, wd)


def _swap_halves(x):
    lane = lax.broadcasted_iota(jnp.int32, (1, LANES), 1)
    upper = (lane & (D_HEAD // 2)) != 0
    outs = []
    for s in range(x.shape[1] // LANES):
        xs = x[:, s * LANES:(s + 1) * LANES]
        from_below = pltpu.roll(xs, D_HEAD // 2, 1)
        from_above = pltpu.roll(xs, LANES - D_HEAD // 2, 1)
        outs.append(jnp.where(upper, from_below, from_above))
    return outs[0] if len(outs) == 1 else jnp.concatenate(outs, axis=1)


def _tile_lanes(t, n):
    reps = n // t.shape[1]
    return t if reps == 1 else jnp.concatenate([t] * reps, axis=1)


def _rope(x, cos, sin):
    n = x.shape[1]
    return x * _tile_lanes(cos, n) + _swap_halves(x) * _tile_lanes(sin, n)


def _head_ms(p, gmat):
    sq = (p * p).astype(_BF)
    outs = [jnp.dot(sq[:, s:s + MXU_DIM], gmat, preferred_element_type=_F32)
            for s in range(0, p.shape[1], MXU_DIM)]
    return outs[0] if len(outs) == 1 else jnp.concatenate(outs, axis=1)


def _in_proj_kernel(x_ref, gain_ref, w_ref, cos_ref, sin_ref, qkg_ref, gmat_ref,
                    aq_ref, ak_ref, avt_ref, bq_ref, bk_ref, bv_ref,
                    cq_ref, ckt_ref, cv_ref, cg_ref, gate_ref):
    h = _rms_rows(x_ref[...], gain_ref[...]).astype(_BF)
    cos = cos_ref[...]
    sin = sin_ref[...]
    gmat = gmat_ref[...]

    def proj(lo, width):
        return jnp.dot(h, w_ref[:, lo:lo + width], preferred_element_type=_F32)

    def normed(p, row):
        return p * lax.rsqrt(_head_ms(p, gmat) + EPS) * qkg_ref[row:row + 1, :]

    aq_ref[...] = _rope(normed(proj(_OFFS[0], A_QK), 0), cos, sin).astype(_BF)
    ak_ref[...] = _rope(normed(proj(_OFFS[1], A_QK), 1), cos, sin).astype(_BF)
    avt_ref[0] = proj(_OFFS[2], A_V).T.astype(_BF)
    bq_ref[...] = normed(proj(_OFFS[3], B_QKV), 2).astype(_BF)
    bk_ref[...] = normed(proj(_OFFS[4], B_QKV), 3).astype(_BF)
    bv_ref[...] = proj(_OFFS[5], B_QKV).astype(_BF)
    cq_ref[...] = _rope(proj(_OFFS[6], C_QK), cos, sin).astype(_BF)
    ckt_ref[0] = (_rope(proj(_OFFS[7], C_QK), cos, sin) * (D_HEAD ** -0.5)).T.astype(_BF)
    cv_ref[...] = proj(_OFFS[8], C_V).astype(_BF)
    cg_ref[...] = proj(_OFFS[9], C_V).astype(_BF)
    for c in range(N_BRANCH * D_MODEL // PROJ_CHUNK):
        lo = c * PROJ_CHUNK
        gate_ref[:, lo:lo + PROJ_CHUNK] = jax.nn.sigmoid(
            proj(_OFFS[10] + lo, PROJ_CHUNK)).astype(_BF)


def _in_proj(x, gain, w, cos_t, sin_t, qk_gains, gmat, seq):
    t = x.shape[0]
    tm = TOKEN_TILE
    nt = t // tm
    pos_blocks = seq // tm
    row = lambda i: (i, 0)
    tok_major = lambda n: pl.BlockSpec((tm, n), row)
    out_shapes = (
        jax.ShapeDtypeStruct((t, A_QK), _BF),
        jax.ShapeDtypeStruct((t, A_QK), _BF),
        jax.ShapeDtypeStruct((nt, A_V, tm), _BF),
        jax.ShapeDtypeStruct((t, B_QKV), _BF),
        jax.ShapeDtypeStruct((t, B_QKV), _BF),
        jax.ShapeDtypeStruct((t, B_QKV), _BF),
        jax.ShapeDtypeStruct((t, C_QK), _BF),
        jax.ShapeDtypeStruct((nt, C_QK, tm), _BF),
        jax.ShapeDtypeStruct((t, C_V), _BF),
        jax.ShapeDtypeStruct((t, C_V), _BF),
        jax.ShapeDtypeStruct((t, N_BRANCH * D_MODEL), _BF),
    )
    out_specs = (
        tok_major(A_QK), tok_major(A_QK),
        pl.BlockSpec((1, A_V, tm), lambda i: (i, 0, 0)),
        tok_major(B_QKV), tok_major(B_QKV), tok_major(B_QKV),
        tok_major(C_QK),
        pl.BlockSpec((1, C_QK, tm), lambda i: (i, 0, 0)),
        tok_major(C_V), tok_major(C_V), tok_major(N_BRANCH * D_MODEL),
    )
    est = D_MODEL * IN_COLS * 2 + 2 * tm * IN_COLS * 2 + 4 * tm * D_MODEL * 4 + (12 << 20)
    return pl.pallas_call(
        _in_proj_kernel,
        grid=(nt,),
        in_specs=[
            pl.BlockSpec((tm, D_MODEL), row),
            _resident((1, D_MODEL), lambda i: (0, 0)),
            _resident((D_MODEL, IN_COLS), lambda i: (0, 0)),
            pl.BlockSpec((tm, LANES), lambda i: (i % pos_blocks, 0)),
            pl.BlockSpec((tm, LANES), lambda i: (i % pos_blocks, 0)),
            _resident((4, A_QK), lambda i: (0, 0)),
            _resident((MXU_DIM, MXU_DIM), lambda i: (0, 0)),
        ],
        out_specs=out_specs,
        out_shape=out_shapes,
        compiler_params=pltpu.CompilerParams(
            dimension_semantics=("parallel",), vmem_limit_bytes=_vmem_limit(est)),
        name="in_proj",
    )(x, gain, w, cos_t, sin_t, qk_gains, gmat)


def _diff_attn_kernel(q_ref, k_ref, vt_ref, lamv_ref, subln_ref, o_ref, acc_ref, *, lam_init):
    blk = ATTN_BLOCK
    i = pl.program_id(2)
    lane = lax.broadcasted_iota(jnp.int32, (1, LANES), 1)
    q = q_ref[...]
    zero = jnp.zeros_like(q)
    qs = (jnp.where(lane < D_HEAD, q, zero), jnp.where(lane >= D_HEAD, q, zero))

    def step(j, carry, diagonal):
        kb = k_ref[pl.ds(pl.multiple_of(j * blk, blk), blk), :]
        vt = vt_ref[j]
        if diagonal:
            key_chunk = lax.broadcasted_iota(jnp.int32, (blk, blk), 0) // CHUNK
            qry_chunk = lax.broadcasted_iota(jnp.int32, (blk, blk), 1) // CHUNK
            visible = key_chunk <= qry_chunk
        out = []
        for c in range(2):
            m_old, l_old = carry[c]
            st = lax.dot_general(kb, qs[c], _NT, preferred_element_type=_F32)
            if diagonal:
                st = jnp.where(visible, st, NEG_BIG)
            m_new = jnp.maximum(m_old, jnp.max(st, axis=0, keepdims=True))
            alpha = jnp.exp(m_old - m_new)
            p = jnp.exp(st - m_new)
            l_new = alpha * l_old + jnp.sum(p, axis=0, keepdims=True)
            pv = jnp.dot(vt, p.astype(_BF), preferred_element_type=_F32)
            acc_ref[c] = alpha * acc_ref[c] + pv
            out.append((m_new, l_new))
        return tuple(out)

    acc_ref[...] = jnp.zeros_like(acc_ref)
    init = tuple((jnp.full((1, blk), NEG_BIG, _F32), jnp.zeros((1, blk), _F32)) for _ in range(2))
    carry = lax.fori_loop(0, i, lambda j, c: step(j, c, False), init)
    (_, l0), (_, l1) = step(i, carry, True)

    lamv = lamv_ref[...]
    lam = (jnp.exp(jnp.sum(lamv[0:1] * lamv[1:2], axis=-1, keepdims=True))
           - jnp.exp(jnp.sum(lamv[2:3] * lamv[3:4], axis=-1, keepdims=True)) + lam_init)
    o = acc_ref[0] / l0 - lam * (acc_ref[1] / l1)
    ms = jnp.mean(o * o, axis=0, keepdims=True)
    y = o * lax.rsqrt(ms + EPS) * subln_ref[...] * (1.0 - lam_init)
    o_ref[...] = y.T.astype(_BF)


def _diff_attn(aq, ak, avt, lamv, subln_col, batch, seq, lam_init):
    blk = ATTN_BLOCK
    nq = seq // blk
    t = batch * seq
    est = 2 * (seq * LANES * 2) * 2 + 16 * blk * blk * 4 + (8 << 20)
    return pl.pallas_call(
        functools.partial(_diff_attn_kernel, lam_init=lam_init),
        grid=(batch, A_HEADS, nq),
        in_specs=[
            pl.BlockSpec((blk, LANES), lambda b, h, i: (b * nq + i, h)),
            pl.BlockSpec((seq, LANES), lambda b, h, i: (b, h)),
            pl.BlockSpec((nq, A_DV, blk), lambda b, h, i: (b, h, 0)),
            pl.BlockSpec((4, D_HEAD), lambda b, h, i: (0, 0)),
            pl.BlockSpec((A_DV, 1), lambda b, h, i: (0, 0)),
        ],
        out_specs=pl.BlockSpec((blk, LANES), lambda b, h, i: (b * nq + i, h)),
        out_shape=jax.ShapeDtypeStruct((t, A_V), _BF),
        scratch_shapes=[pltpu.VMEM((2, A_DV, blk), _F32)],
        compiler_params=pltpu.CompilerParams(
            dimension_semantics=("parallel", "parallel", "parallel"),
            vmem_limit_bytes=_vmem_limit(est)),
        name="diff_attn",
    )(aq, ak, avt, lamv, subln_col)


def _chunk_attn_kernel(q_ref, k_ref, v_ref, mb_ref, o_ref):
    i = pl.program_id(2)
    start = pl.multiple_of(jnp.maximum(i * B_QTILE - (B_WINDOW - B_QTILE), 0), B_QTILE)
    table = jnp.minimum(i, (B_WINDOW - B_QTILE) // B_QTILE)
    lane = lax.broadcasted_iota(jnp.int32, (1, LANES), 1)
    q = q_ref[...]
    zero = jnp.zeros_like(q)
    kw = k_ref[pl.ds(start, B_WINDOW), :]
    vw = v_ref[pl.ds(start, B_WINDOW), :]
    outs = []
    for hh in range(2):
        in_head = (lane >= D_HEAD) if hh else (lane < D_HEAD)
        s = lax.dot_general(jnp.where(in_head, q, zero), kw, _NT, preferred_element_type=_F32)
        s = s + mb_ref[hh, table]
        m = jnp.max(s, axis=-1, keepdims=True)
        p = jnp.exp(s - m)
        l = jnp.sum(p, axis=-1, keepdims=True)
        outs.append(jnp.dot(p.astype(_BF), vw, preferred_element_type=_F32) / l)
    o_ref[...] = jnp.where(lane < D_HEAD, outs[0], outs[1]).astype(_BF)


def _chunk_attn(bq, bk, bv, masked_bias, batch, seq):
    nq = seq // B_QTILE
    t = batch * seq
    ntab = masked_bias.shape[1]
    est = 2 * (2 * ntab * B_QTILE * B_WINDOW * 4) + 4 * seq * LANES * 2 + (10 << 20)
    return pl.pallas_call(
        _chunk_attn_kernel,
        grid=(batch, B_HEADS // 2, nq),
        in_specs=[
            pl.BlockSpec((B_QTILE, LANES), lambda b, h, i: (b * nq + i, h)),
            pl.BlockSpec((seq, LANES), lambda b, h, i: (b, h)),
            pl.BlockSpec((seq, LANES), lambda b, h, i: (b, h)),
            pl.BlockSpec((2, ntab, B_QTILE, B_WINDOW), lambda b, h, i: (h, 0, 0, 0)),
        ],
        out_specs=pl.BlockSpec((B_QTILE, LANES), lambda b, h, i: (b * nq + i, h)),
        out_shape=jax.ShapeDtypeStruct((t, B_QKV), _BF),
        compiler_params=pltpu.CompilerParams(
            dimension_semantics=("parallel", "parallel", "parallel"),
            vmem_limit_bytes=_vmem_limit(est)),
        name="chunk_attn",
    )(bq, bk, bv, masked_bias)


def _masked_bias_tables(rel_bias):
    ntab = (B_WINDOW - B_QTILE) // B_QTILE + 1
    qi = np.arange(B_QTILE)[:, None]
    kj = np.arange(B_WINDOW)[None, :]
    tabs = []
    for t in range(ntab):
        off = t * B_QTILE
        rel = qi - kj + off
        idx = np.clip(rel, -(CHUNK - 1), B_MAX_REL) + (CHUNK - 1)
        dchunk = off // CHUNK + qi // CHUNK - kj // CHUNK
        visible = (dchunk >= 0) & (dchunk <= B_LOOKBACK)
        tabs.append(jnp.where(visible[None], rel_bias.astype(_F32)[:, idx], NEG_BIG))
    return jnp.stack(tabs, axis=1)


def _retention_kernel(q_ref, kt_ref, v_ref, g_ref, cn_ref, o_ref, decay_ref, *, seq):
    blk = RET_BLOCK
    h = pl.program_id(1)
    hf = jnp.full((1, 1), h, jnp.int32).astype(_F32)
    log_gamma = jnp.log(1.0 - jnp.exp2(-5.0 - hf))
    diff = (lax.broadcasted_iota(jnp.int32, (blk, blk), 0)
            - lax.broadcasted_iota(jnp.int32, (blk, blk), 1)).astype(_F32)
    decay_ref[...] = jnp.where(diff >= 0, jnp.exp(log_gamma * jnp.maximum(diff, 0.0)), 0.0)
    pos = lax.broadcasted_iota(jnp.int32, (blk, LANES), 0).astype(_F32)
    xi = jnp.exp(log_gamma * (pos + 1.0))
    zeta = jnp.exp(log_gamma * (blk - 1.0 - pos))
    block_decay = jnp.exp(log_gamma * float(blk))
    odd = (h % 2) == 1
    q_keep = (lax.broadcasted_iota(jnp.int32, (1, LANES), 1) >= D_HEAD) == odd
    k_keep = (lax.broadcasted_iota(jnp.int32, (LANES, 1), 0) >= D_HEAD) == odd
    gain = cn_ref[0]

    def body(j, state):
        rows = pl.ds(pl.multiple_of(j * blk, blk), blk)
        qb = q_ref[rows, :]
        qb = jnp.where(q_keep, qb, jnp.zeros_like(qb))
        kt = kt_ref[j]
        kt = jnp.where(k_keep, kt, jnp.zeros_like(kt))
        vb = v_ref[rows, :]
        sc = jnp.dot(qb, kt, preferred_element_type=_F32) * decay_ref[...]
        inner = jnp.dot(sc.astype(_BF), vb, preferred_element_type=_F32)
        cross = jnp.dot(qb, state.astype(_BF), preferred_element_type=_F32) * xi
        o = inner + cross
        vz = (vb.astype(_F32) * zeta).astype(_BF)
        new_state = state * block_decay + jnp.dot(kt, vz, preferred_element_type=_F32)
        ms = jnp.mean(o * o, axis=-1, keepdims=True)
        y = o * lax.rsqrt(ms + EPS) * gain
        gg = g_ref[rows, :].astype(_F32)
        o_ref[rows, :] = (gg * jax.nn.sigmoid(gg) * y).astype(_BF)
        return new_state

    lax.fori_loop(0, seq // blk, body, jnp.zeros((LANES, C_DV), _F32))


def _retention(cq, ckt, cv, cg, cnorm, batch, seq):
    blk = RET_BLOCK
    nb = seq // blk
    t = batch * seq
    est = 2 * 5 * seq * LANES * 2 + 6 * blk * blk * 4 + (8 << 20)
    return pl.pallas_call(
        functools.partial(_retention_kernel, seq=seq),
        grid=(batch, C_HEADS),
        in_specs=[
            pl.BlockSpec((seq, LANES), lambda b, h: (b, h // 2)),
            pl.BlockSpec((nb, LANES, blk), lambda b, h: (b, h // 2, 0)),
            pl.BlockSpec((seq, C_DV), lambda b, h: (b, h)),
            pl.BlockSpec((seq, C_DV), lambda b, h: (b, h)),
            pl.BlockSpec((1, 1, C_DV), lambda b, h: (h, 0, 0)),
        ],
        out_specs=pl.BlockSpec((seq, C_DV), lambda b, h: (b, h)),
        out_shape=jax.ShapeDtypeStruct((t, C_V), _BF),
        scratch_shapes=[pltpu.VMEM((blk, blk), _F32)],
        compiler_params=pltpu.CompilerParams(
            dimension_semantics=("parallel", "parallel"), vmem_limit_bytes=_vmem_limit(est)),
        name="retention",
    )(cq, ckt, cv, cg, cnorm)


def _merge_kernel(x_ref, ya_ref, yb_ref, yc_ref, gate_ref, wa_ref, wb_ref, wc_ref, wo_ref, o_ref):
    merged = None
    for n, (y_ref, w_ref) in enumerate(((ya_ref, wa_ref), (yb_ref, wb_ref), (yc_ref, wc_ref))):
        br = jnp.dot(y_ref[...], w_ref[...], preferred_element_type=_F32)
        term = gate_ref[:, n * D_MODEL:(n + 1) * D_MODEL].astype(_F32) * br
        merged = term if merged is None else merged + term
    o_ref[...] = x_ref[...] + jnp.dot(merged.astype(_BF), wo_ref[...], preferred_element_type=_F32)


def _merge(x, ya, yb, yc, gates, wa, wb, wc, wo):
    t = x.shape[0]
    tm = TOKEN_TILE
    row = lambda i: (i, 0)
    const = lambda i: (0, 0)
    est = 5 * D_MODEL * D_MODEL * 2 + 2 * tm * (2 * D_MODEL * 4 + 3 * 512 * 2 + 3 * D_MODEL * 2) \
        + 6 * tm * D_MODEL * 4 + (8 << 20)
    return pl.pallas_call(
        _merge_kernel,
        grid=(t // tm,),
        in_specs=[
            pl.BlockSpec((tm, D_MODEL), row),
            pl.BlockSpec((tm, A_V), row),
            pl.BlockSpec((tm, B_QKV), row),
            pl.BlockSpec((tm, C_V), row),
            pl.BlockSpec((tm, N_BRANCH * D_MODEL), row),
            _resident((A_V, D_MODEL), const),
            _resident((B_QKV, D_MODEL), const),
            _resident((C_V, D_MODEL), const),
            _resident((D_MODEL, D_MODEL), const),
        ],
        out_specs=pl.BlockSpec((tm, D_MODEL), row),
        out_shape=jax.ShapeDtypeStruct((t, D_MODEL), _F32),
        compiler_params=pltpu.CompilerParams(
            dimension_semantics=("parallel",), vmem_limit_bytes=_vmem_limit(est)),
        name="merge",
    )(x, ya, yb, yc, gates, wa, wb, wc, wo)


def _rope_tables(seq):
    half = D_HEAD // 2
    inv = ROPE_THETA ** (-jnp.arange(half, dtype=_F32) / half)
    ang = jnp.arange(seq, dtype=_F32)[:, None] * inv[None, :]
    cos, sin = jnp.cos(ang), jnp.sin(ang)
    cos_t = jnp.concatenate([cos, cos] * (LANES // D_HEAD), axis=1)
    sin_t = jnp.concatenate([-sin, sin] * (LANES // D_HEAD), axis=1)
    return cos_t, sin_t


def _head_mean_matrix():
    g = np.kron(np.eye(MXU_DIM // D_HEAD), np.full((D_HEAD, D_HEAD), 1.0 / D_HEAD))
    return jnp.asarray(g, dtype=_BF)


def kernel(x, ffn1_norm, ffn1_w_gate, ffn1_w_up, ffn1_w_down, mix_norm, w_in, a_q_norm, a_k_norm, a_lambda_q1, a_lambda_k1, a_lambda_q2, a_lambda_k2, a_subln, b_q_norm, b_k_norm, b_rel_bias, c_out_norm, w_branch_a, w_branch_b, w_branch_c, w_out, ffn2_norm, ffn2_w_gate, ffn2_w_up, ffn2_w_down):
    batch, seq, d = x.shape
    assert d == D_MODEL and seq % ATTN_BLOCK == 0 and seq % RET_BLOCK == 0
    assert seq % TOKEN_TILE == 0 and seq >= B_WINDOW and ATTN_BLOCK == TOKEN_TILE == RET_BLOCK
    scale = D_HEAD ** -0.5
    cos_t, sin_t = _rope_tables(seq)
    gmat = _head_mean_matrix()
    bf = lambda w: w.astype(_BF)
    xt = x.reshape(batch * seq, D_MODEL)
    for l in range(DEPTH):
        lam_init = 0.8 - 0.6 * math.exp(-0.3 * l)
        xt = _ffn(xt, ffn1_norm[l][None, :], bf(ffn1_w_gate[l]), bf(ffn1_w_up[l]), bf(ffn1_w_down[l]))

        heads = A_QK // D_HEAD
        qk_gains = jnp.stack([
            jnp.tile(a_q_norm[l].astype(_F32) * scale, heads),
            jnp.tile(a_k_norm[l].astype(_F32), heads),
            jnp.tile(b_q_norm[l].astype(_F32) * scale, heads),
            jnp.tile(b_k_norm[l].astype(_F32), heads),
        ])
        aq, ak, avt, bq, bk, bv, cq, ckt, cv, cg, gates = _in_proj(
            xt, mix_norm[l][None, :], bf(w_in[l]), cos_t, sin_t, qk_gains, gmat, seq)

        lamv = jnp.stack([a_lambda_q1[l], a_lambda_k1[l], a_lambda_q2[l], a_lambda_k2[l]]).astype(_F32)
        ya = _diff_attn(aq, ak, avt, lamv, a_subln[l].astype(_F32)[:, None], batch, seq, lam_init)
        yb = _chunk_attn(bq, bk, bv, _masked_bias_tables(b_rel_bias[l]), batch, seq)
        yc = _retention(cq, ckt, cv, cg, c_out_norm[l].astype(_F32)[:, None, :], batch, seq)

        xt = _merge(xt, ya, yb, yc, gates, bf(w_branch_a[l]), bf(w_branch_b[l]), bf(w_branch_c[l]),
                    bf(w_out[l]))
        xt = _ffn(xt, ffn2_norm[l][None, :], bf(ffn2_w_gate[l]), bf(ffn2_w_up[l]), bf(ffn2_w_down[l]))
    return xt.reshape(batch, seq, D_MODEL)
```

```python
import functools
import math

import numpy as np
import jax
import jax.numpy as jnp
from jax import lax
from jax.experimental import pallas as pl
from jax.experimental.pallas import tpu as pltpu

D_MODEL = 1024
DEPTH = 2
CHUNK = 64
D_HEAD = 64
ROPE_THETA = 10000.0
EPS = 1e-6
A_HEADS = 4
A_DV = 2 * D_HEAD
B_HEADS = 8
B_LOOKBACK = 8
B_MAX_REL = 256
C_HEADS = 4
C_DV = 2 * D_HEAD
N_BRANCH = 3
D_FF = 2816

A_QK = A_HEADS * 2 * D_HEAD
A_V = A_HEADS * A_DV
B_QKV = B_HEADS * D_HEAD
C_QK = C_HEADS * D_HEAD
C_V = C_HEADS * C_DV
SPLITS = (A_QK, A_QK, A_V, B_QKV, B_QKV, B_QKV, C_QK, C_QK, C_V, C_V, N_BRANCH * D_MODEL)
IN_COLS = sum(SPLITS)
_OFFS = tuple(int(v) for v in np.cumsum((0,) + SPLITS))

LANES = 128
MXU_DIM = 256
VMEM_BYTES_V7X = 64 * 1024 * 1024
VMEM_HEADROOM = 6 * 1024 * 1024

TOKEN_TILE = 512
FF_CHUNK = MXU_DIM
PROJ_CHUNK = 512
ATTN_BLOCK = 512
B_QTILE = 256
B_WINDOW = B_QTILE + B_LOOKBACK * CHUNK
RET_BLOCK = 512
NEG_BIG = -1e30

_BF = jnp.bfloat16
_F32 = jnp.float32
_NT = (((1,), (1,)), ((), ()))


def _vmem_limit(estimate_bytes):
    return int(min(VMEM_BYTES_V7X - VMEM_HEADROOM, max(32 * 1024 * 1024, estimate_bytes)))


def _resident(shape, index_map):
    return pl.BlockSpec(shape, index_map, pipeline_mode=pl.Buffered(1))


def _rms_rows(x, gain):
    ms = jnp.mean(x * x, axis=-1, keepdims=True)
    return x * lax.rsqrt(ms + EPS) * gain


def _ffn_kernel(x_ref, gain_ref, wg_ref, wu_ref, wd_ref, o_ref, acc_ref):
    x = x_ref[...]
    h = _rms_rows(x, gain_ref[...]).astype(_BF)
    for c in range(D_FF // FF_CHUNK):
        sl = slice(c * FF_CHUNK, (c + 1) * FF_CHUNK)
        g = jnp.dot(h, wg_ref[:, sl], preferred_element_type=_F32)
        u = jnp.dot(h, wu_ref[:, sl], preferred_element_type=_F32)
        a = (g * jax.nn.sigmoid(g) * u).astype(_BF)
        part = jnp.dot(a, wd_ref[sl, :], preferred_element_type=_F32)
        if c == 0:
            acc_ref[...] = part
        else:
            acc_ref[...] += part
    o_ref[...] = x + 0.5 * acc_ref[...]


def _ffn(x, gain, wg, wu, wd):
    t = x.shape[0]
    tm = TOKEN_TILE
    est = 3 * D_MODEL * D_FF * 2 + 5 * tm * D_MODEL * 4 + 4 * tm * FF_CHUNK * 4 + (8 << 20)
    return pl.pallas_call(
        _ffn_kernel,
        grid=(t // tm,),
        in_specs=[
            pl.BlockSpec((tm, D_MODEL), lambda i: (i, 0)),
            _resident((1, D_MODEL), lambda i: (0, 0)),
            _resident((D_MODEL, D_FF), lambda i: (0, 0)),
            _resident((D_MODEL, D_FF), lambda i: (0, 0)),
            _resident((D_FF, D_MODEL), lambda i: (0, 0)),
        ],
        out_specs=pl.BlockSpec((tm, D_MODEL), lambda i: (i, 0)),
        out_shape=jax.ShapeDtypeStruct((t, D_MODEL), _F32),
        scratch_shapes=[pltpu.VMEM((tm, D_MODEL), _F32)],
        compiler_params=pltpu.CompilerParams(
            dimension_semantics=("parallel",), vmem_limit_bytes=_vmem_limit(est)),
        name="ffn",
    )(x, gain, wg, wu, wd)


def _swap_halves(x):
    lane = lax.broadcasted_iota(jnp.int32, (1, LANES), 1)
    upper = (lane & (D_HEAD // 2)) != 0
    outs = []
    for s in range(x.shape[1] // LANES):
        xs = x[:, s * LANES:(s + 1) * LANES]
        from_below = pltpu.roll(xs, D_HEAD // 2, 1)
        from_above = pltpu.roll(xs, LANES - D_HEAD // 2, 1)
        outs.append(jnp.where(upper, from_below, from_above))
    return outs[0] if len(outs) == 1 else jnp.concatenate(outs, axis=1)


def _tile_lanes(t, n):
    reps = n // t.shape[1]
    return t if reps == 1 else jnp.concatenate([t] * reps, axis=1)


def _rope(x, cos, sin):
    n = x.shape[1]
    return x * _tile_lanes(cos, n) + _swap_halves(x) * _tile_lanes(sin, n)


def _head_ms(p, gmat):
    sq = (p * p).astype(_BF)
    outs = [jnp.dot(sq[:, s:s + MXU_DIM], gmat, preferred_element_type=_F32)
            for s in range(0, p.shape[1], MXU_DIM)]
    return outs[0] if len(outs) == 1 else jnp.concatenate(outs, axis=1)


def _in_proj_kernel(x_ref, gain_ref, w_ref, cos_ref, sin_ref, qkg_ref, gmat_ref,
                    aq_ref, ak_ref, avt_ref, bq_ref, bk_ref, bv_ref,
                    cq_ref, ckt_ref, cv_ref, cg_ref, gate_ref):
    h = _rms_rows(x_ref[...], gain_ref[...]).astype(_BF)
    cos = cos_ref[...]
    sin = sin_ref[...]
    gmat = gmat_ref[...]

    def proj(lo, width):
        return jnp.dot(h, w_ref[:, lo:lo + width], preferred_element_type=_F32)

    def normed(p, row):
        return p * lax.rsqrt(_head_ms(p, gmat) + EPS) * qkg_ref[row:row + 1, :]

    aq_ref[...] = _rope(normed(proj(_OFFS[0], A_QK), 0), cos, sin).astype(_BF)
    ak_ref[...] = _rope(normed(proj(_OFFS[1], A_QK), 1), cos, sin).astype(_BF)
    avt_ref[0] = proj(_OFFS[2], A_V).T.astype(_BF)
    bq_ref[...] = normed(proj(_OFFS[3], B_QKV), 2).astype(_BF)
    bk_ref[...] = normed(proj(_OFFS[4], B_QKV), 3).astype(_BF)
    bv_ref[...] = proj(_OFFS[5], B_QKV).astype(_BF)
    cq_ref[...] = _rope(proj(_OFFS[6], C_QK), cos, sin).astype(_BF)
    ckt_ref[0] = (_rope(proj(_OFFS[7], C_QK), cos, sin) * (D_HEAD ** -0.5)).T.astype(_BF)
    cv_ref[...] = proj(_OFFS[8], C_V).astype(_BF)
    cg_ref[...] = proj(_OFFS[9], C_V).astype(_BF)
    for c in range(N_BRANCH * D_MODEL // PROJ_CHUNK):
        lo = c * PROJ_CHUNK
        gate_ref[:, lo:lo + PROJ_CHUNK] = jax.nn.sigmoid(
            proj(_OFFS[10] + lo, PROJ_CHUNK)).astype(_BF)


def _in_proj(x, gain, w, cos_t, sin_t, qk_gains, gmat, seq):
    t = x.shape[0]
    tm = TOKEN_TILE
    nt = t // tm
    pos_blocks = seq // tm
    row = lambda i: (i, 0)
    tok_major = lambda n: pl.BlockSpec((tm, n), row)
    out_shapes = (
        jax.ShapeDtypeStruct((t, A_QK), _BF),
        jax.ShapeDtypeStruct((t, A_QK), _BF),
        jax.ShapeDtypeStruct((nt, A_V, tm), _BF),
        jax.ShapeDtypeStruct((t, B_QKV), _BF),
        jax.ShapeDtypeStruct((t, B_QKV), _BF),
        jax.ShapeDtypeStruct((t, B_QKV), _BF),
        jax.ShapeDtypeStruct((t, C_QK), _BF),
        jax.ShapeDtypeStruct((nt, C_QK, tm), _BF),
        jax.ShapeDtypeStruct((t, C_V), _BF),
        jax.ShapeDtypeStruct((t, C_V), _BF),
        jax.ShapeDtypeStruct((t, N_BRANCH * D_MODEL), _BF),
    )
    out_specs = (
        tok_major(A_QK), tok_major(A_QK),
        pl.BlockSpec((1, A_V, tm), lambda i: (i, 0, 0)),
        tok_major(B_QKV), tok_major(B_QKV), tok_major(B_QKV),
        tok_major(C_QK),
        pl.BlockSpec((1, C_QK, tm), lambda i: (i, 0, 0)),
        tok_major(C_V), tok_major(C_V), tok_major(N_BRANCH * D_MODEL),
    )
    est = D_MODEL * IN_COLS * 2 + 2 * tm * IN_COLS * 2 + 4 * tm * D_MODEL * 4 + (12 << 20)
    return pl.pallas_call(
        _in_proj_kernel,
        grid=(nt,),
        in_specs=[
            pl.BlockSpec((tm, D_MODEL), row),
            _resident((1, D_MODEL), lambda i: (0, 0)),
            _resident((D_MODEL, IN_COLS), lambda i: (0, 0)),
            pl.BlockSpec((tm, LANES), lambda i: (i % pos_blocks, 0)),
            pl.BlockSpec((tm, LANES), lambda i: (i % pos_blocks, 0)),
            _resident((4, A_QK), lambda i: (0, 0)),
            _resident((MXU_DIM, MXU_DIM), lambda i: (0, 0)),
        ],
        out_specs=out_specs,
        out_shape=out_shapes,
        compiler_params=pltpu.CompilerParams(
            dimension_semantics=("parallel",), vmem_limit_bytes=_vmem_limit(est)),
        name="in_proj",
    )(x, gain, w, cos_t, sin_t, qk_gains, gmat)


def _diff_attn_kernel(q_ref, k_ref, vt_ref, lamv_ref, subln_ref, o_ref, acc_ref, *, lam_init):
    blk = ATTN_BLOCK
    i = pl.program_id(2)
    lane = lax.broadcasted_iota(jnp.int32, (1, LANES), 1)
    q = q_ref[...]
    zero = jnp.zeros_like(q)
    qs = (jnp.where(lane < D_HEAD, q, zero), jnp.where(lane >= D_HEAD, q, zero))

    def step(j, carry, diagonal):
        kb = k_ref[pl.ds(pl.multiple_of(j * blk, blk), blk), :]
        vt = vt_ref[j]
        if diagonal:
            key_chunk = lax.broadcasted_iota(jnp.int32, (blk, blk), 0) // CHUNK
            qry_chunk = lax.broadcasted_iota(jnp.int32, (blk, blk), 1) // CHUNK
            visible = key_chunk <= qry_chunk
        out = []
        for c in range(2):
            m_old, l_old = carry[c]
            st = lax.dot_general(kb, qs[c], _NT, preferred_element_type=_F32)
            if diagonal:
                st = jnp.where(visible, st, NEG_BIG)
            m_new = jnp.maximum(m_old, jnp.max(st, axis=0, keepdims=True))
            alpha = jnp.exp(m_old - m_new)
            p = jnp.exp(st - m_new)
            l_new = alpha * l_old + jnp.sum(p, axis=0, keepdims=True)
            pv = jnp.dot(vt, p.astype(_BF), preferred_element_type=_F32)
            acc_ref[c] = alpha * acc_ref[c] + pv
            out.append((m_new, l_new))
        return tuple(out)

    acc_ref[...] = jnp.zeros_like(acc_ref)
    init = tuple((jnp.full((1, blk), NEG_BIG, _F32), jnp.zeros((1, blk), _F32)) for _ in range(2))
    carry = lax.fori_loop(0, i, lambda j, c: step(j, c, False), init)
    (_, l0), (_, l1) = step(i, carry, True)

    lamv = lamv_ref[...]
    lam = (jnp.exp(jnp.sum(lamv[0:1] * lamv[1:2], axis=-1, keepdims=True))
           - jnp.exp(jnp.sum(lamv[2:3] * lamv[3:4], axis=-1, keepdims=True)) + lam_init)
    o = acc_ref[0] / l0 - lam * (acc_ref[1] / l1)
    ms = jnp.mean(o * o, axis=0, keepdims=True)
    y = o * lax.rsqrt(ms + EPS) * subln_ref[...] * (1.0 - lam_init)
    o_ref[...] = y.T.astype(_BF)


def _diff_attn(aq, ak, avt, lamv, subln_col, batch, seq, lam_init):
    blk = ATTN_BLOCK
    nq = seq // blk
    t = batch * seq
    est = 2 * (seq * LANES * 2) * 2 + 16 * blk * blk * 4 + (8 << 20)
    return pl.pallas_call(
        functools.partial(_diff_attn_kernel, lam_init=lam_init),
        grid=(batch, A_HEADS, nq),
        in_specs=[
            pl.BlockSpec((blk, LANES), lambda b, h, i: (b * nq + i, h)),
            pl.BlockSpec((seq, LANES), lambda b, h, i: (b, h)),
            pl.BlockSpec((nq, A_DV, blk), lambda b, h, i: (b, h, 0)),
            pl.BlockSpec((4, D_HEAD), lambda b, h, i: (0, 0)),
            pl.BlockSpec((A_DV, 1), lambda b, h, i: (0, 0)),
        ],
        out_specs=pl.BlockSpec((blk, LANES), lambda b, h, i: (b * nq + i, h)),
        out_shape=jax.ShapeDtypeStruct((t, A_V), _BF),
        scratch_shapes=[pltpu.VMEM((2, A_DV, blk), _F32)],
        compiler_params=pltpu.CompilerParams(
            dimension_semantics=("parallel", "parallel", "parallel"),
            vmem_limit_bytes=_vmem_limit(est)),
        name="diff_attn",
    )(aq, ak, avt, lamv, subln_col)


B_TABLES = (B_WINDOW - B_QTILE) // B_QTILE + 1
B_GEN_LEN = 2048


def _build_bias_tables(gen_ref, mb_ref):
    qchunk = lax.broadcasted_iota(jnp.int32, (B_QTILE, B_WINDOW), 0) // CHUNK
    kchunk = lax.broadcasted_iota(jnp.int32, (B_QTILE, B_WINDOW), 1) // CHUNK
    for hh in range(2):
        rolled = pltpu.roll(jnp.broadcast_to(gen_ref[hh], (B_QTILE, B_GEN_LEN)), 0, 1,
                            stride=1, stride_axis=0)
        for t in range(B_TABLES):
            lo = B_WINDOW - t * B_QTILE
            dchunk = qchunk - kchunk + (t * B_QTILE) // CHUNK
            visible = (dchunk >= 0) & (dchunk <= B_LOOKBACK)
            mb_ref[hh, t] = jnp.where(visible, rolled[:, lo:lo + B_WINDOW], NEG_BIG)


def _chunk_attn_kernel(q_ref, k_ref, v_ref, gen_ref, o_ref, mb_ref):
    i = pl.program_id(2)

    @pl.when(i == 0)
    def _():
        _build_bias_tables(gen_ref, mb_ref)

    start = pl.multiple_of(jnp.maximum(i * B_QTILE - (B_WINDOW - B_QTILE), 0), B_QTILE)
    table = jnp.minimum(i, B_TABLES - 1)
    lane = lax.broadcasted_iota(jnp.int32, (1, LANES), 1)
    q = q_ref[...]
    zero = jnp.zeros_like(q)
    kw = k_ref[pl.ds(start, B_WINDOW), :]
    vw = v_ref[pl.ds(start, B_WINDOW), :]
    outs = []
    for hh in range(2):
        in_head = (lane >= D_HEAD) if hh else (lane < D_HEAD)
        s = lax.dot_general(jnp.where(in_head, q, zero), kw, _NT, preferred_element_type=_F32)
        s = s + mb_ref[hh, table]
        m = jnp.max(s, axis=-1, keepdims=True)
        p = jnp.exp(s - m)
        l = jnp.sum(p, axis=-1, keepdims=True)
        outs.append(jnp.dot(p.astype(_BF), vw, preferred_element_type=_F32) / l)
    o_ref[...] = jnp.where(lane < D_HEAD, outs[0], outs[1]).astype(_BF)


def _chunk_attn(bq, bk, bv, bias_gen, batch, seq):
    nq = seq // B_QTILE
    t = batch * seq
    est = 2 * B_TABLES * B_QTILE * B_WINDOW * 4 + 4 * B_QTILE * B_GEN_LEN * 4 \
        + 4 * seq * LANES * 2 + (10 << 20)
    return pl.pallas_call(
        _chunk_attn_kernel,
        grid=(batch, B_HEADS // 2, nq),
        in_specs=[
            pl.BlockSpec((B_QTILE, LANES), lambda b, h, i: (b * nq + i, h)),
            pl.BlockSpec((seq, LANES), lambda b, h, i: (b, h)),
            pl.BlockSpec((seq, LANES), lambda b, h, i: (b, h)),
            pl.BlockSpec((2, 1, B_GEN_LEN), lambda b, h, i: (h, 0, 0)),
        ],
        out_specs=pl.BlockSpec((B_QTILE, LANES), lambda b, h, i: (b * nq + i, h)),
        out_shape=jax.ShapeDtypeStruct((t, B_QKV), _BF),
        scratch_shapes=[pltpu.VMEM((2, B_TABLES, B_QTILE, B_WINDOW), _F32)],
        compiler_params=pltpu.CompilerParams(
            dimension_semantics=("parallel", "parallel", "arbitrary"),
            vmem_limit_bytes=_vmem_limit(est)),
        name="chunk_attn",
    )(bq, bk, bv, bias_gen)


def _bias_generator(rel_bias):
    b = rel_bias.astype(_F32)
    n_far = B_WINDOW - B_MAX_REL + 1
    mid = b[:, 1:CHUNK - 1 + B_MAX_REL][:, ::-1]
    n_neg = B_GEN_LEN - n_far - mid.shape[1]
    heads = b.shape[0]
    gen = jnp.concatenate([
        jnp.broadcast_to(b[:, -1:], (heads, n_far)), mid,
        jnp.broadcast_to(b[:, :1], (heads, n_neg))], axis=1)
    return gen[:, None, :]


def _retention_kernel(q_ref, kt_ref, v_ref, g_ref, cn_ref, o_ref, decay_ref, *, seq):
    blk = RET_BLOCK
    h = pl.program_id(1)
    hf = jnp.full((1, 1), h, jnp.int32).astype(_F32)
    log_gamma = jnp.log(1.0 - jnp.exp2(-5.0 - hf))
    diff = (lax.broadcasted_iota(jnp.int32, (blk, blk), 0)
            - lax.broadcasted_iota(jnp.int32, (blk, blk), 1)).astype(_F32)
    decay_ref[...] = jnp.where(diff >= 0, jnp.exp(log_gamma * jnp.maximum(diff, 0.0)), 0.0)
    pos = lax.broadcasted_iota(jnp.int32, (blk, LANES), 0).astype(_F32)
    xi = jnp.exp(log_gamma * (pos + 1.0))
    zeta = jnp.exp(log_gamma * (blk - 1.0 - pos))
    block_decay = jnp.exp(log_gamma * float(blk))
    odd = (h % 2) == 1
    q_keep = (lax.broadcasted_iota(jnp.int32, (1, LANES), 1) >= D_HEAD) == odd
    k_keep = (lax.broadcasted_iota(jnp.int32, (LANES, 1), 0) >= D_HEAD) == odd
    gain = cn_ref[0]

    def body(j, state):
        rows = pl.ds(pl.multiple_of(j * blk, blk), blk)
        qb = q_ref[rows, :]
        qb = jnp.where(q_keep, qb, jnp.zeros_like(qb))
        kt = kt_ref[j]
        kt = jnp.where(k_keep, kt, jnp.zeros_like(kt))
        vb = v_ref[rows, :]
        sc = jnp.dot(qb, kt, preferred_element_type=_F32) * decay_ref[...]
        inner = jnp.dot(sc.astype(_BF), vb, preferred_element_type=_F32)
        cross = jnp.dot(qb, state.astype(_BF), preferred_element_type=_F32) * xi
        o = inner + cross
        vz = (vb.astype(_F32) * zeta).astype(_BF)
        new_state = state * block_decay + jnp.dot(kt, vz, preferred_element_type=_F32)
        ms = jnp.mean(o * o, axis=-1, keepdims=True)
        y = o * lax.rsqrt(ms + EPS) * gain
        gg = g_ref[rows, :].astype(_F32)
        o_ref[rows, :] = (gg * jax.nn.sigmoid(gg) * y).astype(_BF)
        return new_state

    lax.fori_loop(0, seq // blk, body, jnp.zeros((LANES, C_DV), _F32))


def _retention(cq, ckt, cv, cg, cnorm, batch, seq):
    blk = RET_BLOCK
    nb = seq // blk
    t = batch * seq
    est = 2 * 5 * seq * LANES * 2 + 6 * blk * blk * 4 + (8 << 20)
    return pl.pallas_call(
        functools.partial(_retention_kernel, seq=seq),
        grid=(batch, C_HEADS),
        in_specs=[
            pl.BlockSpec((seq, LANES), lambda b, h: (b, h // 2)),
            pl.BlockSpec((nb, LANES, blk), lambda b, h: (b, h // 2, 0)),
            pl.BlockSpec((seq, C_DV), lambda b, h: (b, h)),
            pl.BlockSpec((seq, C_DV), lambda b, h: (b, h)),
            pl.BlockSpec((1, 1, C_DV), lambda b, h: (h, 0, 0)),
        ],
        out_specs=pl.BlockSpec((seq, C_DV), lambda b, h: (b, h)),
        out_shape=jax.ShapeDtypeStruct((t, C_V), _BF),
        scratch_shapes=[pltpu.VMEM((blk, blk), _F32)],
        compiler_params=pltpu.CompilerParams(
            dimension_semantics=("parallel", "parallel"), vmem_limit_bytes=_vmem_limit(est)),
        name="retention",
    )(cq, ckt, cv, cg, cnorm)


def _merge_kernel(x_ref, ya_ref, yb_ref, yc_ref, gate_ref, wa_ref, wb_ref, wc_ref, wo_ref, o_ref):
    merged = None
    for n, (y_ref, w_ref) in enumerate(((ya_ref, wa_ref), (yb_ref, wb_ref), (yc_ref, wc_ref))):
        br = jnp.dot(y_ref[...], w_ref[...], preferred_element_type=_F32)
        term = gate_ref[:, n * D_MODEL:(n + 1) * D_MODEL].astype(_F32) * br
        merged = term if merged is None else merged + term
    o_ref[...] = x_ref[...] + jnp.dot(merged.astype(_BF), wo_ref[...], preferred_element_type=_F32)


def _merge(x, ya, yb, yc, gates, wa, wb, wc, wo):
    t = x.shape[0]
    tm = TOKEN_TILE
    row = lambda i: (i, 0)
    const = lambda i: (0, 0)
    est = 5 * D_MODEL * D_MODEL * 2 + 2 * tm * (2 * D_MODEL * 4 + 3 * 512 * 2 + 3 * D_MODEL * 2) \
        + 6 * tm * D_MODEL * 4 + (8 << 20)
    return pl.pallas_call(
        _merge_kernel,
        grid=(t // tm,),
        in_specs=[
            pl.BlockSpec((tm, D_MODEL), row),
            pl.BlockSpec((tm, A_V), row),
            pl.BlockSpec((tm, B_QKV), row),
            pl.BlockSpec((tm, C_V), row),
            pl.BlockSpec((tm, N_BRANCH * D_MODEL), row),
            _resident((A_V, D_MODEL), const),
            _resident((B_QKV, D_MODEL), const),
            _resident((C_V, D_MODEL), const),
            _resident((D_MODEL, D_MODEL), const),
        ],
        out_specs=pl.BlockSpec((tm, D_MODEL), row),
        out_shape=jax.ShapeDtypeStruct((t, D_MODEL), _F32),
        compiler_params=pltpu.CompilerParams(
            dimension_semantics=("parallel",), vmem_limit_bytes=_vmem_limit(est)),
        name="merge",
    )(x, ya, yb, yc, gates, wa, wb, wc, wo)


def _rope_tables(seq):
    half = D_HEAD // 2
    inv = ROPE_THETA ** (-jnp.arange(half, dtype=_F32) / half)
    ang = jnp.arange(seq, dtype=_F32)[:, None] * inv[None, :]
    cos, sin = jnp.cos(ang), jnp.sin(ang)
    cos_t = jnp.concatenate([cos, cos] * (LANES // D_HEAD), axis=1)
    sin_t = jnp.concatenate([-sin, sin] * (LANES // D_HEAD), axis=1)
    return cos_t, sin_t


def _head_mean_matrix():
    g = np.kron(np.eye(MXU_DIM // D_HEAD), np.full((D_HEAD, D_HEAD), 1.0 / D_HEAD))
    return jnp.asarray(g, dtype=_BF)


def kernel(x, ffn1_norm, ffn1_w_gate, ffn1_w_up, ffn1_w_down, mix_norm, w_in, a_q_norm, a_k_norm, a_lambda_q1, a_lambda_k1, a_lambda_q2, a_lambda_k2, a_subln, b_q_norm, b_k_norm, b_rel_bias, c_out_norm, w_branch_a, w_branch_b, w_branch_c, w_out, ffn2_norm, ffn2_w_gate, ffn2_w_up, ffn2_w_down):
    batch, seq, d = x.shape
    assert d == D_MODEL and seq % ATTN_BLOCK == 0 and seq % RET_BLOCK == 0
    assert seq % TOKEN_TILE == 0 and seq >= B_WINDOW and ATTN_BLOCK == TOKEN_TILE == RET_BLOCK
    scale = D_HEAD ** -0.5
    cos_t, sin_t = _rope_tables(seq)
    gmat = _head_mean_matrix()
    bf = lambda w: w.astype(_BF)
    xt = x.reshape(batch * seq, D_MODEL)
    for l in range(DEPTH):
        lam_init = 0.8 - 0.6 * math.exp(-0.3 * l)
        xt = _ffn(xt, ffn1_norm[l][None, :], bf(ffn1_w_gate[l]), bf(ffn1_w_up[l]), bf(ffn1_w_down[l]))

        heads = A_QK // D_HEAD
        qk_gains = jnp.stack([
            jnp.tile(a_q_norm[l].astype(_F32) * scale, heads),
            jnp.tile(a_k_norm[l].astype(_F32), heads),
            jnp.tile(b_q_norm[l].astype(_F32) * scale, heads),
            jnp.tile(b_k_norm[l].astype(_F32), heads),
        ])
        aq, ak, avt, bq, bk, bv, cq, ckt, cv, cg, gates = _in_proj(
            xt, mix_norm[l][None, :], bf(w_in[l]), cos_t, sin_t, qk_gains, gmat, seq)

        lamv = jnp.stack([a_lambda_q1[l], a_lambda_k1[l], a_lambda_q2[l], a_lambda_k2[l]]).astype(_F32)
        ya = _diff_attn(aq, ak, avt, lamv, a_subln[l].astype(_F32)[:, None], batch, seq, lam_init)
        yb = _chunk_attn(bq, bk, bv, _bias_generator(b_rel_bias[l]), batch, seq)
        yc = _retention(cq, ckt, cv, cg, c_out_norm[l].astype(_F32)[:, None, :], batch, seq)

        xt = _merge(xt, ya, yb, yc, gates, bf(w_branch_a[l]), bf(w_branch_b[l]), bf(w_branch_c[l]),
                    bf(w_out[l]))
        xt = _ffn(xt, ffn2_norm[l][None, :], bf(ffn2_w_gate[l]), bf(ffn2_w_up[l]), bf(ffn2_w_down[l]))
    return xt.reshape(batch, seq, D_MODEL)
```

```python
import functools
import math

import numpy as np
import jax
import jax.numpy as jnp
from jax import lax
from jax.experimental import pallas as pl
from jax.experimental.pallas import tpu as pltpu

D_MODEL = 1024
DEPTH = 2
CHUNK = 64
D_HEAD = 64
ROPE_THETA = 10000.0
EPS = 1e-6
A_HEADS = 4
A_DV = 2 * D_HEAD
B_HEADS = 8
B_LOOKBACK = 8
B_MAX_REL = 256
C_HEADS = 4
C_DV = 2 * D_HEAD
N_BRANCH = 3
D_FF = 2816

A_QK = A_HEADS * 2 * D_HEAD
A_V = A_HEADS * A_DV
B_QKV = B_HEADS * D_HEAD
C_QK = C_HEADS * D_HEAD
C_V = C_HEADS * C_DV
SPLITS = (A_QK, A_QK, A_V, B_QKV, B_QKV, B_QKV, C_QK, C_QK, C_V, C_V, N_BRANCH * D_MODEL)
IN_COLS = sum(SPLITS)
_OFFS = tuple(int(v) for v in np.cumsum((0,) + SPLITS))

LANES = 128
MXU_DIM = 256
VMEM_BYTES_V7X = 64 * 1024 * 1024
VMEM_HEADROOM = 6 * 1024 * 1024

TOKEN_TILE = 512
FF_CHUNK = MXU_DIM
PROJ_CHUNK = 512
ATTN_BLOCK = 512
B_QTILE = 256
B_WINDOW = B_QTILE + B_LOOKBACK * CHUNK
RET_BLOCK = 512
NEG_BIG = -1e30

_BF = jnp.bfloat16
_F32 = jnp.float32
_NT = (((1,), (1,)), ((), ()))


def _vmem_limit(estimate_bytes):
    return int(min(VMEM_BYTES_V7X - VMEM_HEADROOM, max(32 * 1024 * 1024, estimate_bytes)))


def _resident(shape, index_map):
    return pl.BlockSpec(shape, index_map, pipeline_mode=pl.Buffered(1))


def _rms_rows(x, gain):
    ms = jnp.mean(x * x, axis=-1, keepdims=True)
    return x * lax.rsqrt(ms + EPS) * gain


def _ffn_kernel(x_ref, gain_ref, wg_ref, wu_ref, wd_ref, o_ref, acc_ref):
    x = x_ref[...]
    h = _rms_rows(x, gain_ref[...]).astype(_BF)
    for c in range(D_FF // FF_CHUNK):
        sl = slice(c * FF_CHUNK, (c + 1) * FF_CHUNK)
        g = jnp.dot(h, wg_ref[:, sl], preferred_element_type=_F32)
        u = jnp.dot(h, wu_ref[:, sl], preferred_element_type=_F32)
        a = (g * jax.nn.sigmoid(g) * u).astype(_BF)
        part = jnp.dot(a, wd_ref[sl, :], preferred_element_type=_F32)
        if c == 0:
            acc_ref[...] = part
        else:
            acc_ref[...] += part
    o_ref[...] = x + 0.5 * acc_ref[...]


def _ffn(x, gain, wg, wu, wd):
    t = x.shape[0]
    tm = TOKEN_TILE
    est = 3 * D_MODEL * D_FF * 2 + 5 * tm * D_MODEL * 4 + 4 * tm * FF_CHUNK * 4 + (8 << 20)
    return pl.pallas_call(
        _ffn_kernel,
        grid=(t // tm,),
        in_specs=[
            pl.BlockSpec((tm, D_MODEL), lambda i: (i, 0)),
            _resident((1, D_MODEL), lambda i: (0, 0)),
            _resident((D_MODEL, D_FF), lambda i: (0, 0)),
            _resident((D_MODEL, D_FF), lambda i: (0, 0)),
            _resident((D_FF, D_MODEL), lambda i: (0, 0)),
        ],
        out_specs=pl.BlockSpec((tm, D_MODEL), lambda i: (i, 0)),
        out_shape=jax.ShapeDtypeStruct((t, D_MODEL), _F32),
        scratch_shapes=[pltpu.VMEM((tm, D_MODEL), _F32)],
        compiler_params=pltpu.CompilerParams(
            dimension_semantics=("parallel",), vmem_limit_bytes=_vmem_limit(est)),
        name="ffn",
    )(x, gain, wg, wu, wd)


def _swap_halves(x):
    lane = lax.broadcasted_iota(jnp.int32, (1, LANES), 1)
    upper = (lane & (D_HEAD // 2)) != 0
    outs = []
    for s in range(x.shape[1] // LANES):
        xs = x[:, s * LANES:(s + 1) * LANES]
        from_below = pltpu.roll(xs, D_HEAD // 2, 1)
        from_above = pltpu.roll(xs, LANES - D_HEAD // 2, 1)
        outs.append(jnp.where(upper, from_below, from_above))
    return outs[0] if len(outs) == 1 else jnp.concatenate(outs, axis=1)


def _tile_lanes(t, n):
    reps = n // t.shape[1]
    return t if reps == 1 else jnp.concatenate([t] * reps, axis=1)


def _rope(x, cos, sin):
    n = x.shape[1]
    return x * _tile_lanes(cos, n) + _swap_halves(x) * _tile_lanes(sin, n)


def _head_ms(p, gmat):
    sq = (p * p).astype(_BF)
    outs = [jnp.dot(sq[:, s:s + MXU_DIM], gmat, preferred_element_type=_F32)
            for s in range(0, p.shape[1], MXU_DIM)]
    return outs[0] if len(outs) == 1 else jnp.concatenate(outs, axis=1)


def _in_proj_kernel(x_ref, gain_ref, w_ref, cos_ref, sin_ref, qkg_ref, gmat_ref,
                    aq_ref, ak_ref, avt_ref, bq_ref, bk_ref, bv_ref,
                    cq_ref, ckt_ref, cv_ref, cg_ref, gate_ref):
    h = _rms_rows(x_ref[...], gain_ref[...]).astype(_BF)
    cos = cos_ref[...]
    sin = sin_ref[...]
    gmat = gmat_ref[...]

    def proj(lo, width):
        return jnp.dot(h, w_ref[:, lo:lo + width], preferred_element_type=_F32)

    def normed(p, row):
        return p * lax.rsqrt(_head_ms(p, gmat) + EPS) * qkg_ref[row:row + 1, :]

    aq_ref[...] = _rope(normed(proj(_OFFS[0], A_QK), 0), cos, sin).astype(_BF)
    ak_ref[...] = _rope(normed(proj(_OFFS[1], A_QK), 1), cos, sin).astype(_BF)
    avt_ref[0] = proj(_OFFS[2], A_V).T.astype(_BF)
    bq_ref[...] = normed(proj(_OFFS[3], B_QKV), 2).astype(_BF)
    bk_ref[...] = normed(proj(_OFFS[4], B_QKV), 3).astype(_BF)
    bv_ref[...] = proj(_OFFS[5], B_QKV).astype(_BF)
    cq_ref[...] = _rope(proj(_OFFS[6], C_QK), cos, sin).astype(_BF)
    ckt_ref[0] = (_rope(proj(_OFFS[7], C_QK), cos, sin) * (D_HEAD ** -0.5)).T.astype(_BF)
    cv_ref[...] = proj(_OFFS[8], C_V).astype(_BF)
    cg_ref[...] = proj(_OFFS[9], C_V).astype(_BF)
    for c in range(N_BRANCH * D_MODEL // PROJ_CHUNK):
        lo = c * PROJ_CHUNK
        gate_ref[:, lo:lo + PROJ_CHUNK] = jax.nn.sigmoid(
            proj(_OFFS[10] + lo, PROJ_CHUNK)).astype(_BF)


def _in_proj(x, gain, w, cos_t, sin_t, qk_gains, gmat, seq):
    t = x.shape[0]
    tm = TOKEN_TILE
    nt = t // tm
    pos_blocks = seq // tm
    row = lambda i: (i, 0)
    tok_major = lambda n: pl.BlockSpec((tm, n), row)
    out_shapes = (
        jax.ShapeDtypeStruct((t, A_QK), _BF),
        jax.ShapeDtypeStruct((t, A_QK), _BF),
        jax.ShapeDtypeStruct((nt, A_V, tm), _BF),
        jax.ShapeDtypeStruct((t, B_QKV), _BF),
        jax.ShapeDtypeStruct((t, B_QKV), _BF),
        jax.ShapeDtypeStruct((t, B_QKV), _BF),
        jax.ShapeDtypeStruct((t, C_QK), _BF),
        jax.ShapeDtypeStruct((nt, C_QK, tm), _BF),
        jax.ShapeDtypeStruct((t, C_V), _BF),
        jax.ShapeDtypeStruct((t, C_V), _BF),
        jax.ShapeDtypeStruct((t, N_BRANCH * D_MODEL), _BF),
    )
    out_specs = (
        tok_major(A_QK), tok_major(A_QK),
        pl.BlockSpec((1, A_V, tm), lambda i: (i, 0, 0)),
        tok_major(B_QKV), tok_major(B_QKV), tok_major(B_QKV),
        tok_major(C_QK),
        pl.BlockSpec((1, C_QK, tm), lambda i: (i, 0, 0)),
        tok_major(C_V), tok_major(C_V), tok_major(N_BRANCH * D_MODEL),
    )
    est = D_MODEL * IN_COLS * 2 + 2 * tm * IN_COLS * 2 + 4 * tm * D_MODEL * 4 + (12 << 20)
    return pl.pallas_call(
        _in_proj_kernel,
        grid=(nt,),
        in_specs=[
            pl.BlockSpec((tm, D_MODEL), row),
            _resident((1, D_MODEL), lambda i: (0, 0)),
            _resident((D_MODEL, IN_COLS), lambda i: (0, 0)),
            pl.BlockSpec((tm, LANES), lambda i: (i % pos_blocks, 0)),
            pl.BlockSpec((tm, LANES), lambda i: (i % pos_blocks, 0)),
            _resident((4, A_QK), lambda i: (0, 0)),
            _resident((MXU_DIM, MXU_DIM), lambda i: (0, 0)),
        ],
        out_specs=out_specs,
        out_shape=out_shapes,
        compiler_params=pltpu.CompilerParams(
            dimension_semantics=("parallel",), vmem_limit_bytes=_vmem_limit(est)),
        name="in_proj",
    )(x, gain, w, cos_t, sin_t, qk_gains, gmat)


LOG2E = math.log2(math.e)
SAFE_LOG2_SCORE = 60.0
NORM_MARGIN = 1.05


def _diff_attn_kernel(bound_ref, q_ref, k_ref, vt_ref, lamv_ref, subln_ref, o_ref, acc_ref, *,
                      lam_init):
    blk = ATTN_BLOCK
    i = pl.program_id(2)
    lane = lax.broadcasted_iota(jnp.int32, (1, LANES), 1)
    q = q_ref[...]
    zero = jnp.zeros_like(q)
    qs = (jnp.where(lane < D_HEAD, q, zero), jnp.where(lane >= D_HEAD, q, zero))

    def scores(j, c, diagonal):
        kb = k_ref[pl.ds(pl.multiple_of(j * blk, blk), blk), :]
        st = lax.dot_general(kb, qs[c], _NT, preferred_element_type=_F32)
        if diagonal:
            key_chunk = lax.broadcasted_iota(jnp.int32, (blk, blk), 0) // CHUNK
            qry_chunk = lax.broadcasted_iota(jnp.int32, (blk, blk), 1) // CHUNK
            st = jnp.where(key_chunk <= qry_chunk, st, NEG_BIG)
        return st

    def plain_step(j, carry, diagonal):
        out = []
        for c in range(2):
            p = jnp.exp2(scores(j, c, diagonal))
            acc_ref[c] += jnp.dot(vt_ref[j], p.astype(_BF), preferred_element_type=_F32)
            out.append(carry[c] + jnp.sum(p, axis=0, keepdims=True))
        return tuple(out)

    def shifted_step(j, carry, diagonal):
        out = []
        for c in range(2):
            m_old, l_old = carry[c]
            st = scores(j, c, diagonal)
            m_new = jnp.maximum(m_old, jnp.max(st, axis=0, keepdims=True))
            alpha = jnp.exp2(m_old - m_new)
            p = jnp.exp2(st - m_new)
            l_new = alpha * l_old + jnp.sum(p, axis=0, keepdims=True)
            pv = jnp.dot(vt_ref[j], p.astype(_BF), preferred_element_type=_F32)
            acc_ref[c] = alpha * acc_ref[c] + pv
            out.append((m_new, l_new))
        return tuple(out)

    def finish(l0, l1):
        lamv = lamv_ref[...]
        lam = (jnp.exp(jnp.sum(lamv[0:1] * lamv[1:2], axis=-1, keepdims=True))
               - jnp.exp(jnp.sum(lamv[2:3] * lamv[3:4], axis=-1, keepdims=True)) + lam_init)
        o = acc_ref[0] / l0 - lam * (acc_ref[1] / l1)
        ms = jnp.mean(o * o, axis=0, keepdims=True)
        y = o * lax.rsqrt(ms + EPS) * subln_ref[...] * (1.0 - lam_init)
        o_ref[...] = y.T.astype(_BF)

    acc_ref[...] = jnp.zeros_like(acc_ref)
    bounded = bound_ref[0] <= SAFE_LOG2_SCORE

    @pl.when(bounded)
    def _():
        init = (jnp.zeros((1, blk), _F32), jnp.zeros((1, blk), _F32))
        carry = lax.fori_loop(0, i, lambda j, c: plain_step(j, c, False), init)
        finish(*plain_step(i, carry, True))

    @pl.when(jnp.logical_not(bounded))
    def _():
        init = tuple((jnp.full((1, blk), NEG_BIG, _F32), jnp.zeros((1, blk), _F32))
                     for _ in range(2))
        carry = lax.fori_loop(0, i, lambda j, c: shifted_step(j, c, False), init)
        (_, l0), (_, l1) = shifted_step(i, carry, True)
        finish(l0, l1)


def _diff_attn(score_bound, aq, ak, avt, lamv, subln_col, batch, seq, lam_init):
    blk = ATTN_BLOCK
    nq = seq // blk
    t = batch * seq
    est = 2 * (seq * LANES * 2) * 2 + 16 * blk * blk * 4 + (8 << 20)
    return pl.pallas_call(
        functools.partial(_diff_attn_kernel, lam_init=lam_init),
        grid=(batch, A_HEADS, nq),
        in_specs=[
            pl.BlockSpec(memory_space=pltpu.SMEM),
            pl.BlockSpec((blk, LANES), lambda b, h, i: (b * nq + i, h)),
            pl.BlockSpec((seq, LANES), lambda b, h, i: (b, h)),
            pl.BlockSpec((nq, A_DV, blk), lambda b, h, i: (b, h, 0)),
            pl.BlockSpec((4, D_HEAD), lambda b, h, i: (0, 0)),
            pl.BlockSpec((A_DV, 1), lambda b, h, i: (0, 0)),
        ],
        out_specs=pl.BlockSpec((blk, LANES), lambda b, h, i: (b * nq + i, h)),
        out_shape=jax.ShapeDtypeStruct((t, A_V), _BF),
        scratch_shapes=[pltpu.VMEM((2, A_DV, blk), _F32)],
        compiler_params=pltpu.CompilerParams(
            dimension_semantics=("parallel", "parallel", "parallel"),
            vmem_limit_bytes=_vmem_limit(est)),
        name="diff_attn",
    )(score_bound, aq, ak, avt, lamv, subln_col)


def _score_bound(q_gain, k_gain):
    return (D_HEAD * NORM_MARGIN * jnp.max(jnp.abs(q_gain)) * jnp.max(jnp.abs(k_gain))).reshape(1)


B_TABLES = (B_WINDOW - B_QTILE) // B_QTILE + 1
B_GEN_LEN = 2048


def _build_bias_tables(gen_ref, mb_ref):
    qchunk = lax.broadcasted_iota(jnp.int32, (B_QTILE, B_WINDOW), 0) // CHUNK
    kchunk = lax.broadcasted_iota(jnp.int32, (B_QTILE, B_WINDOW), 1) // CHUNK
    for hh in range(2):
        rolled = pltpu.roll(jnp.broadcast_to(gen_ref[hh], (B_QTILE, B_GEN_LEN)), 0, 1,
                            stride=1, stride_axis=0)
        for t in range(B_TABLES):
            lo = B_WINDOW - t * B_QTILE
            dchunk = qchunk - kchunk + (t * B_QTILE) // CHUNK
            visible = (dchunk >= 0) & (dchunk <= B_LOOKBACK)
            mb_ref[hh, t] = jnp.where(visible, rolled[:, lo:lo + B_WINDOW], NEG_BIG)


def _chunk_attn_kernel(q_ref, k_ref, v_ref, gen_ref, o_ref, mb_ref):
    i = pl.program_id(2)

    @pl.when(i == 0)
    def _():
        _build_bias_tables(gen_ref, mb_ref)

    start = pl.multiple_of(jnp.maximum(i * B_QTILE - (B_WINDOW - B_QTILE), 0), B_QTILE)
    table = jnp.minimum(i, B_TABLES - 1)
    lane = lax.broadcasted_iota(jnp.int32, (1, LANES), 1)
    q = q_ref[...]
    zero = jnp.zeros_like(q)
    kw = k_ref[pl.ds(start, B_WINDOW), :]
    vw = v_ref[pl.ds(start, B_WINDOW), :]
    outs = []
    for hh in range(2):
        in_head = (lane >= D_HEAD) if hh else (lane < D_HEAD)
        s = lax.dot_general(jnp.where(in_head, q, zero), kw, _NT, preferred_element_type=_F32)
        s = s + mb_ref[hh, table]
        m = jnp.max(s, axis=-1, keepdims=True)
        p = jnp.exp(s - m)
        l = jnp.sum(p, axis=-1, keepdims=True)
        outs.append(jnp.dot(p.astype(_BF), vw, preferred_element_type=_F32) / l)
    o_ref[...] = jnp.where(lane < D_HEAD, outs[0], outs[1]).astype(_BF)


def _chunk_attn(bq, bk, bv, bias_gen, batch, seq):
    nq = seq // B_QTILE
    t = batch * seq
    est = 2 * B_TABLES * B_QTILE * B_WINDOW * 4 + 4 * B_QTILE * B_GEN_LEN * 4 \
        + 4 * seq * LANES * 2 + (10 << 20)
    return pl.pallas_call(
        _chunk_attn_kernel,
        grid=(batch, B_HEADS // 2, nq),
        in_specs=[
            pl.BlockSpec((B_QTILE, LANES), lambda b, h, i: (b * nq + i, h)),
            pl.BlockSpec((seq, LANES), lambda b, h, i: (b, h)),
            pl.BlockSpec((seq, LANES), lambda b, h, i: (b, h)),
            pl.BlockSpec((2, 1, B_GEN_LEN), lambda b, h, i: (h, 0, 0)),
        ],
        out_specs=pl.BlockSpec((B_QTILE, LANES), lambda b, h, i: (b * nq + i, h)),
        out_shape=jax.ShapeDtypeStruct((t, B_QKV), _BF),
        scratch_shapes=[pltpu.VMEM((2, B_TABLES, B_QTILE, B_WINDOW), _F32)],
        compiler_params=pltpu.CompilerParams(
            dimension_semantics=("parallel", "parallel", "arbitrary"),
            vmem_limit_bytes=_vmem_limit(est)),
        name="chunk_attn",
    )(bq, bk, bv, bias_gen)


def _bias_generator(rel_bias):
    b = rel_bias.astype(_F32)
    n_far = B_WINDOW - B_MAX_REL + 1
    mid = b[:, 1:CHUNK - 1 + B_MAX_REL][:, ::-1]
    n_neg = B_GEN_LEN - n_far - mid.shape[1]
    heads = b.shape[0]
    gen = jnp.concatenate([
        jnp.broadcast_to(b[:, -1:], (heads, n_far)), mid,
        jnp.broadcast_to(b[:, :1], (heads, n_neg))], axis=1)
    return gen[:, None, :]


def _retention_kernel(q_ref, kt_ref, v_ref, g_ref, cn_ref, o_ref, decay_ref, *, seq):
    blk = RET_BLOCK
    h = pl.program_id(1)
    hf = jnp.full((1, 1), h, jnp.int32).astype(_F32)
    log_gamma = jnp.log(1.0 - jnp.exp2(-5.0 - hf))
    diff = (lax.broadcasted_iota(jnp.int32, (blk, blk), 0)
            - lax.broadcasted_iota(jnp.int32, (blk, blk), 1)).astype(_F32)
    decay_ref[...] = jnp.where(diff >= 0, jnp.exp(log_gamma * jnp.maximum(diff, 0.0)), 0.0)
    pos = lax.broadcasted_iota(jnp.int32, (blk, LANES), 0).astype(_F32)
    xi = jnp.exp(log_gamma * (pos + 1.0))
    zeta = jnp.exp(log_gamma * (blk - 1.0 - pos))
    block_decay = jnp.exp(log_gamma * float(blk))
    odd = (h % 2) == 1
    q_keep = (lax.broadcasted_iota(jnp.int32, (1, LANES), 1) >= D_HEAD) == odd
    k_keep = (lax.broadcasted_iota(jnp.int32, (LANES, 1), 0) >= D_HEAD) == odd
    gain = cn_ref[0]

    def body(j, state):
        rows = pl.ds(pl.multiple_of(j * blk, blk), blk)
        qb = q_ref[rows, :]
        qb = jnp.where(q_keep, qb, jnp.zeros_like(qb))
        kt = kt_ref[j]
        kt = jnp.where(k_keep, kt, jnp.zeros_like(kt))
        vb = v_ref[rows, :]
        sc = jnp.dot(qb, kt, preferred_element_type=_F32) * decay_ref[...]
        inner = jnp.dot(sc.astype(_BF), vb, preferred_element_type=_F32)
        cross = jnp.dot(qb, state.astype(_BF), preferred_element_type=_F32) * xi
        o = inner + cross
        vz = (vb.astype(_F32) * zeta).astype(_BF)
        new_state = state * block_decay + jnp.dot(kt, vz, preferred_element_type=_F32)
        ms = jnp.mean(o * o, axis=-1, keepdims=True)
        y = o * lax.rsqrt(ms + EPS) * gain
        gg = g_ref[rows, :].astype(_F32)
        o_ref[rows, :] = (gg * jax.nn.sigmoid(gg) * y).astype(_BF)
        return new_state

    lax.fori_loop(0, seq // blk, body, jnp.zeros((LANES, C_DV), _F32))


def _retention(cq, ckt, cv, cg, cnorm, batch, seq):
    blk = RET_BLOCK
    nb = seq // blk
    t = batch * seq
    est = 2 * 5 * seq * LANES * 2 + 6 * blk * blk * 4 + (8 << 20)
    return pl.pallas_call(
        functools.partial(_retention_kernel, seq=seq),
        grid=(batch, C_HEADS),
        in_specs=[
            pl.BlockSpec((seq, LANES), lambda b, h: (b, h // 2)),
            pl.BlockSpec((nb, LANES, blk), lambda b, h: (b, h // 2, 0)),
            pl.BlockSpec((seq, C_DV), lambda b, h: (b, h)),
            pl.BlockSpec((seq, C_DV), lambda b, h: (b, h)),
            pl.BlockSpec((1, 1, C_DV), lambda b, h: (h, 0, 0)),
        ],
        out_specs=pl.BlockSpec((seq, C_DV), lambda b, h: (b, h)),
        out_shape=jax.ShapeDtypeStruct((t, C_V), _BF),
        scratch_shapes=[pltpu.VMEM((blk, blk), _F32)],
        compiler_params=pltpu.CompilerParams(
            dimension_semantics=("parallel", "parallel"), vmem_limit_bytes=_vmem_limit(est)),
        name="retention",
    )(cq, ckt, cv, cg, cnorm)


def _merge_kernel(x_ref, ya_ref, yb_ref, yc_ref, gate_ref, wa_ref, wb_ref, wc_ref, wo_ref, o_ref):
    merged = None
    for n, (y_ref, w_ref) in enumerate(((ya_ref, wa_ref), (yb_ref, wb_ref), (yc_ref, wc_ref))):
        br = jnp.dot(y_ref[...], w_ref[...], preferred_element_type=_F32)
        term = gate_ref[:, n * D_MODEL:(n + 1) * D_MODEL].astype(_F32) * br
        merged = term if merged is None else merged + term
    o_ref[...] = x_ref[...] + jnp.dot(merged.astype(_BF), wo_ref[...], preferred_element_type=_F32)


def _merge(x, ya, yb, yc, gates, wa, wb, wc, wo):
    t = x.shape[0]
    tm = TOKEN_TILE
    row = lambda i: (i, 0)
    const = lambda i: (0, 0)
    est = 5 * D_MODEL * D_MODEL * 2 + 2 * tm * (2 * D_MODEL * 4 + 3 * 512 * 2 + 3 * D_MODEL * 2) \
        + 6 * tm * D_MODEL * 4 + (8 << 20)
    return pl.pallas_call(
        _merge_kernel,
        grid=(t // tm,),
        in_specs=[
            pl.BlockSpec((tm, D_MODEL), row),
            pl.BlockSpec((tm, A_V), row),
            pl.BlockSpec((tm, B_QKV), row),
            pl.BlockSpec((tm, C_V), row),
            pl.BlockSpec((tm, N_BRANCH * D_MODEL), row),
            _resident((A_V, D_MODEL), const),
            _resident((B_QKV, D_MODEL), const),
            _resident((C_V, D_MODEL), const),
            _resident((D_MODEL, D_MODEL), const),
        ],
        out_specs=pl.BlockSpec((tm, D_MODEL), row),
        out_shape=jax.ShapeDtypeStruct((t, D_MODEL), _F32),
        compiler_params=pltpu.CompilerParams(
            dimension_semantics=("parallel",), vmem_limit_bytes=_vmem_limit(est)),
        name="merge",
    )(x, ya, yb, yc, gates, wa, wb, wc, wo)


def _rope_tables(seq):
    half = D_HEAD // 2
    inv = ROPE_THETA ** (-jnp.arange(half, dtype=_F32) / half)
    ang = jnp.arange(seq, dtype=_F32)[:, None] * inv[None, :]
    cos, sin = jnp.cos(ang), jnp.sin(ang)
    cos_t = jnp.concatenate([cos, cos] * (LANES // D_HEAD), axis=1)
    sin_t = jnp.concatenate([-sin, sin] * (LANES // D_HEAD), axis=1)
    return cos_t, sin_t


def _head_mean_matrix():
    g = np.kron(np.eye(MXU_DIM // D_HEAD), np.full((D_HEAD, D_HEAD), 1.0 / D_HEAD))
    return jnp.asarray(g, dtype=_BF)


def kernel(x, ffn1_norm, ffn1_w_gate, ffn1_w_up, ffn1_w_down, mix_norm, w_in, a_q_norm, a_k_norm, a_lambda_q1, a_lambda_k1, a_lambda_q2, a_lambda_k2, a_subln, b_q_norm, b_k_norm, b_rel_bias, c_out_norm, w_branch_a, w_branch_b, w_branch_c, w_out, ffn2_norm, ffn2_w_gate, ffn2_w_up, ffn2_w_down):
    batch, seq, d = x.shape
    assert d == D_MODEL and seq % ATTN_BLOCK == 0 and seq % RET_BLOCK == 0
    assert seq % TOKEN_TILE == 0 and seq >= B_WINDOW and ATTN_BLOCK == TOKEN_TILE == RET_BLOCK
    scale = D_HEAD ** -0.5
    cos_t, sin_t = _rope_tables(seq)
    gmat = _head_mean_matrix()
    bf = lambda w: w.astype(_BF)
    xt = x.reshape(batch * seq, D_MODEL)
    for l in range(DEPTH):
        lam_init = 0.8 - 0.6 * math.exp(-0.3 * l)
        xt = _ffn(xt, ffn1_norm[l][None, :], bf(ffn1_w_gate[l]), bf(ffn1_w_up[l]), bf(ffn1_w_down[l]))

        heads = A_QK // D_HEAD
        qk_gains = jnp.stack([
            jnp.tile(a_q_norm[l].astype(_F32) * (scale * LOG2E), heads),
            jnp.tile(a_k_norm[l].astype(_F32), heads),
            jnp.tile(b_q_norm[l].astype(_F32) * scale, heads),
            jnp.tile(b_k_norm[l].astype(_F32), heads),
        ])
        aq, ak, avt, bq, bk, bv, cq, ckt, cv, cg, gates = _in_proj(
            xt, mix_norm[l][None, :], bf(w_in[l]), cos_t, sin_t, qk_gains, gmat, seq)

        lamv = jnp.stack([a_lambda_q1[l], a_lambda_k1[l], a_lambda_q2[l], a_lambda_k2[l]]).astype(_F32)
        a_bound = _score_bound(qk_gains[0, :D_HEAD], qk_gains[1, :D_HEAD])
        ya = _diff_attn(a_bound, aq, ak, avt, lamv, a_subln[l].astype(_F32)[:, None], batch, seq, lam_init)
        yb = _chunk_attn(bq, bk, bv, _bias_generator(b_rel_bias[l]), batch, seq)
        yc = _retention(cq, ckt, cv, cg, c_out_norm[l].astype(_F32)[:, None, :], batch, seq)

        xt = _merge(xt, ya, yb, yc, gates, bf(w_branch_a[l]), bf(w_branch_b[l]), bf(w_branch_c[l]),
                    bf(w_out[l]))
        xt = _ffn(xt, ffn2_norm[l][None, :], bf(ffn2_w_gate[l]), bf(ffn2_w_up[l]), bf(ffn2_w_down[l]))
    return xt.reshape(batch, seq, D_MODEL)
```

```python
import functools
import math

import numpy as np
import jax
import jax.numpy as jnp
from jax import lax
from jax.experimental import pallas as pl
from jax.experimental.pallas import tpu as pltpu

D_MODEL = 1024
DEPTH = 2
CHUNK = 64
D_HEAD = 64
ROPE_THETA = 10000.0
EPS = 1e-6
A_HEADS = 4
A_DV = 2 * D_HEAD
B_HEADS = 8
B_LOOKBACK = 8
B_MAX_REL = 256
C_HEADS = 4
C_DV = 2 * D_HEAD
N_BRANCH = 3
D_FF = 2816

A_QK = A_HEADS * 2 * D_HEAD
A_V = A_HEADS * A_DV
B_QKV = B_HEADS * D_HEAD
C_QK = C_HEADS * D_HEAD
C_V = C_HEADS * C_DV
SPLITS = (A_QK, A_QK, A_V, B_QKV, B_QKV, B_QKV, C_QK, C_QK, C_V, C_V, N_BRANCH * D_MODEL)
IN_COLS = sum(SPLITS)
_OFFS = tuple(int(v) for v in np.cumsum((0,) + SPLITS))

LANES = 128
MXU_DIM = 256
VMEM_BYTES_V7X = 64 * 1024 * 1024
VMEM_HEADROOM = 6 * 1024 * 1024

TOKEN_TILE = 512
FF_CHUNK = MXU_DIM
PROJ_CHUNK = 512
ATTN_BLOCK = 512
B_QTILE = 256
B_WINDOW = B_QTILE + B_LOOKBACK * CHUNK
RET_BLOCK = 512
NEG_BIG = -1e30

_BF = jnp.bfloat16
_F32 = jnp.float32
_NT = (((1,), (1,)), ((), ()))


def _vmem_limit(estimate_bytes):
    return int(min(VMEM_BYTES_V7X - VMEM_HEADROOM, max(32 * 1024 * 1024, estimate_bytes)))


def _resident(shape, index_map):
    return pl.BlockSpec(shape, index_map, pipeline_mode=pl.Buffered(1))


def _rms_rows(x, gain):
    ms = jnp.mean(x * x, axis=-1, keepdims=True)
    return x * lax.rsqrt(ms + EPS) * gain


def _ffn_kernel(x_ref, gain_ref, wg_ref, wu_ref, wd_ref, o_ref, acc_ref):
    x = x_ref[...]
    h = _rms_rows(x, gain_ref[...]).astype(_BF)
    for c in range(D_FF // FF_CHUNK):
        sl = slice(c * FF_CHUNK, (c + 1) * FF_CHUNK)
        g = jnp.dot(h, wg_ref[:, sl], preferred_element_type=_F32)
        u = jnp.dot(h, wu_ref[:, sl], preferred_element_type=_F32)
        a = (g * jax.nn.sigmoid(g) * u).astype(_BF)
        part = jnp.dot(a, wd_ref[sl, :], preferred_element_type=_F32)
        if c == 0:
            acc_ref[...] = part
        else:
            acc_ref[...] += part
    o_ref[...] = x + 0.5 * acc_ref[...]


def _ffn(x, gain, wg, wu, wd):
    t = x.shape[0]
    tm = TOKEN_TILE
    est = 3 * D_MODEL * D_FF * 2 + 5 * tm * D_MODEL * 4 + 4 * tm * FF_CHUNK * 4 + (8 << 20)
    return pl.pallas_call(
        _ffn_kernel,
        grid=(t // tm,),
        in_specs=[
            pl.BlockSpec((tm, D_MODEL), lambda i: (i, 0)),
            _resident((1, D_MODEL), lambda i: (0, 0)),
            _resident((D_MODEL, D_FF), lambda i: (0, 0)),
            _resident((D_MODEL, D_FF), lambda i: (0, 0)),
            _resident((D_FF, D_MODEL), lambda i: (0, 0)),
        ],
        out_specs=pl.BlockSpec((tm, D_MODEL), lambda i: (i, 0)),
        out_shape=jax.ShapeDtypeStruct((t, D_MODEL), _F32),
        scratch_shapes=[pltpu.VMEM((tm, D_MODEL), _F32)],
        compiler_params=pltpu.CompilerParams(
            dimension_semantics=("parallel",), vmem_limit_bytes=_vmem_limit(est)),
        name="ffn",
    )(x, gain, wg, wu, wd)


def _swap_halves(x):
    lane = lax.broadcasted_iota(jnp.int32, (1, LANES), 1)
    upper = (lane & (D_HEAD // 2)) != 0
    outs = []
    for s in range(x.shape[1] // LANES):
        xs = x[:, s * LANES:(s + 1) * LANES]
        from_below = pltpu.roll(xs, D_HEAD // 2, 1)
        from_above = pltpu.roll(xs, LANES - D_HEAD // 2, 1)
        outs.append(jnp.where(upper, from_below, from_above))
    return outs[0] if len(outs) == 1 else jnp.concatenate(outs, axis=1)


def _tile_lanes(t, n):
    reps = n // t.shape[1]
    return t if reps == 1 else jnp.concatenate([t] * reps, axis=1)


def _rope(x, cos, sin):
    n = x.shape[1]
    return x * _tile_lanes(cos, n) + _swap_halves(x) * _tile_lanes(sin, n)


def _head_ms(p, gmat):
    sq = (p * p).astype(_BF)
    outs = [jnp.dot(sq[:, s:s + MXU_DIM], gmat, preferred_element_type=_F32)
            for s in range(0, p.shape[1], MXU_DIM)]
    return outs[0] if len(outs) == 1 else jnp.concatenate(outs, axis=1)


def _in_proj_kernel(x_ref, gain_ref, w_ref, cos_ref, sin_ref, qkg_ref, gmat_ref,
                    aq_ref, ak_ref, avt_ref, bq_ref, bk_ref, bv_ref,
                    cq_ref, ckt_ref, cv_ref, cg_ref, gate_ref):
    h = _rms_rows(x_ref[...], gain_ref[...]).astype(_BF)
    cos = cos_ref[...]
    sin = sin_ref[...]
    gmat = gmat_ref[...]

    def proj(lo, width):
        return jnp.dot(h, w_ref[:, lo:lo + width], preferred_element_type=_F32)

    def normed(p, row):
        return p * lax.rsqrt(_head_ms(p, gmat) + EPS) * qkg_ref[row:row + 1, :]

    aq_ref[...] = _rope(normed(proj(_OFFS[0], A_QK), 0), cos, sin).astype(_BF)
    ak_ref[...] = _rope(normed(proj(_OFFS[1], A_QK), 1), cos, sin).astype(_BF)
    avt_ref[0] = proj(_OFFS[2], A_V).T.astype(_BF)
    bq_ref[...] = normed(proj(_OFFS[3], B_QKV), 2).astype(_BF)
    bk_ref[...] = normed(proj(_OFFS[4], B_QKV), 3).astype(_BF)
    bv_ref[...] = proj(_OFFS[5], B_QKV).astype(_BF)
    cq_ref[...] = _rope(proj(_OFFS[6], C_QK), cos, sin).astype(_BF)
    ckt_ref[0] = (_rope(proj(_OFFS[7], C_QK), cos, sin) * (D_HEAD ** -0.5)).T.astype(_BF)
    cv_ref[...] = proj(_OFFS[8], C_V).astype(_BF)
    cg_ref[...] = proj(_OFFS[9], C_V).astype(_BF)
    for c in range(N_BRANCH * D_MODEL // PROJ_CHUNK):
        lo = c * PROJ_CHUNK
        gate_ref[:, lo:lo + PROJ_CHUNK] = jax.nn.sigmoid(
            proj(_OFFS[10] + lo, PROJ_CHUNK)).astype(_BF)


def _in_proj(x, gain, w, cos_t, sin_t, qk_gains, gmat, seq):
    t = x.shape[0]
    tm = TOKEN_TILE
    nt = t // tm
    pos_blocks = seq // tm
    row = lambda i: (i, 0)
    tok_major = lambda n: pl.BlockSpec((tm, n), row)
    out_shapes = (
        jax.ShapeDtypeStruct((t, A_QK), _BF),
        jax.ShapeDtypeStruct((t, A_QK), _BF),
        jax.ShapeDtypeStruct((nt, A_V, tm), _BF),
        jax.ShapeDtypeStruct((t, B_QKV), _BF),
        jax.ShapeDtypeStruct((t, B_QKV), _BF),
        jax.ShapeDtypeStruct((t, B_QKV), _BF),
        jax.ShapeDtypeStruct((t, C_QK), _BF),
        jax.ShapeDtypeStruct((nt, C_QK, tm), _BF),
        jax.ShapeDtypeStruct((t, C_V), _BF),
        jax.ShapeDtypeStruct((t, C_V), _BF),
        jax.ShapeDtypeStruct((t, N_BRANCH * D_MODEL), _BF),
    )
    out_specs = (
        tok_major(A_QK), tok_major(A_QK),
        pl.BlockSpec((1, A_V, tm), lambda i: (i, 0, 0)),
        tok_major(B_QKV), tok_major(B_QKV), tok_major(B_QKV),
        tok_major(C_QK),
        pl.BlockSpec((1, C_QK, tm), lambda i: (i, 0, 0)),
        tok_major(C_V), tok_major(C_V), tok_major(N_BRANCH * D_MODEL),
    )
    est = D_MODEL * IN_COLS * 2 + 2 * tm * IN_COLS * 2 + 4 * tm * D_MODEL * 4 + (12 << 20)
    return pl.pallas_call(
        _in_proj_kernel,
        grid=(nt,),
        in_specs=[
            pl.BlockSpec((tm, D_MODEL), row),
            _resident((1, D_MODEL), lambda i: (0, 0)),
            _resident((D_MODEL, IN_COLS), lambda i: (0, 0)),
            pl.BlockSpec((tm, LANES), lambda i: (i % pos_blocks, 0)),
            pl.BlockSpec((tm, LANES), lambda i: (i % pos_blocks, 0)),
            _resident((4, A_QK), lambda i: (0, 0)),
            _resident((MXU_DIM, MXU_DIM), lambda i: (0, 0)),
        ],
        out_specs=out_specs,
        out_shape=out_shapes,
        compiler_params=pltpu.CompilerParams(
            dimension_semantics=("parallel",), vmem_limit_bytes=_vmem_limit(est)),
        name="in_proj",
    )(x, gain, w, cos_t, sin_t, qk_gains, gmat)


LOG2E = math.log2(math.e)
SAFE_LOG2_SCORE = 60.0
NORM_MARGIN = 1.05
A_PAIR = 4


def _diff_attn_kernel(bound_ref, q_ref, k_ref, vt_ref, lamv_ref, subln_ref, o_ref, acc_ref, *,
                      lam_init):
    blk = ATTN_BLOCK
    i = pl.program_id(2)
    n_chain = 2 * A_PAIR
    per_group = MXU_DIM // D_HEAD
    lane = lax.broadcasted_iota(jnp.int32, (1, MXU_DIM), 1)
    qs = []
    for n in range(n_chain):
        g, r = divmod(n, per_group)
        qg = q_ref[:, g * MXU_DIM:(g + 1) * MXU_DIM]
        qs.append(jnp.where((lane >= D_HEAD * r) & (lane < D_HEAD * (r + 1)), qg,
                            jnp.zeros_like(qg)))

    def scores(j, n, diagonal):
        g = n // per_group
        kb = k_ref[pl.ds(pl.multiple_of(j * blk, blk), blk), g * MXU_DIM:(g + 1) * MXU_DIM]
        st = lax.dot_general(kb, qs[n], _NT, preferred_element_type=_F32)
        if diagonal:
            key_chunk = lax.broadcasted_iota(jnp.int32, (blk, blk), 0) // CHUNK
            qry_chunk = lax.broadcasted_iota(jnp.int32, (blk, blk), 1) // CHUNK
            st = jnp.where(key_chunk <= qry_chunk, st, NEG_BIG)
        return st

    def values_t(j, n):
        head = n // 2
        return vt_ref[j, head * A_DV:(head + 1) * A_DV, :]

    def plain_step(j, carry, diagonal):
        out = []
        st = scores(j, 0, diagonal)
        for n in range(n_chain):
            st_next = scores(j, n + 1, diagonal) if n + 1 < n_chain else None
            p = jnp.exp2(st)
            acc_ref[n] += jnp.dot(values_t(j, n), p.astype(_BF), preferred_element_type=_F32)
            out.append(carry[n] + jnp.sum(p, axis=0, keepdims=True))
            st = st_next
        return tuple(out)

    def shifted_step(j, carry, diagonal):
        out = []
        for n in range(n_chain):
            m_old, l_old = carry[n]
            st = scores(j, n, diagonal)
            m_new = jnp.maximum(m_old, jnp.max(st, axis=0, keepdims=True))
            alpha = jnp.exp2(m_old - m_new)
            p = jnp.exp2(st - m_new)
            l_new = alpha * l_old + jnp.sum(p, axis=0, keepdims=True)
            pv = jnp.dot(values_t(j, n), p.astype(_BF), preferred_element_type=_F32)
            acc_ref[n] = alpha * acc_ref[n] + pv
            out.append((m_new, l_new))
        return tuple(out)

    def finish(ls):
        lamv = lamv_ref[...]
        lam = (jnp.exp(jnp.sum(lamv[0:1] * lamv[1:2], axis=-1, keepdims=True))
               - jnp.exp(jnp.sum(lamv[2:3] * lamv[3:4], axis=-1, keepdims=True)) + lam_init)
        for head in range(A_PAIR):
            o = (acc_ref[2 * head] / ls[2 * head]
                 - lam * (acc_ref[2 * head + 1] / ls[2 * head + 1]))
            ms = jnp.mean(o * o, axis=0, keepdims=True)
            y = o * lax.rsqrt(ms + EPS) * subln_ref[...] * (1.0 - lam_init)
            o_ref[:, head * A_DV:(head + 1) * A_DV] = y.T.astype(_BF)

    acc_ref[...] = jnp.zeros_like(acc_ref)
    bounded = bound_ref[0] <= SAFE_LOG2_SCORE

    @pl.when(bounded)
    def _():
        init = tuple(jnp.zeros((1, blk), _F32) for _ in range(n_chain))
        carry = lax.fori_loop(0, i, lambda j, c: plain_step(j, c, False), init)
        finish(plain_step(i, carry, True))

    @pl.when(jnp.logical_not(bounded))
    def _():
        init = tuple((jnp.full((1, blk), NEG_BIG, _F32), jnp.zeros((1, blk), _F32))
                     for _ in range(n_chain))
        carry = lax.fori_loop(0, i, lambda j, c: shifted_step(j, c, False), init)
        finish([l for _, l in shifted_step(i, carry, True)])


def _diff_attn(score_bound, aq, ak, avt, lamv, subln_col, batch, seq, lam_init):
    blk = ATTN_BLOCK
    nq = seq // blk
    t = batch * seq
    width = A_PAIR * A_DV
    est = 2 * (seq * width * 2) * 2 + 2 * A_PAIR * (A_DV * blk * 4) + 16 * blk * blk * 4 + (8 << 20)
    return pl.pallas_call(
        functools.partial(_diff_attn_kernel, lam_init=lam_init),
        grid=(batch, A_HEADS // A_PAIR, nq),
        in_specs=[
            pl.BlockSpec(memory_space=pltpu.SMEM),
            pl.BlockSpec((blk, width), lambda b, h, i: (b * nq + i, h)),
            pl.BlockSpec((seq, width), lambda b, h, i: (b, h)),
            pl.BlockSpec((nq, width, blk), lambda b, h, i: (b, h, 0)),
            pl.BlockSpec((4, D_HEAD), lambda b, h, i: (0, 0)),
            pl.BlockSpec((A_DV, 1), lambda b, h, i: (0, 0)),
        ],
        out_specs=pl.BlockSpec((blk, width), lambda b, h, i: (b * nq + i, h)),
        out_shape=jax.ShapeDtypeStruct((t, A_V), _BF),
        scratch_shapes=[pltpu.VMEM((2 * A_PAIR, A_DV, blk), _F32)],
        compiler_params=pltpu.CompilerParams(
            dimension_semantics=("parallel", "parallel", "parallel"),
            vmem_limit_bytes=_vmem_limit(est)),
        name="diff_attn",
    )(score_bound, aq, ak, avt, lamv, subln_col)


def _score_bound(q_gain, k_gain):
    return (D_HEAD * NORM_MARGIN * jnp.max(jnp.abs(q_gain)) * jnp.max(jnp.abs(k_gain))).reshape(1)


B_TABLES = (B_WINDOW - B_QTILE) // B_QTILE + 1
B_GEN_LEN = 2048


def _build_bias_tables(gen_ref, mb_ref):
    qchunk = lax.broadcasted_iota(jnp.int32, (B_QTILE, B_WINDOW), 0) // CHUNK
    kchunk = lax.broadcasted_iota(jnp.int32, (B_QTILE, B_WINDOW), 1) // CHUNK
    for hh in range(2):
        rolled = pltpu.roll(jnp.broadcast_to(gen_ref[hh], (B_QTILE, B_GEN_LEN)), 0, 1,
                            stride=1, stride_axis=0)
        for t in range(B_TABLES):
            lo = B_WINDOW - t * B_QTILE
            dchunk = qchunk - kchunk + (t * B_QTILE) // CHUNK
            visible = (dchunk >= 0) & (dchunk <= B_LOOKBACK)
            mb_ref[hh, t] = jnp.where(visible, rolled[:, lo:lo + B_WINDOW], NEG_BIG)


def _chunk_attn_kernel(bound_ref, q_ref, k_ref, v_ref, gen_ref, o_ref, mb_ref):
    i = pl.program_id(2)

    @pl.when(i == 0)
    def _():
        _build_bias_tables(gen_ref, mb_ref)

    start = pl.multiple_of(jnp.maximum(i * B_QTILE - (B_WINDOW - B_QTILE), 0), B_QTILE)
    table = jnp.minimum(i, B_TABLES - 1)
    lane = lax.broadcasted_iota(jnp.int32, (1, LANES), 1)
    q = q_ref[...]
    zero = jnp.zeros_like(q)
    kw = k_ref[pl.ds(start, B_WINDOW), :]
    vw = v_ref[pl.ds(start, B_WINDOW), :]

    def attend(shifted):
        outs = []
        for hh in range(2):
            in_head = (lane >= D_HEAD) if hh else (lane < D_HEAD)
            s = lax.dot_general(jnp.where(in_head, q, zero), kw, _NT, preferred_element_type=_F32)
            s = s + mb_ref[hh, table]
            if shifted:
                s = s - jnp.max(s, axis=-1, keepdims=True)
            p = jnp.exp2(s)
            l = jnp.sum(p, axis=-1, keepdims=True)
            outs.append(jnp.dot(p.astype(_BF), vw, preferred_element_type=_F32) / l)
        o_ref[...] = jnp.where(lane < D_HEAD, outs[0], outs[1]).astype(_BF)

    bounded = bound_ref[0] <= SAFE_LOG2_SCORE
    pl.when(bounded)(lambda: attend(False))
    pl.when(jnp.logical_not(bounded))(lambda: attend(True))


def _chunk_attn(score_bound, bq, bk, bv, bias_gen, batch, seq):
    nq = seq // B_QTILE
    t = batch * seq
    est = 2 * B_TABLES * B_QTILE * B_WINDOW * 4 + 4 * B_QTILE * B_GEN_LEN * 4 \
        + 4 * seq * LANES * 2 + (10 << 20)
    return pl.pallas_call(
        _chunk_attn_kernel,
        grid=(batch, B_HEADS // 2, nq),
        in_specs=[
            pl.BlockSpec(memory_space=pltpu.SMEM),
            pl.BlockSpec((B_QTILE, LANES), lambda b, h, i: (b * nq + i, h)),
            pl.BlockSpec((seq, LANES), lambda b, h, i: (b, h)),
            pl.BlockSpec((seq, LANES), lambda b, h, i: (b, h)),
            pl.BlockSpec((2, 1, B_GEN_LEN), lambda b, h, i: (h, 0, 0)),
        ],
        out_specs=pl.BlockSpec((B_QTILE, LANES), lambda b, h, i: (b * nq + i, h)),
        out_shape=jax.ShapeDtypeStruct((t, B_QKV), _BF),
        scratch_shapes=[pltpu.VMEM((2, B_TABLES, B_QTILE, B_WINDOW), _F32)],
        compiler_params=pltpu.CompilerParams(
            dimension_semantics=("parallel", "parallel", "arbitrary"),
            vmem_limit_bytes=_vmem_limit(est)),
        name="chunk_attn",
    )(score_bound, bq, bk, bv, bias_gen)


def _bias_generator(rel_bias):
    b = rel_bias.astype(_F32) * LOG2E
    n_far = B_WINDOW - B_MAX_REL + 1
    mid = b[:, 1:CHUNK - 1 + B_MAX_REL][:, ::-1]
    n_neg = B_GEN_LEN - n_far - mid.shape[1]
    heads = b.shape[0]
    gen = jnp.concatenate([
        jnp.broadcast_to(b[:, -1:], (heads, n_far)), mid,
        jnp.broadcast_to(b[:, :1], (heads, n_neg))], axis=1)
    return gen[:, None, :]


def _retention_kernel(q_ref, kt_ref, v_ref, g_ref, cn_ref, o_ref, decay_ref, *, seq):
    blk = RET_BLOCK
    h = pl.program_id(1)
    hf = jnp.full((1, 1), h, jnp.int32).astype(_F32)
    log_gamma = jnp.log(1.0 - jnp.exp2(-5.0 - hf))
    diff = (lax.broadcasted_iota(jnp.int32, (blk, blk), 0)
            - lax.broadcasted_iota(jnp.int32, (blk, blk), 1)).astype(_F32)
    decay_ref[...] = jnp.where(diff >= 0, jnp.exp(log_gamma * jnp.maximum(diff, 0.0)), 0.0)
    pos = lax.broadcasted_iota(jnp.int32, (blk, LANES), 0).astype(_F32)
    xi = jnp.exp(log_gamma * (pos + 1.0))
    zeta = jnp.exp(log_gamma * (blk - 1.0 - pos))
    block_decay = jnp.exp(log_gamma * float(blk))
    odd = (h % 2) == 1
    q_keep = (lax.broadcasted_iota(jnp.int32, (1, LANES), 1) >= D_HEAD) == odd
    k_keep = (lax.broadcasted_iota(jnp.int32, (LANES, 1), 0) >= D_HEAD) == odd
    gain = cn_ref[0]

    def body(j, state):
        rows = pl.ds(pl.multiple_of(j * blk, blk), blk)
        qb = q_ref[rows, :]
        qb = jnp.where(q_keep, qb, jnp.zeros_like(qb))
        kt = kt_ref[j]
        kt = jnp.where(k_keep, kt, jnp.zeros_like(kt))
        vb = v_ref[rows, :]
        sc = jnp.dot(qb, kt, preferred_element_type=_F32) * decay_ref[...]
        inner = jnp.dot(sc.astype(_BF), vb, preferred_element_type=_F32)
        cross = jnp.dot(qb, state.astype(_BF), preferred_element_type=_F32) * xi
        o = inner + cross
        vz = (vb.astype(_F32) * zeta).astype(_BF)
        new_state = state * block_decay + jnp.dot(kt, vz, preferred_element_type=_F32)
        ms = jnp.mean(o * o, axis=-1, keepdims=True)
        y = o * lax.rsqrt(ms + EPS) * gain
        gg = g_ref[rows, :].astype(_F32)
        o_ref[rows, :] = (gg * jax.nn.sigmoid(gg) * y).astype(_BF)
        return new_state

    lax.fori_loop(0, seq // blk, body, jnp.zeros((LANES, C_DV), _F32))


def _retention(cq, ckt, cv, cg, cnorm, batch, seq):
    blk = RET_BLOCK
    nb = seq // blk
    t = batch * seq
    est = 2 * 5 * seq * LANES * 2 + 6 * blk * blk * 4 + (8 << 20)
    return pl.pallas_call(
        functools.partial(_retention_kernel, seq=seq),
        grid=(batch, C_HEADS),
        in_specs=[
            pl.BlockSpec((seq, LANES), lambda b, h: (b, h // 2)),
            pl.BlockSpec((nb, LANES, blk), lambda b, h: (b, h // 2, 0)),
            pl.BlockSpec((seq, C_DV), lambda b, h: (b, h)),
            pl.BlockSpec((seq, C_DV), lambda b, h: (b, h)),
            pl.BlockSpec((1, 1, C_DV), lambda b, h: (h, 0, 0)),
        ],
        out_specs=pl.BlockSpec((seq, C_DV), lambda b, h: (b, h)),
        out_shape=jax.ShapeDtypeStruct((t, C_V), _BF),
        scratch_shapes=[pltpu.VMEM((blk, blk), _F32)],
        compiler_params=pltpu.CompilerParams(
            dimension_semantics=("parallel", "parallel"), vmem_limit_bytes=_vmem_limit(est)),
        name="retention",
    )(cq, ckt, cv, cg, cnorm)


def _merge_kernel(x_ref, ya_ref, yb_ref, yc_ref, gate_ref, wa_ref, wb_ref, wc_ref, wo_ref, o_ref):
    merged = None
    for n, (y_ref, w_ref) in enumerate(((ya_ref, wa_ref), (yb_ref, wb_ref), (yc_ref, wc_ref))):
        br = jnp.dot(y_ref[...], w_ref[...], preferred_element_type=_F32)
        term = gate_ref[:, n * D_MODEL:(n + 1) * D_MODEL].astype(_F32) * br
        merged = term if merged is None else merged + term
    o_ref[...] = x_ref[...] + jnp.dot(merged.astype(_BF), wo_ref[...], preferred_element_type=_F32)


def _merge(x, ya, yb, yc, gates, wa, wb, wc, wo):
    t = x.shape[0]
    tm = TOKEN_TILE
    row = lambda i: (i, 0)
    const = lambda i: (0, 0)
    est = 5 * D_MODEL * D_MODEL * 2 + 2 * tm * (2 * D_MODEL * 4 + 3 * 512 * 2 + 3 * D_MODEL * 2) \
        + 6 * tm * D_MODEL * 4 + (8 << 20)
    return pl.pallas_call(
        _merge_kernel,
        grid=(t // tm,),
        in_specs=[
            pl.BlockSpec((tm, D_MODEL), row),
            pl.BlockSpec((tm, A_V), row),
            pl.BlockSpec((tm, B_QKV), row),
            pl.BlockSpec((tm, C_V), row),
            pl.BlockSpec((tm, N_BRANCH * D_MODEL), row),
            _resident((A_V, D_MODEL), const),
            _resident((B_QKV, D_MODEL), const),
            _resident((C_V, D_MODEL), const),
            _resident((D_MODEL, D_MODEL), const),
        ],
        out_specs=pl.BlockSpec((tm, D_MODEL), row),
        out_shape=jax.ShapeDtypeStruct((t, D_MODEL), _F32),
        compiler_params=pltpu.CompilerParams(
            dimension_semantics=("parallel",), vmem_limit_bytes=_vmem_limit(est)),
        name="merge",
    )(x, ya, yb, yc, gates, wa, wb, wc, wo)


def _rope_tables(seq):
    half = D_HEAD // 2
    inv = ROPE_THETA ** (-jnp.arange(half, dtype=_F32) / half)
    ang = jnp.arange(seq, dtype=_F32)[:, None] * inv[None, :]
    cos, sin = jnp.cos(ang), jnp.sin(ang)
    cos_t = jnp.concatenate([cos, cos] * (LANES // D_HEAD), axis=1)
    sin_t = jnp.concatenate([-sin, sin] * (LANES // D_HEAD), axis=1)
    return cos_t, sin_t


def _head_mean_matrix():
    g = np.kron(np.eye(MXU_DIM // D_HEAD), np.full((D_HEAD, D_HEAD), 1.0 / D_HEAD))
    return jnp.asarray(g, dtype=_BF)


def kernel(x, ffn1_norm, ffn1_w_gate, ffn1_w_up, ffn1_w_down, mix_norm, w_in, a_q_norm, a_k_norm, a_lambda_q1, a_lambda_k1, a_lambda_q2, a_lambda_k2, a_subln, b_q_norm, b_k_norm, b_rel_bias, c_out_norm, w_branch_a, w_branch_b, w_branch_c, w_out, ffn2_norm, ffn2_w_gate, ffn2_w_up, ffn2_w_down):
    batch, seq, d = x.shape
    assert d == D_MODEL and seq % ATTN_BLOCK == 0 and seq % RET_BLOCK == 0
    assert seq % TOKEN_TILE == 0 and seq >= B_WINDOW and ATTN_BLOCK == TOKEN_TILE == RET_BLOCK
    scale = D_HEAD ** -0.5
    cos_t, sin_t = _rope_tables(seq)
    gmat = _head_mean_matrix()
    bf = lambda w: w.astype(_BF)
    xt = x.reshape(batch * seq, D_MODEL)
    for l in range(DEPTH):
        lam_init = 0.8 - 0.6 * math.exp(-0.3 * l)
        xt = _ffn(xt, ffn1_norm[l][None, :], bf(ffn1_w_gate[l]), bf(ffn1_w_up[l]), bf(ffn1_w_down[l]))

        heads = A_QK // D_HEAD
        qk_gains = jnp.stack([
            jnp.tile(a_q_norm[l].astype(_F32) * (scale * LOG2E), heads),
            jnp.tile(a_k_norm[l].astype(_F32), heads),
            jnp.tile(b_q_norm[l].astype(_F32) * (scale * LOG2E), heads),
            jnp.tile(b_k_norm[l].astype(_F32), heads),
        ])
        aq, ak, avt, bq, bk, bv, cq, ckt, cv, cg, gates = _in_proj(
            xt, mix_norm[l][None, :], bf(w_in[l]), cos_t, sin_t, qk_gains, gmat, seq)

        lamv = jnp.stack([a_lambda_q1[l], a_lambda_k1[l], a_lambda_q2[l], a_lambda_k2[l]]).astype(_F32)
        a_bound = _score_bound(qk_gains[0, :D_HEAD], qk_gains[1, :D_HEAD])
        ya = _diff_attn(a_bound, aq, ak, avt, lamv, a_subln[l].astype(_F32)[:, None], batch, seq, lam_init)
        bias_gen = _bias_generator(b_rel_bias[l])
        b_bound = _score_bound(qk_gains[2, :D_HEAD], qk_gains[3, :D_HEAD]) + jnp.max(jnp.abs(bias_gen))
        yb = _chunk_attn(b_bound, bq, bk, bv, bias_gen, batch, seq)
        yc = _retention(cq, ckt, cv, cg, c_out_norm[l].astype(_F32)[:, None, :], batch, seq)

        xt = _merge(xt, ya, yb, yc, gates, bf(w_branch_a[l]), bf(w_branch_b[l]), bf(w_branch_c[l]),
                    bf(w_out[l]))
        xt = _ffn(xt, ffn2_norm[l][None, :], bf(ffn2_w_gate[l]), bf(ffn2_w_up[l]), bf(ffn2_w_down[l]))
    return xt.reshape(batch, seq, D_MODEL)
```

```python
import functools
import math

import numpy as np
import jax
import jax.numpy as jnp
from jax import lax
from jax.experimental import pallas as pl
from jax.experimental.pallas import tpu as pltpu

D_MODEL = 1024
DEPTH = 2
CHUNK = 64
D_HEAD = 64
ROPE_THETA = 10000.0
EPS = 1e-6
A_HEADS = 4
A_DV = 2 * D_HEAD
B_HEADS = 8
B_LOOKBACK = 8
B_MAX_REL = 256
C_HEADS = 4
C_DV = 2 * D_HEAD
N_BRANCH = 3
D_FF = 2816

A_QK = A_HEADS * 2 * D_HEAD
A_V = A_HEADS * A_DV
B_QKV = B_HEADS * D_HEAD
C_QK = C_HEADS * D_HEAD
C_V = C_HEADS * C_DV
SPLITS = (A_QK, A_QK, A_V, B_QKV, B_QKV, B_QKV, C_QK, C_QK, C_V, C_V, N_BRANCH * D_MODEL)
IN_COLS = sum(SPLITS)
_OFFS = tuple(int(v) for v in np.cumsum((0,) + SPLITS))

LANES = 128
MXU_DIM = 256
VMEM_BYTES_V7X = 64 * 1024 * 1024
VMEM_HEADROOM = 6 * 1024 * 1024

TOKEN_TILE = 512
FF_CHUNK = MXU_DIM
PROJ_CHUNK = 512
ATTN_BLOCK = 512
B_QTILE = 256
B_WINDOW = B_QTILE + B_LOOKBACK * CHUNK
RET_BLOCK = 512
NEG_BIG = -1e30

_BF = jnp.bfloat16
_F32 = jnp.float32
_NT = (((1,), (1,)), ((), ()))


def _vmem_limit(estimate_bytes):
    return int(min(VMEM_BYTES_V7X - VMEM_HEADROOM, max(32 * 1024 * 1024, estimate_bytes)))


def _resident(shape, index_map):
    return pl.BlockSpec(shape, index_map, pipeline_mode=pl.Buffered(1))


def _rms_rows(x, gain):
    ms = jnp.mean(x * x, axis=-1, keepdims=True)
    return x * lax.rsqrt(ms + EPS) * gain


def _ffn_kernel(x_ref, gain_ref, wg_ref, wu_ref, wd_ref, o_ref, acc_ref):
    x = x_ref[...]
    h = _rms_rows(x, gain_ref[...]).astype(_BF)
    for c in range(D_FF // FF_CHUNK):
        sl = slice(c * FF_CHUNK, (c + 1) * FF_CHUNK)
        g = jnp.dot(h, wg_ref[:, sl], preferred_element_type=_F32)
        u = jnp.dot(h, wu_ref[:, sl], preferred_element_type=_F32)
        a = (g * jax.nn.sigmoid(g) * u).astype(_BF)
        part = jnp.dot(a, wd_ref[sl, :], preferred_element_type=_F32)
        if c == 0:
            acc_ref[...] = part
        else:
            acc_ref[...] += part
    o_ref[...] = x + 0.5 * acc_ref[...]


def _ffn(x, gain, wg, wu, wd):
    t = x.shape[0]
    tm = TOKEN_TILE
    est = 3 * D_MODEL * D_FF * 2 + 5 * tm * D_MODEL * 4 + 4 * tm * FF_CHUNK * 4 + (8 << 20)
    return pl.pallas_call(
        _ffn_kernel,
        grid=(t // tm,),
        in_specs=[
            pl.BlockSpec((tm, D_MODEL), lambda i: (i, 0)),
            _resident((1, D_MODEL), lambda i: (0, 0)),
            _resident((D_MODEL, D_FF), lambda i: (0, 0)),
            _resident((D_MODEL, D_FF), lambda i: (0, 0)),
            _resident((D_FF, D_MODEL), lambda i: (0, 0)),
        ],
        out_specs=pl.BlockSpec((tm, D_MODEL), lambda i: (i, 0)),
        out_shape=jax.ShapeDtypeStruct((t, D_MODEL), _F32),
        scratch_shapes=[pltpu.VMEM((tm, D_MODEL), _F32)],
        compiler_params=pltpu.CompilerParams(
            dimension_semantics=("parallel",), vmem_limit_bytes=_vmem_limit(est)),
        name="ffn",
    )(x, gain, wg, wu, wd)


def _swap_halves(x):
    lane = lax.broadcasted_iota(jnp.int32, (1, LANES), 1)
    upper = (lane & (D_HEAD // 2)) != 0
    outs = []
    for s in range(x.shape[1] // LANES):
        xs = x[:, s * LANES:(s + 1) * LANES]
        from_below = pltpu.roll(xs, D_HEAD // 2, 1)
        from_above = pltpu.roll(xs, LANES - D_HEAD // 2, 1)
        outs.append(jnp.where(upper, from_below, from_above))
    return outs[0] if len(outs) == 1 else jnp.concatenate(outs, axis=1)


def _tile_lanes(t, n):
    reps = n // t.shape[1]
    return t if reps == 1 else jnp.concatenate([t] * reps, axis=1)


def _rope(x, cos, sin):
    n = x.shape[1]
    return x * _tile_lanes(cos, n) + _swap_halves(x) * _tile_lanes(sin, n)


def _head_ms(p, gmat):
    sq = (p * p).astype(_BF)
    outs = [jnp.dot(sq[:, s:s + MXU_DIM], gmat, preferred_element_type=_F32)
            for s in range(0, p.shape[1], MXU_DIM)]
    return outs[0] if len(outs) == 1 else jnp.concatenate(outs, axis=1)


def _in_proj_kernel(x_ref, gain_ref, w_ref, cos_ref, sin_ref, qkg_ref, gmat_ref,
                    aq_ref, ak_ref, avt_ref, bq_ref, bk_ref, bv_ref,
                    cq_ref, ckt_ref, cv_ref, cg_ref, gate_ref):
    h = _rms_rows(x_ref[...], gain_ref[...]).astype(_BF)
    cos = cos_ref[...]
    sin = sin_ref[...]
    gmat = gmat_ref[...]

    def proj(lo, width):
        return jnp.dot(h, w_ref[:, lo:lo + width], preferred_element_type=_F32)

    def normed(p, row):
        return p * lax.rsqrt(_head_ms(p, gmat) + EPS) * qkg_ref[row:row + 1, :]

    aq_ref[...] = _rope(normed(proj(_OFFS[0], A_QK), 0), cos, sin).astype(_BF)
    ak_ref[...] = _rope(normed(proj(_OFFS[1], A_QK), 1), cos, sin).astype(_BF)
    avt_ref[0] = proj(_OFFS[2], A_V).T.astype(_BF)
    bq_ref[...] = normed(proj(_OFFS[3], B_QKV), 2).astype(_BF)
    bk_ref[...] = normed(proj(_OFFS[4], B_QKV), 3).astype(_BF)
    bv_ref[...] = proj(_OFFS[5], B_QKV).astype(_BF)
    cq_ref[...] = _rope(proj(_OFFS[6], C_QK), cos, sin).astype(_BF)
    ckt_ref[0] = (_rope(proj(_OFFS[7], C_QK), cos, sin) * (D_HEAD ** -0.5)).T.astype(_BF)
    cv_ref[...] = proj(_OFFS[8], C_V).astype(_BF)
    cg_ref[...] = proj(_OFFS[9], C_V).astype(_BF)
    for c in range(N_BRANCH * D_MODEL // PROJ_CHUNK):
        lo = c * PROJ_CHUNK
        gate_ref[:, lo:lo + PROJ_CHUNK] = jax.nn.sigmoid(
            proj(_OFFS[10] + lo, PROJ_CHUNK)).astype(_BF)


def _in_proj(x, gain, w, cos_t, sin_t, qk_gains, gmat, seq):
    t = x.shape[0]
    tm = TOKEN_TILE
    nt = t // tm
    pos_blocks = seq // tm
    row = lambda i: (i, 0)
    tok_major = lambda n: pl.BlockSpec((tm, n), row)
    out_shapes = (
        jax.ShapeDtypeStruct((t, A_QK), _BF),
        jax.ShapeDtypeStruct((t, A_QK), _BF),
        jax.ShapeDtypeStruct((nt, A_V, tm), _BF),
        jax.ShapeDtypeStruct((t, B_QKV), _BF),
        jax.ShapeDtypeStruct((t, B_QKV), _BF),
        jax.ShapeDtypeStruct((t, B_QKV), _BF),
        jax.ShapeDtypeStruct((t, C_QK), _BF),
        jax.ShapeDtypeStruct((nt, C_QK, tm), _BF),
        jax.ShapeDtypeStruct((t, C_V), _BF),
        jax.ShapeDtypeStruct((t, C_V), _BF),
        jax.ShapeDtypeStruct((t, N_BRANCH * D_MODEL), _BF),
    )
    out_specs = (
        tok_major(A_QK), tok_major(A_QK),
        pl.BlockSpec((1, A_V, tm), lambda i: (i, 0, 0)),
        tok_major(B_QKV), tok_major(B_QKV), tok_major(B_QKV),
        tok_major(C_QK),
        pl.BlockSpec((1, C_QK, tm), lambda i: (i, 0, 0)),
        tok_major(C_V), tok_major(C_V), tok_major(N_BRANCH * D_MODEL),
    )
    est = D_MODEL * IN_COLS * 2 + 2 * tm * IN_COLS * 2 + 4 * tm * D_MODEL * 4 + (12 << 20)
    return pl.pallas_call(
        _in_proj_kernel,
        grid=(nt,),
        in_specs=[
            pl.BlockSpec((tm, D_MODEL), row),
            _resident((1, D_MODEL), lambda i: (0, 0)),
            _resident((D_MODEL, IN_COLS), lambda i: (0, 0)),
            pl.BlockSpec((tm, LANES), lambda i: (i % pos_blocks, 0)),
            pl.BlockSpec((tm, LANES), lambda i: (i % pos_blocks, 0)),
            _resident((4, A_QK), lambda i: (0, 0)),
            _resident((MXU_DIM, MXU_DIM), lambda i: (0, 0)),
        ],
        out_specs=out_specs,
        out_shape=out_shapes,
        compiler_params=pltpu.CompilerParams(
            dimension_semantics=("parallel",), vmem_limit_bytes=_vmem_limit(est)),
        name="in_proj",
    )(x, gain, w, cos_t, sin_t, qk_gains, gmat)


LOG2E = math.log2(math.e)
SAFE_LOG2_SCORE = 60.0
NORM_MARGIN = 1.05
A_PAIR = 4


def _diff_attn_kernel(bound_ref, q_ref, k_ref, vt_ref, lamv_ref, subln_ref, o_ref, acc_ref, *,
                      lam_init):
    blk = ATTN_BLOCK
    i = pl.program_id(2)
    n_chain = 2 * A_PAIR
    per_group = MXU_DIM // D_HEAD
    lane = lax.broadcasted_iota(jnp.int32, (1, MXU_DIM), 1)
    qs = []
    for n in range(n_chain):
        g, r = divmod(n, per_group)
        qg = q_ref[:, g * MXU_DIM:(g + 1) * MXU_DIM]
        qs.append(jnp.where((lane >= D_HEAD * r) & (lane < D_HEAD * (r + 1)), qg,
                            jnp.zeros_like(qg)))

    def scores(j, n, diagonal):
        g = n // per_group
        kb = k_ref[pl.ds(pl.multiple_of(j * blk, blk), blk), g * MXU_DIM:(g + 1) * MXU_DIM]
        st = lax.dot_general(kb, qs[n], _NT, preferred_element_type=_F32)
        if diagonal:
            key_chunk = lax.broadcasted_iota(jnp.int32, (blk, blk), 0) // CHUNK
            qry_chunk = lax.broadcasted_iota(jnp.int32, (blk, blk), 1) // CHUNK
            st = jnp.where(key_chunk <= qry_chunk, st, NEG_BIG)
        return st

    def values_t(j, n):
        head = n // 2
        return vt_ref[j, head * A_DV:(head + 1) * A_DV, :]

    def plain_step(j, carry, diagonal):
        out = []
        st = scores(j, 0, diagonal)
        for n in range(n_chain):
            st_next = scores(j, n + 1, diagonal) if n + 1 < n_chain else None
            p = jnp.exp2(st)
            acc_ref[n] += jnp.dot(values_t(j, n), p.astype(_BF), preferred_element_type=_F32)
            out.append(carry[n] + jnp.sum(p, axis=0, keepdims=True))
            st = st_next
        return tuple(out)

    def shifted_step(j, carry, diagonal):
        out = []
        for n in range(n_chain):
            m_old, l_old = carry[n]
            st = scores(j, n, diagonal)
            m_new = jnp.maximum(m_old, jnp.max(st, axis=0, keepdims=True))
            alpha = jnp.exp2(m_old - m_new)
            p = jnp.exp2(st - m_new)
            l_new = alpha * l_old + jnp.sum(p, axis=0, keepdims=True)
            pv = jnp.dot(values_t(j, n), p.astype(_BF), preferred_element_type=_F32)
            acc_ref[n] = alpha * acc_ref[n] + pv
            out.append((m_new, l_new))
        return tuple(out)

    def finish(ls):
        lamv = lamv_ref[...]
        lam = (jnp.exp(jnp.sum(lamv[0:1] * lamv[1:2], axis=-1, keepdims=True))
               - jnp.exp(jnp.sum(lamv[2:3] * lamv[3:4], axis=-1, keepdims=True)) + lam_init)
        for head in range(A_PAIR):
            o = (acc_ref[2 * head] / ls[2 * head]
                 - lam * (acc_ref[2 * head + 1] / ls[2 * head + 1]))
            ms = jnp.mean(o * o, axis=0, keepdims=True)
            y = o * lax.rsqrt(ms + EPS) * subln_ref[...] * (1.0 - lam_init)
            o_ref[:, head * A_DV:(head + 1) * A_DV] = y.T.astype(_BF)

    acc_ref[...] = jnp.zeros_like(acc_ref)
    bounded = bound_ref[0] <= SAFE_LOG2_SCORE

    @pl.when(bounded)
    def _():
        init = tuple(jnp.zeros((1, blk), _F32) for _ in range(n_chain))
        carry = lax.fori_loop(0, i, lambda j, c: plain_step(j, c, False), init)
        finish(plain_step(i, carry, True))

    @pl.when(jnp.logical_not(bounded))
    def _():
        init = tuple((jnp.full((1, blk), NEG_BIG, _F32), jnp.zeros((1, blk), _F32))
                     for _ in range(n_chain))
        carry = lax.fori_loop(0, i, lambda j, c: shifted_step(j, c, False), init)
        finish([l for _, l in shifted_step(i, carry, True)])


def _diff_attn(score_bound, aq, ak, avt, lamv, subln_col, batch, seq, lam_init):
    blk = ATTN_BLOCK
    nq = seq // blk
    t = batch * seq
    width = A_PAIR * A_DV
    est = 2 * (seq * width * 2) * 2 + 2 * A_PAIR * (A_DV * blk * 4) + 16 * blk * blk * 4 + (8 << 20)
    return pl.pallas_call(
        functools.partial(_diff_attn_kernel, lam_init=lam_init),
        grid=(batch, A_HEADS // A_PAIR, nq),
        in_specs=[
            pl.BlockSpec(memory_space=pltpu.SMEM),
            pl.BlockSpec((blk, width), lambda b, h, i: (b * nq + i, h)),
            pl.BlockSpec((seq, width), lambda b, h, i: (b, h)),
            pl.BlockSpec((nq, width, blk), lambda b, h, i: (b, h, 0)),
            pl.BlockSpec((4, D_HEAD), lambda b, h, i: (0, 0)),
            pl.BlockSpec((A_DV, 1), lambda b, h, i: (0, 0)),
        ],
        out_specs=pl.BlockSpec((blk, width), lambda b, h, i: (b * nq + i, h)),
        out_shape=jax.ShapeDtypeStruct((t, A_V), _BF),
        scratch_shapes=[pltpu.VMEM((2 * A_PAIR, A_DV, blk), _F32)],
        compiler_params=pltpu.CompilerParams(
            dimension_semantics=("parallel", "parallel", "parallel"),
            vmem_limit_bytes=_vmem_limit(est)),
        name="diff_attn",
    )(score_bound, aq, ak, avt, lamv, subln_col)


def _score_bound(q_gain, k_gain):
    return (D_HEAD * NORM_MARGIN * jnp.max(jnp.abs(q_gain)) * jnp.max(jnp.abs(k_gain))).reshape(1)


B_TABLES = (B_WINDOW - B_QTILE) // B_QTILE + 1
B_GEN_LEN = 2048


def _build_bias_tables(gen_ref, mb_ref):
    qchunk = lax.broadcasted_iota(jnp.int32, (B_QTILE, B_WINDOW), 0) // CHUNK
    kchunk = lax.broadcasted_iota(jnp.int32, (B_QTILE, B_WINDOW), 1) // CHUNK
    for hh in range(2):
        rolled = pltpu.roll(jnp.broadcast_to(gen_ref[hh], (B_QTILE, B_GEN_LEN)), 0, 1,
                            stride=1, stride_axis=0)
        for t in range(B_TABLES):
            lo = B_WINDOW - t * B_QTILE
            dchunk = qchunk - kchunk + (t * B_QTILE) // CHUNK
            visible = (dchunk >= 0) & (dchunk <= B_LOOKBACK)
            mb_ref[hh, t] = jnp.where(visible, rolled[:, lo:lo + B_WINDOW], NEG_BIG)


def _chunk_attn_kernel(bound_ref, q_ref, k_ref, v_ref, gen_ref, o_ref, mb_ref):
    i = pl.program_id(2)

    @pl.when(i == 0)
    def _():
        _build_bias_tables(gen_ref, mb_ref)

    start = pl.multiple_of(jnp.maximum(i * B_QTILE - (B_WINDOW - B_QTILE), 0), B_QTILE)
    table = jnp.minimum(i, B_TABLES - 1)
    lane = lax.broadcasted_iota(jnp.int32, (1, LANES), 1)
    q = q_ref[...]
    zero = jnp.zeros_like(q)
    kw = k_ref[pl.ds(start, B_WINDOW), :]
    vw = v_ref[pl.ds(start, B_WINDOW), :]

    def attend(shifted):
        outs = []
        for hh in range(2):
            in_head = (lane >= D_HEAD) if hh else (lane < D_HEAD)
            s = lax.dot_general(jnp.where(in_head, q, zero), kw, _NT, preferred_element_type=_F32)
            s = s + mb_ref[hh, table]
            if shifted:
                s = s - jnp.max(s, axis=-1, keepdims=True)
            p = jnp.exp2(s)
            l = jnp.sum(p, axis=-1, keepdims=True)
            outs.append(jnp.dot(p.astype(_BF), vw, preferred_element_type=_F32) / l)
        o_ref[...] = jnp.where(lane < D_HEAD, outs[0], outs[1]).astype(_BF)

    bounded = bound_ref[0] <= SAFE_LOG2_SCORE
    pl.when(bounded)(lambda: attend(False))
    pl.when(jnp.logical_not(bounded))(lambda: attend(True))


def _chunk_attn(score_bound, bq, bk, bv, bias_gen, batch, seq):
    nq = seq // B_QTILE
    t = batch * seq
    est = 2 * B_TABLES * B_QTILE * B_WINDOW * 4 + 4 * B_QTILE * B_GEN_LEN * 4 \
        + 4 * seq * LANES * 2 + (10 << 20)
    return pl.pallas_call(
        _chunk_attn_kernel,
        grid=(batch, B_HEADS // 2, nq),
        in_specs=[
            pl.BlockSpec(memory_space=pltpu.SMEM),
            pl.BlockSpec((B_QTILE, LANES), lambda b, h, i: (b * nq + i, h)),
            pl.BlockSpec((seq, LANES), lambda b, h, i: (b, h)),
            pl.BlockSpec((seq, LANES), lambda b, h, i: (b, h)),
            pl.BlockSpec((2, 1, B_GEN_LEN), lambda b, h, i: (h, 0, 0)),
        ],
        out_specs=pl.BlockSpec((B_QTILE, LANES), lambda b, h, i: (b * nq + i, h)),
        out_shape=jax.ShapeDtypeStruct((t, B_QKV), _BF),
        scratch_shapes=[pltpu.VMEM((2, B_TABLES, B_QTILE, B_WINDOW), _F32)],
        compiler_params=pltpu.CompilerParams(
            dimension_semantics=("parallel", "parallel", "arbitrary"),
            vmem_limit_bytes=_vmem_limit(est)),
        name="chunk_attn",
    )(score_bound, bq, bk, bv, bias_gen)


def _bias_generator(rel_bias):
    b = rel_bias.astype(_F32) * LOG2E
    n_far = B_WINDOW - B_MAX_REL + 1
    mid = b[:, 1:CHUNK - 1 + B_MAX_REL][:, ::-1]
    n_neg = B_GEN_LEN - n_far - mid.shape[1]
    heads = b.shape[0]
    gen = jnp.concatenate([
        jnp.broadcast_to(b[:, -1:], (heads, n_far)), mid,
        jnp.broadcast_to(b[:, :1], (heads, n_neg))], axis=1)
    return gen[:, None, :]


def _retention_kernel(q_ref, kt_ref, v_ref, g_ref, cn_ref, o_ref, decay_ref, *, seq):
    blk = RET_BLOCK
    h = pl.program_id(1)
    hf = jnp.full((1, 1), h, jnp.int32).astype(_F32)
    log_gamma = jnp.log(1.0 - jnp.exp2(-5.0 - hf))
    diff = (lax.broadcasted_iota(jnp.int32, (blk, blk), 0)
            - lax.broadcasted_iota(jnp.int32, (blk, blk), 1)).astype(_F32)
    decay_ref[...] = jnp.where(diff >= 0, jnp.exp(log_gamma * jnp.maximum(diff, 0.0)), 0.0)
    pos = lax.broadcasted_iota(jnp.int32, (blk, LANES), 0).astype(_F32)
    xi = jnp.exp(log_gamma * (pos + 1.0))
    zeta = jnp.exp(log_gamma * (blk - 1.0 - pos))
    block_decay = jnp.exp(log_gamma * float(blk))
    odd = (h % 2) == 1
    q_keep = (lax.broadcasted_iota(jnp.int32, (1, LANES), 1) >= D_HEAD) == odd
    k_keep = (lax.broadcasted_iota(jnp.int32, (LANES, 1), 0) >= D_HEAD) == odd
    gain = cn_ref[0]

    def body(j, state):
        rows = pl.ds(pl.multiple_of(j * blk, blk), blk)
        qb = q_ref[rows, :]
        qb = jnp.where(q_keep, qb, jnp.zeros_like(qb))
        kt = kt_ref[j]
        kt = jnp.where(k_keep, kt, jnp.zeros_like(kt))
        vb = v_ref[rows, :]
        sc = jnp.dot(qb, kt, preferred_element_type=_F32) * decay_ref[...]
        inner = jnp.dot(sc.astype(_BF), vb, preferred_element_type=_F32)
        cross = jnp.dot(qb, state.astype(_BF), preferred_element_type=_F32) * xi
        o = inner + cross
        vz = (vb.astype(_F32) * zeta).astype(_BF)
        new_state = state * block_decay + jnp.dot(kt, vz, preferred_element_type=_F32)
        ms = jnp.mean(o * o, axis=-1, keepdims=True)
        y = o * lax.rsqrt(ms + EPS) * gain
        gg = g_ref[rows, :].astype(_F32)
        o_ref[rows, :] = (gg * jax.nn.sigmoid(gg) * y).astype(_BF)
        return new_state

    lax.fori_loop(0, seq // blk, body, jnp.zeros((LANES, C_DV), _F32))


def _retention(cq, ckt, cv, cg, cnorm, batch, seq):
    blk = RET_BLOCK
    nb = seq // blk
    t = batch * seq
    est = 2 * 5 * seq * LANES * 2 + 6 * blk * blk * 4 + (8 << 20)
    return pl.pallas_call(
        functools.partial(_retention_kernel, seq=seq),
        grid=(batch, C_HEADS),
        in_specs=[
            pl.BlockSpec((seq, LANES), lambda b, h: (b, h // 2)),
            pl.BlockSpec((nb, LANES, blk), lambda b, h: (b, h // 2, 0)),
            pl.BlockSpec((seq, C_DV), lambda b, h: (b, h)),
            pl.BlockSpec((seq, C_DV), lambda b, h: (b, h)),
            pl.BlockSpec((1, 1, C_DV), lambda b, h: (h, 0, 0)),
        ],
        out_specs=pl.BlockSpec((seq, C_DV), lambda b, h: (b, h)),
        out_shape=jax.ShapeDtypeStruct((t, C_V), _BF),
        scratch_shapes=[pltpu.VMEM((blk, blk), _F32)],
        compiler_params=pltpu.CompilerParams(
            dimension_semantics=("parallel", "parallel"), vmem_limit_bytes=_vmem_limit(est)),
        name="retention",
    )(cq, ckt, cv, cg, cnorm)


def _merge_ffn_kernel(x_ref, ya_ref, yb_ref, yc_ref, gate_ref, wa_ref, wb_ref, wc_ref, wo_ref,
                      gain_ref, wg_ref, wu_ref, wd_ref, o_ref, acc_ref):
    merged = None
    for n, (y_ref, w_ref) in enumerate(((ya_ref, wa_ref), (yb_ref, wb_ref), (yc_ref, wc_ref))):
        br = jnp.dot(y_ref[...], w_ref[...], preferred_element_type=_F32)
        term = gate_ref[:, n * D_MODEL:(n + 1) * D_MODEL].astype(_F32) * br
        merged = term if merged is None else merged + term
    x = x_ref[...] + jnp.dot(merged.astype(_BF), wo_ref[...], preferred_element_type=_F32)
    h = _rms_rows(x, gain_ref[...]).astype(_BF)
    for c in range(D_FF // FF_CHUNK):
        sl = slice(c * FF_CHUNK, (c + 1) * FF_CHUNK)
        g = jnp.dot(h, wg_ref[:, sl], preferred_element_type=_F32)
        u = jnp.dot(h, wu_ref[:, sl], preferred_element_type=_F32)
        a = (g * jax.nn.sigmoid(g) * u).astype(_BF)
        part = jnp.dot(a, wd_ref[sl, :], preferred_element_type=_F32)
        if c == 0:
            acc_ref[...] = part
        else:
            acc_ref[...] += part
    o_ref[...] = x + 0.5 * acc_ref[...]


def _merge_ffn(x, ya, yb, yc, gates, wa, wb, wc, wo, gain, wg, wu, wd):
    t = x.shape[0]
    tm = TOKEN_TILE
    row = lambda i: (i, 0)
    const = lambda i: (0, 0)
    est = (5 * D_MODEL * D_MODEL + 3 * D_MODEL * D_FF) * 2 \
        + 2 * tm * (2 * D_MODEL * 4 + 3 * 512 * 2 + 3 * D_MODEL * 2) \
        + 8 * tm * D_MODEL * 4 + 4 * tm * FF_CHUNK * 4 + (8 << 20)
    return pl.pallas_call(
        _merge_ffn_kernel,
        grid=(t // tm,),
        in_specs=[
            pl.BlockSpec((tm, D_MODEL), row),
            pl.BlockSpec((tm, A_V), row),
            pl.BlockSpec((tm, B_QKV), row),
            pl.BlockSpec((tm, C_V), row),
            pl.BlockSpec((tm, N_BRANCH * D_MODEL), row),
            _resident((A_V, D_MODEL), const),
            _resident((B_QKV, D_MODEL), const),
            _resident((C_V, D_MODEL), const),
            _resident((D_MODEL, D_MODEL), const),
            _resident((1, D_MODEL), const),
            _resident((D_MODEL, D_FF), const),
            _resident((D_MODEL, D_FF), const),
            _resident((D_FF, D_MODEL), const),
        ],
        out_specs=pl.BlockSpec((tm, D_MODEL), row),
        out_shape=jax.ShapeDtypeStruct((t, D_MODEL), _F32),
        scratch_shapes=[pltpu.VMEM((tm, D_MODEL), _F32)],
        compiler_params=pltpu.CompilerParams(
            dimension_semantics=("parallel",), vmem_limit_bytes=_vmem_limit(est)),
        name="merge_ffn",
    )(x, ya, yb, yc, gates, wa, wb, wc, wo, gain, wg, wu, wd)


def _rope_tables(seq):
    half = D_HEAD // 2
    inv = ROPE_THETA ** (-jnp.arange(half, dtype=_F32) / half)
    ang = jnp.arange(seq, dtype=_F32)[:, None] * inv[None, :]
    cos, sin = jnp.cos(ang), jnp.sin(ang)
    cos_t = jnp.concatenate([cos, cos] * (LANES // D_HEAD), axis=1)
    sin_t = jnp.concatenate([-sin, sin] * (LANES // D_HEAD), axis=1)
    return cos_t, sin_t


def _head_mean_matrix():
    g = np.kron(np.eye(MXU_DIM // D_HEAD), np.full((D_HEAD, D_HEAD), 1.0 / D_HEAD))
    return jnp.asarray(g, dtype=_BF)


def kernel(x, ffn1_norm, ffn1_w_gate, ffn1_w_up, ffn1_w_down, mix_norm, w_in, a_q_norm, a_k_norm, a_lambda_q1, a_lambda_k1, a_lambda_q2, a_lambda_k2, a_subln, b_q_norm, b_k_norm, b_rel_bias, c_out_norm, w_branch_a, w_branch_b, w_branch_c, w_out, ffn2_norm, ffn2_w_gate, ffn2_w_up, ffn2_w_down):
    batch, seq, d = x.shape
    assert d == D_MODEL and seq % ATTN_BLOCK == 0 and seq % RET_BLOCK == 0
    assert seq % TOKEN_TILE == 0 and seq >= B_WINDOW and ATTN_BLOCK == TOKEN_TILE == RET_BLOCK
    scale = D_HEAD ** -0.5
    cos_t, sin_t = _rope_tables(seq)
    gmat = _head_mean_matrix()
    bf = lambda w: w.astype(_BF)
    xt = x.reshape(batch * seq, D_MODEL)
    for l in range(DEPTH):
        lam_init = 0.8 - 0.6 * math.exp(-0.3 * l)
        xt = _ffn(xt, ffn1_norm[l][None, :], bf(ffn1_w_gate[l]), bf(ffn1_w_up[l]), bf(ffn1_w_down[l]))

        heads = A_QK // D_HEAD
        qk_gains = jnp.stack([
            jnp.tile(a_q_norm[l].astype(_F32) * (scale * LOG2E), heads),
            jnp.tile(a_k_norm[l].astype(_F32), heads),
            jnp.tile(b_q_norm[l].astype(_F32) * (scale * LOG2E), heads),
            jnp.tile(b_k_norm[l].astype(_F32), heads),
        ])
        aq, ak, avt, bq, bk, bv, cq, ckt, cv, cg, gates = _in_proj(
            xt, mix_norm[l][None, :], bf(w_in[l]), cos_t, sin_t, qk_gains, gmat, seq)

        lamv = jnp.stack([a_lambda_q1[l], a_lambda_k1[l], a_lambda_q2[l], a_lambda_k2[l]]).astype(_F32)
        a_bound = _score_bound(qk_gains[0, :D_HEAD], qk_gains[1, :D_HEAD])
        ya = _diff_attn(a_bound, aq, ak, avt, lamv, a_subln[l].astype(_F32)[:, None], batch, seq, lam_init)
        bias_gen = _bias_generator(b_rel_bias[l])
        b_bound = _score_bound(qk_gains[2, :D_HEAD], qk_gains[3, :D_HEAD]) + jnp.max(jnp.abs(bias_gen))
        yb = _chunk_attn(b_bound, bq, bk, bv, bias_gen, batch, seq)
        yc = _retention(cq, ckt, cv, cg, c_out_norm[l].astype(_F32)[:, None, :], batch, seq)

        xt = _merge_ffn(xt, ya, yb, yc, gates, bf(w_branch_a[l]), bf(w_branch_b[l]),
                        bf(w_branch_c[l]), bf(w_out[l]), ffn2_norm[l][None, :],
                        bf(ffn2_w_gate[l]), bf(ffn2_w_up[l]), bf(ffn2_w_down[l]))
    return xt.reshape(batch, seq, D_MODEL)
```

```python
import functools
import math

import numpy as np
import jax
import jax.numpy as jnp
from jax import lax
from jax.experimental import pallas as pl
from jax.experimental.pallas import tpu as pltpu

D_MODEL = 1024
DEPTH = 2
CHUNK = 64
D_HEAD = 64
ROPE_THETA = 10000.0
EPS = 1e-6
A_HEADS = 4
A_DV = 2 * D_HEAD
B_HEADS = 8
B_LOOKBACK = 8
B_MAX_REL = 256
C_HEADS = 4
C_DV = 2 * D_HEAD
N_BRANCH = 3
D_FF = 2816

A_QK = A_HEADS * 2 * D_HEAD
A_V = A_HEADS * A_DV
B_QKV = B_HEADS * D_HEAD
C_QK = C_HEADS * D_HEAD
C_V = C_HEADS * C_DV
SPLITS = (A_QK, A_QK, A_V, B_QKV, B_QKV, B_QKV, C_QK, C_QK, C_V, C_V, N_BRANCH * D_MODEL)
IN_COLS = sum(SPLITS)
_OFFS = tuple(int(v) for v in np.cumsum((0,) + SPLITS))

LANES = 128
MXU_DIM = 256
VMEM_BYTES_V7X = 64 * 1024 * 1024
VMEM_HEADROOM = 6 * 1024 * 1024

TOKEN_TILE = 512
FF_CHUNK = MXU_DIM
PROJ_CHUNK = 512
ATTN_BLOCK = 512
B_QTILE = 256
B_WINDOW = B_QTILE + B_LOOKBACK * CHUNK
RET_BLOCK = 512
RET_UNROLL = 4
NEG_BIG = -1e30

_BF = jnp.bfloat16
_F32 = jnp.float32
_NT = (((1,), (1,)), ((), ()))


def _vmem_limit(estimate_bytes):
    return int(min(VMEM_BYTES_V7X - VMEM_HEADROOM, max(32 * 1024 * 1024, estimate_bytes)))


def _resident(shape, index_map):
    return pl.BlockSpec(shape, index_map, pipeline_mode=pl.Buffered(1))


def _layer_weight(layer, rows, cols):
    return pl.BlockSpec((None, rows, cols), lambda i: (layer, 0, 0), pipeline_mode=pl.Buffered(1))


def _rms_rows(x, gain):
    ms = jnp.mean(x * x, axis=-1, keepdims=True)
    return x * lax.rsqrt(ms + EPS) * gain


def _ffn_kernel(x_ref, gain_ref, wg_ref, wu_ref, wd_ref, o_ref, acc_ref):
    x = x_ref[...]
    h = _rms_rows(x, gain_ref[...]).astype(_BF)
    for c in range(D_FF // FF_CHUNK):
        sl = slice(c * FF_CHUNK, (c + 1) * FF_CHUNK)
        g = jnp.dot(h, wg_ref[:, sl], preferred_element_type=_F32)
        u = jnp.dot(h, wu_ref[:, sl], preferred_element_type=_F32)
        a = (g * jax.nn.sigmoid(g) * u).astype(_BF)
        part = jnp.dot(a, wd_ref[sl, :], preferred_element_type=_F32)
        if c == 0:
            acc_ref[...] = part
        else:
            acc_ref[...] += part
    o_ref[...] = x + 0.5 * acc_ref[...]


def _ffn(layer, x, gain, wg, wu, wd):
    t = x.shape[0]
    tm = TOKEN_TILE
    est = 3 * D_MODEL * D_FF * 2 + 5 * tm * D_MODEL * 4 + 4 * tm * FF_CHUNK * 4 + (8 << 20)
    return pl.pallas_call(
        _ffn_kernel,
        grid=(t // tm,),
        in_specs=[
            pl.BlockSpec((tm, D_MODEL), lambda i: (i, 0)),
            _resident((1, D_MODEL), lambda i: (0, 0)),
            _layer_weight(layer, D_MODEL, D_FF),
            _layer_weight(layer, D_MODEL, D_FF),
            _layer_weight(layer, D_FF, D_MODEL),
        ],
        out_specs=pl.BlockSpec((tm, D_MODEL), lambda i: (i, 0)),
        out_shape=jax.ShapeDtypeStruct((t, D_MODEL), _F32),
        scratch_shapes=[pltpu.VMEM((tm, D_MODEL), _F32)],
        compiler_params=pltpu.CompilerParams(
            dimension_semantics=("parallel",), vmem_limit_bytes=_vmem_limit(est)),
        name="ffn",
    )(x, gain, wg, wu, wd)


def _swap_halves(x):
    lane = lax.broadcasted_iota(jnp.int32, (1, LANES), 1)
    upper = (lane & (D_HEAD // 2)) != 0
    outs = []
    for s in range(x.shape[1] // LANES):
        xs = x[:, s * LANES:(s + 1) * LANES]
        from_below = pltpu.roll(xs, D_HEAD // 2, 1)
        from_above = pltpu.roll(xs, LANES - D_HEAD // 2, 1)
        outs.append(jnp.where(upper, from_below, from_above))
    return outs[0] if len(outs) == 1 else jnp.concatenate(outs, axis=1)


def _tile_lanes(t, n):
    reps = n // t.shape[1]
    return t if reps == 1 else jnp.concatenate([t] * reps, axis=1)


def _rope(x, cos, sin):
    n = x.shape[1]
    return x * _tile_lanes(cos, n) + _swap_halves(x) * _tile_lanes(sin, n)


def _head_ms(p, gmat):
    sq = (p * p).astype(_BF)
    outs = [jnp.dot(sq[:, s:s + MXU_DIM], gmat, preferred_element_type=_F32)
            for s in range(0, p.shape[1], MXU_DIM)]
    return outs[0] if len(outs) == 1 else jnp.concatenate(outs, axis=1)


def _in_proj_kernel(x_ref, gain_ref, w_ref, cos_ref, sin_ref, qkg_ref, gmat_ref,
                    aq_ref, ak_ref, avt_ref, bq_ref, bk_ref, bv_ref,
                    cq_ref, ckt_ref, cv_ref, cg_ref, gate_ref):
    h = _rms_rows(x_ref[...], gain_ref[...]).astype(_BF)
    cos = cos_ref[...]
    sin = sin_ref[...]
    gmat = gmat_ref[...]

    def proj(lo, width):
        return jnp.dot(h, w_ref[:, lo:lo + width], preferred_element_type=_F32)

    def normed(p, row):
        return p * lax.rsqrt(_head_ms(p, gmat) + EPS) * qkg_ref[row:row + 1, :]

    aq_ref[...] = _rope(normed(proj(_OFFS[0], A_QK), 0), cos, sin).astype(_BF)
    ak_ref[...] = _rope(normed(proj(_OFFS[1], A_QK), 1), cos, sin).astype(_BF)
    avt_ref[0] = proj(_OFFS[2], A_V).T.astype(_BF)
    bq_ref[...] = normed(proj(_OFFS[3], B_QKV), 2).astype(_BF)
    bk_ref[...] = normed(proj(_OFFS[4], B_QKV), 3).astype(_BF)
    bv_ref[...] = proj(_OFFS[5], B_QKV).astype(_BF)
    cq_ref[...] = _rope(proj(_OFFS[6], C_QK), cos, sin).astype(_BF)
    ckt_ref[0] = (_rope(proj(_OFFS[7], C_QK), cos, sin) * (D_HEAD ** -0.5)).T.astype(_BF)
    cv_ref[...] = proj(_OFFS[8], C_V).astype(_BF)
    cg_ref[...] = proj(_OFFS[9], C_V).astype(_BF)
    for c in range(N_BRANCH * D_MODEL // PROJ_CHUNK):
        lo = c * PROJ_CHUNK
        gate_ref[:, lo:lo + PROJ_CHUNK] = jax.nn.sigmoid(
            proj(_OFFS[10] + lo, PROJ_CHUNK)).astype(_BF)


def _in_proj(layer, x, gain, w, cos_t, sin_t, qk_gains, gmat, seq):
    t = x.shape[0]
    tm = TOKEN_TILE
    nt = t // tm
    pos_blocks = seq // tm
    row = lambda i: (i, 0)
    tok_major = lambda n: pl.BlockSpec((tm, n), row)
    out_shapes = (
        jax.ShapeDtypeStruct((t, A_QK), _BF),
        jax.ShapeDtypeStruct((t, A_QK), _BF),
        jax.ShapeDtypeStruct((nt, A_V, tm), _BF),
        jax.ShapeDtypeStruct((t, B_QKV), _BF),
        jax.ShapeDtypeStruct((t, B_QKV), _BF),
        jax.ShapeDtypeStruct((t, B_QKV), _BF),
        jax.ShapeDtypeStruct((t, C_QK), _BF),
        jax.ShapeDtypeStruct((nt, C_QK, tm), _BF),
        jax.ShapeDtypeStruct((t, C_V), _BF),
        jax.ShapeDtypeStruct((t, C_V), _BF),
        jax.ShapeDtypeStruct((t, N_BRANCH * D_MODEL), _BF),
    )
    out_specs = (
        tok_major(A_QK), tok_major(A_QK),
        pl.BlockSpec((1, A_V, tm), lambda i: (i, 0, 0)),
        tok_major(B_QKV), tok_major(B_QKV), tok_major(B_QKV),
        tok_major(C_QK),
        pl.BlockSpec((1, C_QK, tm), lambda i: (i, 0, 0)),
        tok_major(C_V), tok_major(C_V), tok_major(N_BRANCH * D_MODEL),
    )
    est = D_MODEL * IN_COLS * 2 + 2 * tm * IN_COLS * 2 + 4 * tm * D_MODEL * 4 + (12 << 20)
    return pl.pallas_call(
        _in_proj_kernel,
        grid=(nt,),
        in_specs=[
            pl.BlockSpec((tm, D_MODEL), row),
            _resident((1, D_MODEL), lambda i: (0, 0)),
            _layer_weight(layer, D_MODEL, IN_COLS),
            pl.BlockSpec((tm, LANES), lambda i: (i % pos_blocks, 0)),
            pl.BlockSpec((tm, LANES), lambda i: (i % pos_blocks, 0)),
            _resident((4, A_QK), lambda i: (0, 0)),
            _resident((MXU_DIM, MXU_DIM), lambda i: (0, 0)),
        ],
        out_specs=out_specs,
        out_shape=out_shapes,
        compiler_params=pltpu.CompilerParams(
            dimension_semantics=("parallel",), vmem_limit_bytes=_vmem_limit(est)),
        name="in_proj",
    )(x, gain, w, cos_t, sin_t, qk_gains, gmat)


LOG2E = math.log2(math.e)
SAFE_LOG2_SCORE = 60.0
NORM_MARGIN = 1.05
A_PAIR = 4


def _diff_attn_kernel(bound_ref, q_ref, k_ref, vt_ref, lamv_ref, subln_ref, o_ref, acc_ref, *,
                      lam_init):
    blk = ATTN_BLOCK
    i = pl.program_id(2)
    n_chain = 2 * A_PAIR
    per_group = MXU_DIM // D_HEAD
    lane = lax.broadcasted_iota(jnp.int32, (1, MXU_DIM), 1)
    qs = []
    for n in range(n_chain):
        g, r = divmod(n, per_group)
        qg = q_ref[:, g * MXU_DIM:(g + 1) * MXU_DIM]
        qs.append(jnp.where((lane >= D_HEAD * r) & (lane < D_HEAD * (r + 1)), qg,
                            jnp.zeros_like(qg)))

    def scores(j, n, diagonal):
        g = n // per_group
        kb = k_ref[pl.ds(pl.multiple_of(j * blk, blk), blk), g * MXU_DIM:(g + 1) * MXU_DIM]
        st = lax.dot_general(kb, qs[n], _NT, preferred_element_type=_F32)
        if diagonal:
            key_chunk = lax.broadcasted_iota(jnp.int32, (blk, blk), 0) // CHUNK
            qry_chunk = lax.broadcasted_iota(jnp.int32, (blk, blk), 1) // CHUNK
            st = jnp.where(key_chunk <= qry_chunk, st, NEG_BIG)
        return st

    def values_t(j, n):
        head = n // 2
        return vt_ref[j, head * A_DV:(head + 1) * A_DV, :]

    def plain_step(j, carry, diagonal):
        out = []
        st = scores(j, 0, diagonal)
        for n in range(n_chain):
            st_next = scores(j, n + 1, diagonal) if n + 1 < n_chain else None
            p = jnp.exp2(st)
            acc_ref[n] += jnp.dot(values_t(j, n), p.astype(_BF), preferred_element_type=_F32)
            out.append(carry[n] + jnp.sum(p, axis=0, keepdims=True))
            st = st_next
        return tuple(out)

    def shifted_step(j, carry, diagonal):
        out = []
        for n in range(n_chain):
            m_old, l_old = carry[n]
            st = scores(j, n, diagonal)
            m_new = jnp.maximum(m_old, jnp.max(st, axis=0, keepdims=True))
            alpha = jnp.exp2(m_old - m_new)
            p = jnp.exp2(st - m_new)
            l_new = alpha * l_old + jnp.sum(p, axis=0, keepdims=True)
            pv = jnp.dot(values_t(j, n), p.astype(_BF), preferred_element_type=_F32)
            acc_ref[n] = alpha * acc_ref[n] + pv
            out.append((m_new, l_new))
        return tuple(out)

    def finish(ls):
        lamv = lamv_ref[...]
        lam = (jnp.exp(jnp.sum(lamv[0:1] * lamv[1:2], axis=-1, keepdims=True))
               - jnp.exp(jnp.sum(lamv[2:3] * lamv[3:4], axis=-1, keepdims=True)) + lam_init)
        for head in range(A_PAIR):
            o = (acc_ref[2 * head] / ls[2 * head]
                 - lam * (acc_ref[2 * head + 1] / ls[2 * head + 1]))
            ms = jnp.mean(o * o, axis=0, keepdims=True)
            y = o * lax.rsqrt(ms + EPS) * subln_ref[...] * (1.0 - lam_init)
            o_ref[:, head * A_DV:(head + 1) * A_DV] = y.T.astype(_BF)

    acc_ref[...] = jnp.zeros_like(acc_ref)
    bounded = bound_ref[0] <= SAFE_LOG2_SCORE

    @pl.when(bounded)
    def _():
        init = tuple(jnp.zeros((1, blk), _F32) for _ in range(n_chain))
        carry = lax.fori_loop(0, i, lambda j, c: plain_step(j, c, False), init)
        finish(plain_step(i, carry, True))

    @pl.when(jnp.logical_not(bounded))
    def _():
        init = tuple((jnp.full((1, blk), NEG_BIG, _F32), jnp.zeros((1, blk), _F32))
                     for _ in range(n_chain))
        carry = lax.fori_loop(0, i, lambda j, c: shifted_step(j, c, False), init)
        finish([l for _, l in shifted_step(i, carry, True)])


def _diff_attn(score_bound, aq, ak, avt, lamv, subln_col, batch, seq, lam_init):
    blk = ATTN_BLOCK
    nq = seq // blk
    t = batch * seq
    width = A_PAIR * A_DV
    est = 2 * (seq * width * 2) * 2 + 2 * A_PAIR * (A_DV * blk * 4) + 16 * blk * blk * 4 + (8 << 20)
    return pl.pallas_call(
        functools.partial(_diff_attn_kernel, lam_init=lam_init),
        grid=(batch, A_HEADS // A_PAIR, nq),
        in_specs=[
            pl.BlockSpec(memory_space=pltpu.SMEM),
            pl.BlockSpec((blk, width), lambda b, h, i: (b * nq + i, h)),
            pl.BlockSpec((seq, width), lambda b, h, i: (b, h)),
            pl.BlockSpec((nq, width, blk), lambda b, h, i: (b, h, 0)),
            pl.BlockSpec((4, D_HEAD), lambda b, h, i: (0, 0)),
            pl.BlockSpec((A_DV, 1), lambda b, h, i: (0, 0)),
        ],
        out_specs=pl.BlockSpec((blk, width), lambda b, h, i: (b * nq + i, h)),
        out_shape=jax.ShapeDtypeStruct((t, A_V), _BF),
        scratch_shapes=[pltpu.VMEM((2 * A_PAIR, A_DV, blk), _F32)],
        compiler_params=pltpu.CompilerParams(
            dimension_semantics=("parallel", "parallel", "parallel"),
            vmem_limit_bytes=_vmem_limit(est)),
        name="diff_attn",
    )(score_bound, aq, ak, avt, lamv, subln_col)


def _score_bound(q_gain, k_gain):
    return (D_HEAD * NORM_MARGIN * jnp.max(jnp.abs(q_gain)) * jnp.max(jnp.abs(k_gain))).reshape(1)


B_TABLES = (B_WINDOW - B_QTILE) // B_QTILE + 1
B_GEN_LEN = 2048


def _build_bias_tables(gen_ref, mb_ref):
    qchunk = lax.broadcasted_iota(jnp.int32, (B_QTILE, B_WINDOW), 0) // CHUNK
    kchunk = lax.broadcasted_iota(jnp.int32, (B_QTILE, B_WINDOW), 1) // CHUNK
    for hh in range(2):
        rolled = pltpu.roll(jnp.broadcast_to(gen_ref[hh], (B_QTILE, B_GEN_LEN)), 0, 1,
                            stride=1, stride_axis=0)
        for t in range(B_TABLES):
            lo = B_WINDOW - t * B_QTILE
            dchunk = qchunk - kchunk + (t * B_QTILE) // CHUNK
            visible = (dchunk >= 0) & (dchunk <= B_LOOKBACK)
            mb_ref[hh, t] = jnp.where(visible, rolled[:, lo:lo + B_WINDOW], NEG_BIG)


def _chunk_attn_kernel(bound_ref, q_ref, k_ref, v_ref, gen_ref, o_ref, mb_ref):
    i = pl.program_id(2)

    @pl.when(i == 0)
    def _():
        _build_bias_tables(gen_ref, mb_ref)

    start = pl.multiple_of(jnp.maximum(i * B_QTILE - (B_WINDOW - B_QTILE), 0), B_QTILE)
    table = jnp.minimum(i, B_TABLES - 1)
    lane = lax.broadcasted_iota(jnp.int32, (1, LANES), 1)
    q = q_ref[...]
    zero = jnp.zeros_like(q)
    kw = k_ref[pl.ds(start, B_WINDOW), :]
    vw = v_ref[pl.ds(start, B_WINDOW), :]

    def attend(shifted):
        outs = []
        for hh in range(2):
            in_head = (lane >= D_HEAD) if hh else (lane < D_HEAD)
            s = lax.dot_general(jnp.where(in_head, q, zero), kw, _NT, preferred_element_type=_F32)
            s = s + mb_ref[hh, table]
            if shifted:
                s = s - jnp.max(s, axis=-1, keepdims=True)
            p = jnp.exp2(s)
            l = jnp.sum(p, axis=-1, keepdims=True)
            outs.append(jnp.dot(p.astype(_BF), vw, preferred_element_type=_F32) / l)
        o_ref[...] = jnp.where(lane < D_HEAD, outs[0], outs[1]).astype(_BF)

    bounded = bound_ref[0] <= SAFE_LOG2_SCORE
    pl.when(bounded)(lambda: attend(False))
    pl.when(jnp.logical_not(bounded))(lambda: attend(True))


def _chunk_attn(score_bound, bq, bk, bv, bias_gen, batch, seq):
    nq = seq // B_QTILE
    t = batch * seq
    est = 2 * B_TABLES * B_QTILE * B_WINDOW * 4 + 4 * B_QTILE * B_GEN_LEN * 4 \
        + 4 * seq * LANES * 2 + (10 << 20)
    return pl.pallas_call(
        _chunk_attn_kernel,
        grid=(batch, B_HEADS // 2, nq),
        in_specs=[
            pl.BlockSpec(memory_space=pltpu.SMEM),
            pl.BlockSpec((B_QTILE, LANES), lambda b, h, i: (b * nq + i, h)),
            pl.BlockSpec((seq, LANES), lambda b, h, i: (b, h)),
            pl.BlockSpec((seq, LANES), lambda b, h, i: (b, h)),
            pl.BlockSpec((2, 1, B_GEN_LEN), lambda b, h, i: (h, 0, 0)),
        ],
        out_specs=pl.BlockSpec((B_QTILE, LANES), lambda b, h, i: (b * nq + i, h)),
        out_shape=jax.ShapeDtypeStruct((t, B_QKV), _BF),
        scratch_shapes=[pltpu.VMEM((2, B_TABLES, B_QTILE, B_WINDOW), _F32)],
        compiler_params=pltpu.CompilerParams(
            dimension_semantics=("parallel", "parallel", "arbitrary"),
            vmem_limit_bytes=_vmem_limit(est)),
        name="chunk_attn",
    )(score_bound, bq, bk, bv, bias_gen)


def _bias_generator(rel_bias):
    b = rel_bias.astype(_F32) * LOG2E
    n_far = B_WINDOW - B_MAX_REL + 1
    mid = b[:, 1:CHUNK - 1 + B_MAX_REL][:, ::-1]
    n_neg = B_GEN_LEN - n_far - mid.shape[1]
    heads = b.shape[0]
    gen = jnp.concatenate([
        jnp.broadcast_to(b[:, -1:], (heads, n_far)), mid,
        jnp.broadcast_to(b[:, :1], (heads, n_neg))], axis=1)
    return gen[:, None, :]


def _retention_kernel(q_ref, kt_ref, v_ref, g_ref, cn_ref, o_ref, decay_ref, *, seq):
    blk = RET_BLOCK
    h = pl.program_id(1)
    hf = jnp.full((1, 1), h, jnp.int32).astype(_F32)
    log_gamma = jnp.log(1.0 - jnp.exp2(-5.0 - hf))
    diff = (lax.broadcasted_iota(jnp.int32, (blk, blk), 0)
            - lax.broadcasted_iota(jnp.int32, (blk, blk), 1)).astype(_F32)
    decay_ref[...] = jnp.where(diff >= 0, jnp.exp(log_gamma * jnp.maximum(diff, 0.0)), 0.0)
    pos = lax.broadcasted_iota(jnp.int32, (blk, LANES), 0).astype(_F32)
    xi = jnp.exp(log_gamma * (pos + 1.0))
    zeta = jnp.exp(log_gamma * (blk - 1.0 - pos))
    block_decay = jnp.exp(log_gamma * float(blk))
    odd = (h % 2) == 1
    q_keep = (lax.broadcasted_iota(jnp.int32, (1, LANES), 1) >= D_HEAD) == odd
    k_keep = (lax.broadcasted_iota(jnp.int32, (LANES, 1), 0) >= D_HEAD) == odd
    gain = cn_ref[0]

    def body(j, state):
        rows = pl.ds(pl.multiple_of(j * blk, blk), blk)
        qb = q_ref[rows, :]
        qb = jnp.where(q_keep, qb, jnp.zeros_like(qb))
        kt = kt_ref[j]
        kt = jnp.where(k_keep, kt, jnp.zeros_like(kt))
        vb = v_ref[rows, :]
        sc = jnp.dot(qb, kt, preferred_element_type=_F32) * decay_ref[...]
        inner = jnp.dot(sc.astype(_BF), vb, preferred_element_type=_F32)
        cross = jnp.dot(qb, state.astype(_BF), preferred_element_type=_F32) * xi
        o = inner + cross
        vz = (vb.astype(_F32) * zeta).astype(_BF)
        new_state = state * block_decay + jnp.dot(kt, vz, preferred_element_type=_F32)
        ms = jnp.mean(o * o, axis=-1, keepdims=True)
        y = o * lax.rsqrt(ms + EPS) * gain
        gg = g_ref[rows, :].astype(_F32)
        o_ref[rows, :] = (gg * jax.nn.sigmoid(gg) * y).astype(_BF)
        return new_state

    lax.fori_loop(0, seq // blk, body, jnp.zeros((LANES, C_DV), _F32), unroll=RET_UNROLL)


def _retention(cq, ckt, cv, cg, cnorm, batch, seq):
    blk = RET_BLOCK
    nb = seq // blk
    t = batch * seq
    est = 2 * 5 * seq * LANES * 2 + 6 * blk * blk * 4 + (8 << 20)
    return pl.pallas_call(
        functools.partial(_retention_kernel, seq=seq),
        grid=(batch, C_HEADS),
        in_specs=[
            pl.BlockSpec((seq, LANES), lambda b, h: (b, h // 2)),
            pl.BlockSpec((nb, LANES, blk), lambda b, h: (b, h // 2, 0)),
            pl.BlockSpec((seq, C_DV), lambda b, h: (b, h)),
            pl.BlockSpec((seq, C_DV), lambda b, h: (b, h)),
            pl.BlockSpec((1, 1, C_DV), lambda b, h: (h, 0, 0)),
        ],
        out_specs=pl.BlockSpec((seq, C_DV), lambda b, h: (b, h)),
        out_shape=jax.ShapeDtypeStruct((t, C_V), _BF),
        scratch_shapes=[pltpu.VMEM((blk, blk), _F32)],
        compiler_params=pltpu.CompilerParams(
            dimension_semantics=("parallel", "parallel"), vmem_limit_bytes=_vmem_limit(est)),
        name="retention",
    )(cq, ckt, cv, cg, cnorm)


def _merge_ffn_kernel(x_ref, ya_ref, yb_ref, yc_ref, gate_ref, wa_ref, wb_ref, wc_ref, wo_ref,
                      gain_ref, wg_ref, wu_ref, wd_ref, o_ref, acc_ref):
    merged = None
    for n, (y_ref, w_ref) in enumerate(((ya_ref, wa_ref), (yb_ref, wb_ref), (yc_ref, wc_ref))):
        br = jnp.dot(y_ref[...], w_ref[...], preferred_element_type=_F32)
        term = gate_ref[:, n * D_MODEL:(n + 1) * D_MODEL].astype(_F32) * br
        merged = term if merged is None else merged + term
    x = x_ref[...] + jnp.dot(merged.astype(_BF), wo_ref[...], preferred_element_type=_F32)
    h = _rms_rows(x, gain_ref[...]).astype(_BF)
    for c in range(D_FF // FF_CHUNK):
        sl = slice(c * FF_CHUNK, (c + 1) * FF_CHUNK)
        g = jnp.dot(h, wg_ref[:, sl], preferred_element_type=_F32)
        u = jnp.dot(h, wu_ref[:, sl], preferred_element_type=_F32)
        a = (g * jax.nn.sigmoid(g) * u).astype(_BF)
        part = jnp.dot(a, wd_ref[sl, :], preferred_element_type=_F32)
        if c == 0:
            acc_ref[...] = part
        else:
            acc_ref[...] += part
    o_ref[...] = x + 0.5 * acc_ref[...]


def _merge_ffn(layer, x, ya, yb, yc, gates, wa, wb, wc, wo, gain, wg, wu, wd):
    t = x.shape[0]
    tm = TOKEN_TILE
    row = lambda i: (i, 0)
    const = lambda i: (0, 0)
    est = (5 * D_MODEL * D_MODEL + 3 * D_MODEL * D_FF) * 2 \
        + 2 * tm * (2 * D_MODEL * 4 + 3 * 512 * 2 + 3 * D_MODEL * 2) \
        + 8 * tm * D_MODEL * 4 + 4 * tm * FF_CHUNK * 4 + (8 << 20)
    return pl.pallas_call(
        _merge_ffn_kernel,
        grid=(t // tm,),
        in_specs=[
            pl.BlockSpec((tm, D_MODEL), row),
            pl.BlockSpec((tm, A_V), row),
            pl.BlockSpec((tm, B_QKV), row),
            pl.BlockSpec((tm, C_V), row),
            pl.BlockSpec((tm, N_BRANCH * D_MODEL), row),
            _layer_weight(layer, A_V, D_MODEL),
            _layer_weight(layer, B_QKV, D_MODEL),
            _layer_weight(layer, C_V, D_MODEL),
            _layer_weight(layer, D_MODEL, D_MODEL),
            _resident((1, D_MODEL), const),
            _layer_weight(layer, D_MODEL, D_FF),
            _layer_weight(layer, D_MODEL, D_FF),
            _layer_weight(layer, D_FF, D_MODEL),
        ],
        out_specs=pl.BlockSpec((tm, D_MODEL), row),
        out_shape=jax.ShapeDtypeStruct((t, D_MODEL), _F32),
        scratch_shapes=[pltpu.VMEM((tm, D_MODEL), _F32)],
        compiler_params=pltpu.CompilerParams(
            dimension_semantics=("parallel",), vmem_limit_bytes=_vmem_limit(est)),
        name="merge_ffn",
    )(x, ya, yb, yc, gates, wa, wb, wc, wo, gain, wg, wu, wd)


def _rope_tables(seq):
    half = D_HEAD // 2
    inv = ROPE_THETA ** (-jnp.arange(half, dtype=_F32) / half)
    ang = jnp.arange(seq, dtype=_F32)[:, None] * inv[None, :]
    cos, sin = jnp.cos(ang), jnp.sin(ang)
    cos_t = jnp.concatenate([cos, cos] * (LANES // D_HEAD), axis=1)
    sin_t = jnp.concatenate([-sin, sin] * (LANES // D_HEAD), axis=1)
    return cos_t, sin_t


def _head_mean_matrix():
    g = np.kron(np.eye(MXU_DIM // D_HEAD), np.full((D_HEAD, D_HEAD), 1.0 / D_HEAD))
    return jnp.asarray(g, dtype=_BF)


def kernel(x, ffn1_norm, ffn1_w_gate, ffn1_w_up, ffn1_w_down, mix_norm, w_in, a_q_norm, a_k_norm, a_lambda_q1, a_lambda_k1, a_lambda_q2, a_lambda_k2, a_subln, b_q_norm, b_k_norm, b_rel_bias, c_out_norm, w_branch_a, w_branch_b, w_branch_c, w_out, ffn2_norm, ffn2_w_gate, ffn2_w_up, ffn2_w_down):
    batch, seq, d = x.shape
    assert d == D_MODEL and seq % ATTN_BLOCK == 0 and seq % RET_BLOCK == 0
    assert seq % TOKEN_TILE == 0 and seq >= B_WINDOW and ATTN_BLOCK == TOKEN_TILE == RET_BLOCK
    scale = D_HEAD ** -0.5
    cos_t, sin_t = _rope_tables(seq)
    gmat = _head_mean_matrix()
    bf = lambda w: w.astype(_BF)
    f1g, f1u, f1d = bf(ffn1_w_gate), bf(ffn1_w_up), bf(ffn1_w_down)
    f2g, f2u, f2d = bf(ffn2_w_gate), bf(ffn2_w_up), bf(ffn2_w_down)
    win, wba, wbb, wbc, wo = bf(w_in), bf(w_branch_a), bf(w_branch_b), bf(w_branch_c), bf(w_out)
    xt = x.reshape(batch * seq, D_MODEL)
    for l in range(DEPTH):
        lam_init = 0.8 - 0.6 * math.exp(-0.3 * l)
        xt = _ffn(l, xt, ffn1_norm[l][None, :], f1g, f1u, f1d)

        heads = A_QK // D_HEAD
        qk_gains = jnp.stack([
            jnp.tile(a_q_norm[l].astype(_F32) * (scale * LOG2E), heads),
            jnp.tile(a_k_norm[l].astype(_F32), heads),
            jnp.tile(b_q_norm[l].astype(_F32) * (scale * LOG2E), heads),
            jnp.tile(b_k_norm[l].astype(_F32), heads),
        ])
        aq, ak, avt, bq, bk, bv, cq, ckt, cv, cg, gates = _in_proj(
            l, xt, mix_norm[l][None, :], win, cos_t, sin_t, qk_gains, gmat, seq)

        lamv = jnp.stack([a_lambda_q1[l], a_lambda_k1[l], a_lambda_q2[l], a_lambda_k2[l]]).astype(_F32)
        a_bound = _score_bound(qk_gains[0, :D_HEAD], qk_gains[1, :D_HEAD])
        ya = _diff_attn(a_bound, aq, ak, avt, lamv, a_subln[l].astype(_F32)[:, None], batch, seq, lam_init)
        bias_gen = _bias_generator(b_rel_bias[l])
        b_bound = _score_bound(qk_gains[2, :D_HEAD], qk_gains[3, :D_HEAD]) + jnp.max(jnp.abs(bias_gen))
        yb = _chunk_attn(b_bound, bq, bk, bv, bias_gen, batch, seq)
        yc = _retention(cq, ckt, cv, cg, c_out_norm[l].astype(_F32)[:, None, :], batch, seq)

        xt = _merge_ffn(l, xt, ya, yb, yc, gates, wba, wbb, wbc, wo, ffn2_norm[l][None, :],
                        f2g, f2u, f2d)
    return xt.reshape(batch, seq, D_MODEL)
```

```python
import functools
import math

import numpy as np
import jax
import jax.numpy as jnp
from jax import lax
from jax.experimental import pallas as pl
from jax.experimental.pallas import tpu as pltpu

D_MODEL = 1024
DEPTH = 2
CHUNK = 64
D_HEAD = 64
ROPE_THETA = 10000.0
EPS = 1e-6
A_HEADS = 4
A_DV = 2 * D_HEAD
B_HEADS = 8
B_LOOKBACK = 8
B_MAX_REL = 256
C_HEADS = 4
C_DV = 2 * D_HEAD
N_BRANCH = 3
D_FF = 2816

A_QK = A_HEADS * 2 * D_HEAD
A_V = A_HEADS * A_DV
B_QKV = B_HEADS * D_HEAD
C_QK = C_HEADS * D_HEAD
C_V = C_HEADS * C_DV
SPLITS = (A_QK, A_QK, A_V, B_QKV, B_QKV, B_QKV, C_QK, C_QK, C_V, C_V, N_BRANCH * D_MODEL)
IN_COLS = sum(SPLITS)
_OFFS = tuple(int(v) for v in np.cumsum((0,) + SPLITS))

LANES = 128
MXU_DIM = 256
VMEM_BYTES_V7X = 64 * 1024 * 1024
VMEM_HEADROOM = 6 * 1024 * 1024

TOKEN_TILE = 512
FF_CHUNK = MXU_DIM
PROJ_CHUNK = 512
ATTN_BLOCK = 512
B_QTILE = 256
B_WINDOW = B_QTILE + B_LOOKBACK * CHUNK
RET_BLOCK = 512
RET_UNROLL = 4
NEG_BIG = -1e30

_BF = jnp.bfloat16
_F32 = jnp.float32
_NT = (((1,), (1,)), ((), ()))


def _vmem_limit(estimate_bytes):
    return int(min(VMEM_BYTES_V7X - VMEM_HEADROOM, max(32 * 1024 * 1024, estimate_bytes)))


def _resident(shape, index_map):
    return pl.BlockSpec(shape, index_map, pipeline_mode=pl.Buffered(1))


def _layer_weight(layer, rows, cols):
    return pl.BlockSpec((None, rows, cols), lambda i: (layer, 0, 0), pipeline_mode=pl.Buffered(1))


def _rms_rows(x, gain):
    ms = jnp.mean(x * x, axis=-1, keepdims=True)
    return x * lax.rsqrt(ms + EPS) * gain


def _ffn_kernel(x_ref, gain_ref, wg_ref, wu_ref, wd_ref, o_ref, acc_ref):
    x = x_ref[...]
    h = _rms_rows(x, gain_ref[...]).astype(_BF)
    for c in range(D_FF // FF_CHUNK):
        sl = slice(c * FF_CHUNK, (c + 1) * FF_CHUNK)
        g = jnp.dot(h, wg_ref[:, sl], preferred_element_type=_F32)
        u = jnp.dot(h, wu_ref[:, sl], preferred_element_type=_F32)
        a = (g * jax.nn.sigmoid(g) * u).astype(_BF)
        part = jnp.dot(a, wd_ref[sl, :], preferred_element_type=_F32)
        if c == 0:
            acc_ref[...] = part
        else:
            acc_ref[...] += part
    o_ref[...] = x + 0.5 * acc_ref[...]


def _ffn(layer, x, gain, wg, wu, wd):
    t = x.shape[0]
    tm = TOKEN_TILE
    est = 3 * D_MODEL * D_FF * 2 + 5 * tm * D_MODEL * 4 + 4 * tm * FF_CHUNK * 4 + (8 << 20)
    return pl.pallas_call(
        _ffn_kernel,
        grid=(t // tm,),
        in_specs=[
            pl.BlockSpec((tm, D_MODEL), lambda i: (i, 0)),
            _resident((1, D_MODEL), lambda i: (0, 0)),
            _layer_weight(layer, D_MODEL, D_FF),
            _layer_weight(layer, D_MODEL, D_FF),
            _layer_weight(layer, D_FF, D_MODEL),
        ],
        out_specs=pl.BlockSpec((tm, D_MODEL), lambda i: (i, 0)),
        out_shape=jax.ShapeDtypeStruct((t, D_MODEL), _F32),
        scratch_shapes=[pltpu.VMEM((tm, D_MODEL), _F32)],
        compiler_params=pltpu.CompilerParams(
            dimension_semantics=("parallel",), vmem_limit_bytes=_vmem_limit(est)),
        name="ffn",
    )(x, gain, wg, wu, wd)


def _swap_halves(x):
    lane = lax.broadcasted_iota(jnp.int32, (1, LANES), 1)
    upper = (lane & (D_HEAD // 2)) != 0
    outs = []
    for s in range(x.shape[1] // LANES):
        xs = x[:, s * LANES:(s + 1) * LANES]
        from_below = pltpu.roll(xs, D_HEAD // 2, 1)
        from_above = pltpu.roll(xs, LANES - D_HEAD // 2, 1)
        outs.append(jnp.where(upper, from_below, from_above))
    return outs[0] if len(outs) == 1 else jnp.concatenate(outs, axis=1)


def _tile_lanes(t, n):
    reps = n // t.shape[1]
    return t if reps == 1 else jnp.concatenate([t] * reps, axis=1)


def _rope(x, cos, sin):
    n = x.shape[1]
    return x * _tile_lanes(cos, n) + _swap_halves(x) * _tile_lanes(sin, n)


def _head_ms(p, gmat):
    sq = (p * p).astype(_BF)
    outs = [jnp.dot(sq[:, s:s + MXU_DIM], gmat, preferred_element_type=_F32)
            for s in range(0, p.shape[1], MXU_DIM)]
    return outs[0] if len(outs) == 1 else jnp.concatenate(outs, axis=1)


def _in_proj_kernel(x_ref, gain_ref, w_ref, cos_ref, sin_ref, qkg_ref, gmat_ref,
                    aq_ref, ak_ref, avt_ref, bq_ref, bk_ref, bv_ref,
                    cq_ref, ckt_ref, cv_ref, cg_ref, gate_ref):
    h = _rms_rows(x_ref[...], gain_ref[...]).astype(_BF)
    cos = cos_ref[...]
    sin = sin_ref[...]
    gmat = gmat_ref[...]

    def proj(lo, width):
        return jnp.dot(h, w_ref[:, lo:lo + width], preferred_element_type=_F32)

    def normed(p, row):
        return p * lax.rsqrt(_head_ms(p, gmat) + EPS) * qkg_ref[row:row + 1, :]

    aq_ref[...] = _rope(normed(proj(_OFFS[0], A_QK), 0), cos, sin).astype(_BF)
    ak_ref[...] = _rope(normed(proj(_OFFS[1], A_QK), 1), cos, sin).astype(_BF)
    avt_ref[0] = proj(_OFFS[2], A_V).T.astype(_BF)
    bq_ref[...] = normed(proj(_OFFS[3], B_QKV), 2).astype(_BF)
    bk_ref[...] = normed(proj(_OFFS[4], B_QKV), 3).astype(_BF)
    bv_ref[...] = proj(_OFFS[5], B_QKV).astype(_BF)
    cq_ref[...] = _rope(proj(_OFFS[6], C_QK), cos, sin).astype(_BF)
    ckt_ref[0] = (_rope(proj(_OFFS[7], C_QK), cos, sin) * (D_HEAD ** -0.5)).T.astype(_BF)
    cv_ref[...] = proj(_OFFS[8], C_V).astype(_BF)
    cg_ref[...] = proj(_OFFS[9], C_V).astype(_BF)
    for c in range(N_BRANCH * D_MODEL // PROJ_CHUNK):
        lo = c * PROJ_CHUNK
        gate_ref[:, lo:lo + PROJ_CHUNK] = jax.nn.sigmoid(
            proj(_OFFS[10] + lo, PROJ_CHUNK)).astype(_BF)


def _in_proj(layer, x, gain, w, cos_t, sin_t, qk_gains, gmat, seq):
    t = x.shape[0]
    tm = TOKEN_TILE
    nt = t // tm
    pos_blocks = seq // tm
    row = lambda i: (i, 0)
    tok_major = lambda n: pl.BlockSpec((tm, n), row)
    out_shapes = (
        jax.ShapeDtypeStruct((t, A_QK), _BF),
        jax.ShapeDtypeStruct((t, A_QK), _BF),
        jax.ShapeDtypeStruct((nt, A_V, tm), _BF),
        jax.ShapeDtypeStruct((t, B_QKV), _BF),
        jax.ShapeDtypeStruct((t, B_QKV), _BF),
        jax.ShapeDtypeStruct((t, B_QKV), _BF),
        jax.ShapeDtypeStruct((t, C_QK), _BF),
        jax.ShapeDtypeStruct((nt, C_QK, tm), _BF),
        jax.ShapeDtypeStruct((t, C_V), _BF),
        jax.ShapeDtypeStruct((t, C_V), _BF),
        jax.ShapeDtypeStruct((t, N_BRANCH * D_MODEL), _BF),
    )
    out_specs = (
        tok_major(A_QK), tok_major(A_QK),
        pl.BlockSpec((1, A_V, tm), lambda i: (i, 0, 0)),
        tok_major(B_QKV), tok_major(B_QKV), tok_major(B_QKV),
        tok_major(C_QK),
        pl.BlockSpec((1, C_QK, tm), lambda i: (i, 0, 0)),
        tok_major(C_V), tok_major(C_V), tok_major(N_BRANCH * D_MODEL),
    )
    est = D_MODEL * IN_COLS * 2 + 2 * tm * IN_COLS * 2 + 4 * tm * D_MODEL * 4 + (12 << 20)
    return pl.pallas_call(
        _in_proj_kernel,
        grid=(nt,),
        in_specs=[
            pl.BlockSpec((tm, D_MODEL), row),
            _resident((1, D_MODEL), lambda i: (0, 0)),
            _layer_weight(layer, D_MODEL, IN_COLS),
            pl.BlockSpec((tm, LANES), lambda i: (i % pos_blocks, 0)),
            pl.BlockSpec((tm, LANES), lambda i: (i % pos_blocks, 0)),
            _resident((4, A_QK), lambda i: (0, 0)),
            _resident((MXU_DIM, MXU_DIM), lambda i: (0, 0)),
        ],
        out_specs=out_specs,
        out_shape=out_shapes,
        compiler_params=pltpu.CompilerParams(
            dimension_semantics=("parallel",), vmem_limit_bytes=_vmem_limit(est)),
        name="in_proj",
    )(x, gain, w, cos_t, sin_t, qk_gains, gmat)


LOG2E = math.log2(math.e)
SAFE_LOG2_SCORE = 60.0
NORM_MARGIN = 1.05
A_PAIR = 4


def _diff_attn_kernel(bound_ref, q_ref, k_ref, vt_ref, lamv_ref, subln_ref, o_ref, acc_ref, *,
                      lam_init):
    blk = ATTN_BLOCK
    i = pl.program_id(2)
    n_chain = 2 * A_PAIR
    per_group = MXU_DIM // D_HEAD
    lane = lax.broadcasted_iota(jnp.int32, (1, MXU_DIM), 1)
    qs = []
    for n in range(n_chain):
        g, r = divmod(n, per_group)
        qg = q_ref[:, g * MXU_DIM:(g + 1) * MXU_DIM]
        qs.append(jnp.where((lane >= D_HEAD * r) & (lane < D_HEAD * (r + 1)), qg,
                            jnp.zeros_like(qg)))

    def scores(j, n, diagonal):
        g = n // per_group
        kb = k_ref[pl.ds(pl.multiple_of(j * blk, blk), blk), g * MXU_DIM:(g + 1) * MXU_DIM]
        st = lax.dot_general(kb, qs[n], _NT, preferred_element_type=_F32)
        if diagonal:
            key_chunk = lax.broadcasted_iota(jnp.int32, (blk, blk), 0) // CHUNK
            qry_chunk = lax.broadcasted_iota(jnp.int32, (blk, blk), 1) // CHUNK
            st = jnp.where(key_chunk <= qry_chunk, st, NEG_BIG)
        return st

    def values_t(j, n):
        head = n // 2
        return vt_ref[j, head * A_DV:(head + 1) * A_DV, :]

    def plain_step(j, carry, diagonal):
        out = []
        st = scores(j, 0, diagonal)
        for n in range(n_chain):
            st_next = scores(j, n + 1, diagonal) if n + 1 < n_chain else None
            p = jnp.exp2(st)
            acc_ref[n] += jnp.dot(values_t(j, n), p.astype(_BF), preferred_element_type=_F32)
            out.append(carry[n] + jnp.sum(p, axis=0, keepdims=True))
            st = st_next
        return tuple(out)

    def shifted_step(j, carry, diagonal):
        out = []
        for n in range(n_chain):
            m_old, l_old = carry[n]
            st = scores(j, n, diagonal)
            m_new = jnp.maximum(m_old, jnp.max(st, axis=0, keepdims=True))
            alpha = jnp.exp2(m_old - m_new)
            p = jnp.exp2(st - m_new)
            l_new = alpha * l_old + jnp.sum(p, axis=0, keepdims=True)
            pv = jnp.dot(values_t(j, n), p.astype(_BF), preferred_element_type=_F32)
            acc_ref[n] = alpha * acc_ref[n] + pv
            out.append((m_new, l_new))
        return tuple(out)

    def finish(ls):
        lamv = lamv_ref[...]
        lam = (jnp.exp(jnp.sum(lamv[0:1] * lamv[1:2], axis=-1, keepdims=True))
               - jnp.exp(jnp.sum(lamv[2:3] * lamv[3:4], axis=-1, keepdims=True)) + lam_init)
        for head in range(A_PAIR):
            o = (acc_ref[2 * head] / ls[2 * head]
                 - lam * (acc_ref[2 * head + 1] / ls[2 * head + 1]))
            ms = jnp.mean(o * o, axis=0, keepdims=True)
            y = o * lax.rsqrt(ms + EPS) * subln_ref[...] * (1.0 - lam_init)
            o_ref[:, head * A_DV:(head + 1) * A_DV] = y.T.astype(_BF)

    acc_ref[...] = jnp.zeros_like(acc_ref)
    bounded = bound_ref[0] <= SAFE_LOG2_SCORE

    @pl.when(bounded)
    def _():
        init = tuple(jnp.zeros((1, blk), _F32) for _ in range(n_chain))
        carry = lax.fori_loop(0, i, lambda j, c: plain_step(j, c, False), init)
        finish(plain_step(i, carry, True))

    @pl.when(jnp.logical_not(bounded))
    def _():
        init = tuple((jnp.full((1, blk), NEG_BIG, _F32), jnp.zeros((1, blk), _F32))
                     for _ in range(n_chain))
        carry = lax.fori_loop(0, i, lambda j, c: shifted_step(j, c, False), init)
        finish([l for _, l in shifted_step(i, carry, True)])


def _diff_attn(score_bound, aq, ak, avt, lamv, subln_col, batch, seq, lam_init):
    blk = ATTN_BLOCK
    nq = seq // blk
    t = batch * seq
    width = A_PAIR * A_DV
    est = 2 * (seq * width * 2) * 2 + 2 * A_PAIR * (A_DV * blk * 4) + 16 * blk * blk * 4 + (8 << 20)
    return pl.pallas_call(
        functools.partial(_diff_attn_kernel, lam_init=lam_init),
        grid=(batch, A_HEADS // A_PAIR, nq),
        in_specs=[
            pl.BlockSpec(memory_space=pltpu.SMEM),
            pl.BlockSpec((blk, width), lambda b, h, i: (b * nq + i, h)),
            pl.BlockSpec((seq, width), lambda b, h, i: (b, h)),
            pl.BlockSpec((nq, width, blk), lambda b, h, i: (b, h, 0)),
            pl.BlockSpec((4, D_HEAD), lambda b, h, i: (0, 0)),
            pl.BlockSpec((A_DV, 1), lambda b, h, i: (0, 0)),
        ],
        out_specs=pl.BlockSpec((blk, width), lambda b, h, i: (b * nq + i, h)),
        out_shape=jax.ShapeDtypeStruct((t, A_V), _BF),
        scratch_shapes=[pltpu.VMEM((2 * A_PAIR, A_DV, blk), _F32)],
        compiler_params=pltpu.CompilerParams(
            dimension_semantics=("parallel", "parallel", "parallel"),
            vmem_limit_bytes=_vmem_limit(est)),
        name="diff_attn",
    )(score_bound, aq, ak, avt, lamv, subln_col)


def _score_bound(q_gain, k_gain):
    return (D_HEAD * NORM_MARGIN * jnp.max(jnp.abs(q_gain)) * jnp.max(jnp.abs(k_gain))).reshape(1)


B_TABLES = (B_WINDOW - B_QTILE) // B_QTILE + 1
B_GEN_LEN = 2048
B_GROUP = MXU_DIM // D_HEAD


def _build_bias_tables(gen_ref, mb_ref):
    qchunk = lax.broadcasted_iota(jnp.int32, (B_QTILE, B_WINDOW), 0) // CHUNK
    kchunk = lax.broadcasted_iota(jnp.int32, (B_QTILE, B_WINDOW), 1) // CHUNK
    for hh in range(B_GROUP):
        rolled = pltpu.roll(jnp.broadcast_to(gen_ref[hh], (B_QTILE, B_GEN_LEN)), 0, 1,
                            stride=1, stride_axis=0)
        for t in range(B_TABLES):
            lo = B_WINDOW - t * B_QTILE
            dchunk = qchunk - kchunk + (t * B_QTILE) // CHUNK
            visible = (dchunk >= 0) & (dchunk <= B_LOOKBACK)
            mb_ref[hh, t] = jnp.where(visible, rolled[:, lo:lo + B_WINDOW], NEG_BIG)


def _chunk_attn_kernel(bound_ref, q_ref, k_ref, v_ref, gen_ref, o_ref, mb_ref):
    i = pl.program_id(2)

    @pl.when(i == 0)
    def _():
        _build_bias_tables(gen_ref, mb_ref)

    start = pl.multiple_of(jnp.maximum(i * B_QTILE - (B_WINDOW - B_QTILE), 0), B_QTILE)
    table = jnp.minimum(i, B_TABLES - 1)
    lane = lax.broadcasted_iota(jnp.int32, (1, MXU_DIM), 1)
    q = q_ref[...]
    zero = jnp.zeros_like(q)
    kw = k_ref[pl.ds(start, B_WINDOW), :]
    vw = v_ref[pl.ds(start, B_WINDOW), :]
    in_head = [(lane >= D_HEAD * hh) & (lane < D_HEAD * (hh + 1)) for hh in range(B_GROUP)]

    def logits(hh):
        s = lax.dot_general(jnp.where(in_head[hh], q, zero), kw, _NT, preferred_element_type=_F32)
        return s + mb_ref[hh, table]

    def attend(shifted):
        out = None
        s = logits(0)
        for hh in range(B_GROUP):
            s_next = logits(hh + 1) if hh + 1 < B_GROUP else None
            if shifted:
                s = s - jnp.max(s, axis=-1, keepdims=True)
            p = jnp.exp2(s)
            l = jnp.sum(p, axis=-1, keepdims=True)
            o = jnp.dot(p.astype(_BF), vw, preferred_element_type=_F32) / l
            out = o if out is None else jnp.where(in_head[hh], o, out)
            s = s_next
        o_ref[...] = out.astype(_BF)

    bounded = bound_ref[0] <= SAFE_LOG2_SCORE
    pl.when(bounded)(lambda: attend(False))
    pl.when(jnp.logical_not(bounded))(lambda: attend(True))


def _chunk_attn(score_bound, bq, bk, bv, bias_gen, batch, seq):
    nq = seq // B_QTILE
    t = batch * seq
    est = B_GROUP * B_TABLES * B_QTILE * B_WINDOW * 4 + 4 * B_QTILE * B_GEN_LEN * 4 \
        + 4 * seq * MXU_DIM * 2 + (12 << 20)
    return pl.pallas_call(
        _chunk_attn_kernel,
        grid=(batch, B_HEADS // B_GROUP, nq),
        in_specs=[
            pl.BlockSpec(memory_space=pltpu.SMEM),
            pl.BlockSpec((B_QTILE, MXU_DIM), lambda b, h, i: (b * nq + i, h)),
            pl.BlockSpec((seq, MXU_DIM), lambda b, h, i: (b, h)),
            pl.BlockSpec((seq, MXU_DIM), lambda b, h, i: (b, h)),
            pl.BlockSpec((B_GROUP, 1, B_GEN_LEN), lambda b, h, i: (h, 0, 0)),
        ],
        out_specs=pl.BlockSpec((B_QTILE, MXU_DIM), lambda b, h, i: (b * nq + i, h)),
        out_shape=jax.ShapeDtypeStruct((t, B_QKV), _BF),
        scratch_shapes=[pltpu.VMEM((B_GROUP, B_TABLES, B_QTILE, B_WINDOW), _F32)],
        compiler_params=pltpu.CompilerParams(
            dimension_semantics=("parallel", "parallel", "arbitrary"),
            vmem_limit_bytes=_vmem_limit(est)),
        name="chunk_attn",
    )(score_bound, bq, bk, bv, bias_gen)


def _bias_generator(rel_bias):
    b = rel_bias.astype(_F32) * LOG2E
    n_far = B_WINDOW - B_MAX_REL + 1
    mid = b[:, 1:CHUNK - 1 + B_MAX_REL][:, ::-1]
    n_neg = B_GEN_LEN - n_far - mid.shape[1]
    heads = b.shape[0]
    gen = jnp.concatenate([
        jnp.broadcast_to(b[:, -1:], (heads, n_far)), mid,
        jnp.broadcast_to(b[:, :1], (heads, n_neg))], axis=1)
    return gen[:, None, :]


def _retention_kernel(q_ref, kt_ref, v_ref, g_ref, cn_ref, o_ref, decay_ref, *, seq):
    blk = RET_BLOCK
    h = pl.program_id(1)
    hf = jnp.full((1, 1), h, jnp.int32).astype(_F32)
    log_gamma = jnp.log(1.0 - jnp.exp2(-5.0 - hf))
    diff = (lax.broadcasted_iota(jnp.int32, (blk, blk), 0)
            - lax.broadcasted_iota(jnp.int32, (blk, blk), 1)).astype(_F32)
    decay_ref[...] = jnp.where(diff >= 0, jnp.exp(log_gamma * jnp.maximum(diff, 0.0)), 0.0)
    pos = lax.broadcasted_iota(jnp.int32, (blk, LANES), 0).astype(_F32)
    xi = jnp.exp(log_gamma * (pos + 1.0))
    zeta = jnp.exp(log_gamma * (blk - 1.0 - pos))
    block_decay = jnp.exp(log_gamma * float(blk))
    odd = (h % 2) == 1
    q_keep = (lax.broadcasted_iota(jnp.int32, (1, LANES), 1) >= D_HEAD) == odd
    k_keep = (lax.broadcasted_iota(jnp.int32, (LANES, 1), 0) >= D_HEAD) == odd
    gain = cn_ref[0]

    def body(j, state):
        rows = pl.ds(pl.multiple_of(j * blk, blk), blk)
        qb = q_ref[rows, :]
        qb = jnp.where(q_keep, qb, jnp.zeros_like(qb))
        kt = kt_ref[j]
        kt = jnp.where(k_keep, kt, jnp.zeros_like(kt))
        vb = v_ref[rows, :]
        sc = jnp.dot(qb, kt, preferred_element_type=_F32) * decay_ref[...]
        inner = jnp.dot(sc.astype(_BF), vb, preferred_element_type=_F32)
        cross = jnp.dot(qb, state.astype(_BF), preferred_element_type=_F32) * xi
        o = inner + cross
        vz = (vb.astype(_F32) * zeta).astype(_BF)
        new_state = state * block_decay + jnp.dot(kt, vz, preferred_element_type=_F32)
        ms = jnp.mean(o * o, axis=-1, keepdims=True)
        y = o * lax.rsqrt(ms + EPS) * gain
        gg = g_ref[rows, :].astype(_F32)
        o_ref[rows, :] = (gg * jax.nn.sigmoid(gg) * y).astype(_BF)
        return new_state

    lax.fori_loop(0, seq // blk, body, jnp.zeros((LANES, C_DV), _F32), unroll=RET_UNROLL)


def _retention(cq, ckt, cv, cg, cnorm, batch, seq):
    blk = RET_BLOCK
    nb = seq // blk
    t = batch * seq
    est = 2 * 5 * seq * LANES * 2 + 6 * blk * blk * 4 + (8 << 20)
    return pl.pallas_call(
        functools.partial(_retention_kernel, seq=seq),
        grid=(batch, C_HEADS),
        in_specs=[
            pl.BlockSpec((seq, LANES), lambda b, h: (b, h // 2)),
            pl.BlockSpec((nb, LANES, blk), lambda b, h: (b, h // 2, 0)),
            pl.BlockSpec((seq, C_DV), lambda b, h: (b, h)),
            pl.BlockSpec((seq, C_DV), lambda b, h: (b, h)),
            pl.BlockSpec((1, 1, C_DV), lambda b, h: (h, 0, 0)),
        ],
        out_specs=pl.BlockSpec((seq, C_DV), lambda b, h: (b, h)),
        out_shape=jax.ShapeDtypeStruct((t, C_V), _BF),
        scratch_shapes=[pltpu.VMEM((blk, blk), _F32)],
        compiler_params=pltpu.CompilerParams(
            dimension_semantics=("parallel", "parallel"), vmem_limit_bytes=_vmem_limit(est)),
        name="retention",
    )(cq, ckt, cv, cg, cnorm)


def _merge_ffn_kernel(x_ref, ya_ref, yb_ref, yc_ref, gate_ref, wa_ref, wb_ref, wc_ref, wo_ref,
                      gain_ref, wg_ref, wu_ref, wd_ref, o_ref, acc_ref):
    merged = None
    for n, (y_ref, w_ref) in enumerate(((ya_ref, wa_ref), (yb_ref, wb_ref), (yc_ref, wc_ref))):
        br = jnp.dot(y_ref[...], w_ref[...], preferred_element_type=_F32)
        term = gate_ref[:, n * D_MODEL:(n + 1) * D_MODEL].astype(_F32) * br
        merged = term if merged is None else merged + term
    x = x_ref[...] + jnp.dot(merged.astype(_BF), wo_ref[...], preferred_element_type=_F32)
    h = _rms_rows(x, gain_ref[...]).astype(_BF)
    for c in range(D_FF // FF_CHUNK):
        sl = slice(c * FF_CHUNK, (c + 1) * FF_CHUNK)
        g = jnp.dot(h, wg_ref[:, sl], preferred_element_type=_F32)
        u = jnp.dot(h, wu_ref[:, sl], preferred_element_type=_F32)
        a = (g * jax.nn.sigmoid(g) * u).astype(_BF)
        part = jnp.dot(a, wd_ref[sl, :], preferred_element_type=_F32)
        if c == 0:
            acc_ref[...] = part
        else:
            acc_ref[...] += part
    o_ref[...] = x + 0.5 * acc_ref[...]


def _merge_ffn(layer, x, ya, yb, yc, gates, wa, wb, wc, wo, gain, wg, wu, wd):
    t = x.shape[0]
    tm = TOKEN_TILE
    row = lambda i: (i, 0)
    const = lambda i: (0, 0)
    est = (5 * D_MODEL * D_MODEL + 3 * D_MODEL * D_FF) * 2 \
        + 2 * tm * (2 * D_MODEL * 4 + 3 * 512 * 2 + 3 * D_MODEL * 2) \
        + 8 * tm * D_MODEL * 4 + 4 * tm * FF_CHUNK * 4 + (8 << 20)
    return pl.pallas_call(
        _merge_ffn_kernel,
        grid=(t // tm,),
        in_specs=[
            pl.BlockSpec((tm, D_MODEL), row),
            pl.BlockSpec((tm, A_V), row),
            pl.BlockSpec((tm, B_QKV), row),
            pl.BlockSpec((tm, C_V), row),
            pl.BlockSpec((tm, N_BRANCH * D_MODEL), row),
            _layer_weight(layer, A_V, D_MODEL),
            _layer_weight(layer, B_QKV, D_MODEL),
            _layer_weight(layer, C_V, D_MODEL),
            _layer_weight(layer, D_MODEL, D_MODEL),
            _resident((1, D_MODEL), const),
            _layer_weight(layer, D_MODEL, D_FF),
            _layer_weight(layer, D_MODEL, D_FF),
            _layer_weight(layer, D_FF, D_MODEL),
        ],
        out_specs=pl.BlockSpec((tm, D_MODEL), row),
        out_shape=jax.ShapeDtypeStruct((t, D_MODEL), _F32),
        scratch_shapes=[pltpu.VMEM((tm, D_MODEL), _F32)],
        compiler_params=pltpu.CompilerParams(
            dimension_semantics=("parallel",), vmem_limit_bytes=_vmem_limit(est)),
        name="merge_ffn",
    )(x, ya, yb, yc, gates, wa, wb, wc, wo, gain, wg, wu, wd)


def _rope_tables(seq):
    half = D_HEAD // 2
    inv = ROPE_THETA ** (-jnp.arange(half, dtype=_F32) / half)
    ang = jnp.arange(seq, dtype=_F32)[:, None] * inv[None, :]
    cos, sin = jnp.cos(ang), jnp.sin(ang)
    cos_t = jnp.concatenate([cos, cos] * (LANES // D_HEAD), axis=1)
    sin_t = jnp.concatenate([-sin, sin] * (LANES // D_HEAD), axis=1)
    return cos_t, sin_t


def _head_mean_matrix():
    g = np.kron(np.eye(MXU_DIM // D_HEAD), np.full((D_HEAD, D_HEAD), 1.0 / D_HEAD))
    return jnp.asarray(g, dtype=_BF)


def kernel(x, ffn1_norm, ffn1_w_gate, ffn1_w_up, ffn1_w_down, mix_norm, w_in, a_q_norm, a_k_norm, a_lambda_q1, a_lambda_k1, a_lambda_q2, a_lambda_k2, a_subln, b_q_norm, b_k_norm, b_rel_bias, c_out_norm, w_branch_a, w_branch_b, w_branch_c, w_out, ffn2_norm, ffn2_w_gate, ffn2_w_up, ffn2_w_down):
    batch, seq, d = x.shape
    assert d == D_MODEL and seq % ATTN_BLOCK == 0 and seq % RET_BLOCK == 0
    assert seq % TOKEN_TILE == 0 and seq >= B_WINDOW and ATTN_BLOCK == TOKEN_TILE == RET_BLOCK
    scale = D_HEAD ** -0.5
    cos_t, sin_t = _rope_tables(seq)
    gmat = _head_mean_matrix()
    bf = lambda w: w.astype(_BF)
    f1g, f1u, f1d = bf(ffn1_w_gate), bf(ffn1_w_up), bf(ffn1_w_down)
    f2g, f2u, f2d = bf(ffn2_w_gate), bf(ffn2_w_up), bf(ffn2_w_down)
    win, wba, wbb, wbc, wo = bf(w_in), bf(w_branch_a), bf(w_branch_b), bf(w_branch_c), bf(w_out)
    xt = x.reshape(batch * seq, D_MODEL)
    for l in range(DEPTH):
        lam_init = 0.8 - 0.6 * math.exp(-0.3 * l)
        xt = _ffn(l, xt, ffn1_norm[l][None, :], f1g, f1u, f1d)

        heads = A_QK // D_HEAD
        qk_gains = jnp.stack([
            jnp.tile(a_q_norm[l].astype(_F32) * (scale * LOG2E), heads),
            jnp.tile(a_k_norm[l].astype(_F32), heads),
            jnp.tile(b_q_norm[l].astype(_F32) * (scale * LOG2E), heads),
            jnp.tile(b_k_norm[l].astype(_F32), heads),
        ])
        aq, ak, avt, bq, bk, bv, cq, ckt, cv, cg, gates = _in_proj(
            l, xt, mix_norm[l][None, :], win, cos_t, sin_t, qk_gains, gmat, seq)

        lamv = jnp.stack([a_lambda_q1[l], a_lambda_k1[l], a_lambda_q2[l], a_lambda_k2[l]]).astype(_F32)
        a_bound = _score_bound(qk_gains[0, :D_HEAD], qk_gains[1, :D_HEAD])
        ya = _diff_attn(a_bound, aq, ak, avt, lamv, a_subln[l].astype(_F32)[:, None], batch, seq, lam_init)
        bias_gen = _bias_generator(b_rel_bias[l])
        b_bound = _score_bound(qk_gains[2, :D_HEAD], qk_gains[3, :D_HEAD]) + jnp.max(jnp.abs(bias_gen))
        yb = _chunk_attn(b_bound, bq, bk, bv, bias_gen, batch, seq)
        yc = _retention(cq, ckt, cv, cg, c_out_norm[l].astype(_F32)[:, None, :], batch, seq)

        xt = _merge_ffn(l, xt, ya, yb, yc, gates, wba, wbb, wbc, wo, ffn2_norm[l][None, :],
                        f2g, f2u, f2d)
    return xt.reshape(batch, seq, D_MODEL)
```

```python
import functools
import math

import numpy as np
import jax
import jax.numpy as jnp
from jax import lax
from jax.experimental import pallas as pl
from jax.experimental.pallas import tpu as pltpu

D_MODEL = 1024
DEPTH = 2
CHUNK = 64
D_HEAD = 64
ROPE_THETA = 10000.0
EPS = 1e-6
A_HEADS = 4
A_DV = 2 * D_HEAD
B_HEADS = 8
B_LOOKBACK = 8
B_MAX_REL = 256
C_HEADS = 4
C_DV = 2 * D_HEAD
N_BRANCH = 3
D_FF = 2816

A_QK = A_HEADS * 2 * D_HEAD
A_V = A_HEADS * A_DV
B_QKV = B_HEADS * D_HEAD
C_QK = C_HEADS * D_HEAD
C_V = C_HEADS * C_DV
SPLITS = (A_QK, A_QK, A_V, B_QKV, B_QKV, B_QKV, C_QK, C_QK, C_V, C_V, N_BRANCH * D_MODEL)
IN_COLS = sum(SPLITS)
_OFFS = tuple(int(v) for v in np.cumsum((0,) + SPLITS))

LANES = 128
MXU_DIM = 256
VMEM_BYTES_V7X = 64 * 1024 * 1024
VMEM_HEADROOM = 6 * 1024 * 1024

TOKEN_TILE = 512
FF_CHUNK = MXU_DIM
PROJ_CHUNK = 512
ATTN_BLOCK = 512
B_QTILE = 256
B_WINDOW = B_QTILE + B_LOOKBACK * CHUNK
RET_BLOCK = 512
RET_UNROLL = 16
NEG_BIG = -1e30

_BF = jnp.bfloat16
_F32 = jnp.float32
_NT = (((1,), (1,)), ((), ()))


def _vmem_limit(estimate_bytes):
    return int(min(VMEM_BYTES_V7X - VMEM_HEADROOM, max(32 * 1024 * 1024, estimate_bytes)))


def _resident(shape, index_map):
    return pl.BlockSpec(shape, index_map, pipeline_mode=pl.Buffered(1))


def _layer_weight(layer, rows, cols):
    return pl.BlockSpec((None, rows, cols), lambda i: (layer, 0, 0), pipeline_mode=pl.Buffered(1))


def _rms_rows(x, gain):
    ms = jnp.mean(x * x, axis=-1, keepdims=True)
    return x * lax.rsqrt(ms + EPS) * gain


def _ffn_kernel(x_ref, gain_ref, wg_ref, wu_ref, wd_ref, o_ref, acc_ref):
    x = x_ref[...]
    h = _rms_rows(x, gain_ref[...]).astype(_BF)
    for c in range(D_FF // FF_CHUNK):
        sl = slice(c * FF_CHUNK, (c + 1) * FF_CHUNK)
        g = jnp.dot(h, wg_ref[:, sl], preferred_element_type=_F32)
        u = jnp.dot(h, wu_ref[:, sl], preferred_element_type=_F32)
        a = (g * jax.nn.sigmoid(g) * u).astype(_BF)
        part = jnp.dot(a, wd_ref[sl, :], preferred_element_type=_F32)
        if c == 0:
            acc_ref[...] = part
        else:
            acc_ref[...] += part
    o_ref[...] = x + 0.5 * acc_ref[...]


def _ffn(layer, x, gain, wg, wu, wd):
    t = x.shape[0]
    tm = TOKEN_TILE
    est = 3 * D_MODEL * D_FF * 2 + 5 * tm * D_MODEL * 4 + 4 * tm * FF_CHUNK * 4 + (8 << 20)
    return pl.pallas_call(
        _ffn_kernel,
        grid=(t // tm,),
        in_specs=[
            pl.BlockSpec((tm, D_MODEL), lambda i: (i, 0)),
            _resident((1, D_MODEL), lambda i: (0, 0)),
            _layer_weight(layer, D_MODEL, D_FF),
            _layer_weight(layer, D_MODEL, D_FF),
            _layer_weight(layer, D_FF, D_MODEL),
        ],
        out_specs=pl.BlockSpec((tm, D_MODEL), lambda i: (i, 0)),
        out_shape=jax.ShapeDtypeStruct((t, D_MODEL), _F32),
        scratch_shapes=[pltpu.VMEM((tm, D_MODEL), _F32)],
        compiler_params=pltpu.CompilerParams(
            dimension_semantics=("parallel",), vmem_limit_bytes=_vmem_limit(est)),
        name="ffn",
    )(x, gain, wg, wu, wd)


def _swap_halves(x):
    lane = lax.broadcasted_iota(jnp.int32, (1, LANES), 1)
    upper = (lane & (D_HEAD // 2)) != 0
    outs = []
    for s in range(x.shape[1] // LANES):
        xs = x[:, s * LANES:(s + 1) * LANES]
        from_below = pltpu.roll(xs, D_HEAD // 2, 1)
        from_above = pltpu.roll(xs, LANES - D_HEAD // 2, 1)
        outs.append(jnp.where(upper, from_below, from_above))
    return outs[0] if len(outs) == 1 else jnp.concatenate(outs, axis=1)


def _tile_lanes(t, n):
    reps = n // t.shape[1]
    return t if reps == 1 else jnp.concatenate([t] * reps, axis=1)


def _rope(x, cos, sin):
    n = x.shape[1]
    return x * _tile_lanes(cos, n) + _swap_halves(x) * _tile_lanes(sin, n)


def _head_ms(p, gmat):
    sq = (p * p).astype(_BF)
    outs = [jnp.dot(sq[:, s:s + MXU_DIM], gmat, preferred_element_type=_F32)
            for s in range(0, p.shape[1], MXU_DIM)]
    return outs[0] if len(outs) == 1 else jnp.concatenate(outs, axis=1)


def _in_proj_kernel(x_ref, gain_ref, w_ref, cos_ref, sin_ref, qkg_ref, gmat_ref,
                    aq_ref, ak_ref, avt_ref, bq_ref, bk_ref, bv_ref,
                    cq_ref, ckt_ref, cv_ref, cg_ref, gate_ref):
    h = _rms_rows(x_ref[...], gain_ref[...]).astype(_BF)
    cos = cos_ref[...]
    sin = sin_ref[...]
    gmat = gmat_ref[...]

    def proj(lo, width):
        return jnp.dot(h, w_ref[:, lo:lo + width], preferred_element_type=_F32)

    def normed(p, row):
        return p * lax.rsqrt(_head_ms(p, gmat) + EPS) * qkg_ref[row:row + 1, :]

    aq_ref[...] = _rope(normed(proj(_OFFS[0], A_QK), 0), cos, sin).astype(_BF)
    ak_ref[...] = _rope(normed(proj(_OFFS[1], A_QK), 1), cos, sin).astype(_BF)
    avt_ref[0] = proj(_OFFS[2], A_V).T.astype(_BF)
    bq_ref[...] = normed(proj(_OFFS[3], B_QKV), 2).astype(_BF)
    bk_ref[...] = normed(proj(_OFFS[4], B_QKV), 3).astype(_BF)
    bv_ref[...] = proj(_OFFS[5], B_QKV).astype(_BF)
    cq_ref[...] = _rope(proj(_OFFS[6], C_QK), cos, sin).astype(_BF)
    ckt_ref[0] = (_rope(proj(_OFFS[7], C_QK), cos, sin) * (D_HEAD ** -0.5)).T.astype(_BF)
    cv_ref[...] = proj(_OFFS[8], C_V).astype(_BF)
    cg_ref[...] = proj(_OFFS[9], C_V).astype(_BF)
    for c in range(N_BRANCH * D_MODEL // PROJ_CHUNK):
        lo = c * PROJ_CHUNK
        gate_ref[:, lo:lo + PROJ_CHUNK] = jax.nn.sigmoid(
            proj(_OFFS[10] + lo, PROJ_CHUNK)).astype(_BF)


def _in_proj(layer, x, gain, w, cos_t, sin_t, qk_gains, gmat, seq):
    t = x.shape[0]
    tm = TOKEN_TILE
    nt = t // tm
    pos_blocks = seq // tm
    row = lambda i: (i, 0)
    tok_major = lambda n: pl.BlockSpec((tm, n), row)
    out_shapes = (
        jax.ShapeDtypeStruct((t, A_QK), _BF),
        jax.ShapeDtypeStruct((t, A_QK), _BF),
        jax.ShapeDtypeStruct((nt, A_V, tm), _BF),
        jax.ShapeDtypeStruct((t, B_QKV), _BF),
        jax.ShapeDtypeStruct((t, B_QKV), _BF),
        jax.ShapeDtypeStruct((t, B_QKV), _BF),
        jax.ShapeDtypeStruct((t, C_QK), _BF),
        jax.ShapeDtypeStruct((nt, C_QK, tm), _BF),
        jax.ShapeDtypeStruct((t, C_V), _BF),
        jax.ShapeDtypeStruct((t, C_V), _BF),
        jax.ShapeDtypeStruct((t, N_BRANCH * D_MODEL), _BF),
    )
    out_specs = (
        tok_major(A_QK), tok_major(A_QK),
        pl.BlockSpec((1, A_V, tm), lambda i: (i, 0, 0)),
        tok_major(B_QKV), tok_major(B_QKV), tok_major(B_QKV),
        tok_major(C_QK),
        pl.BlockSpec((1, C_QK, tm), lambda i: (i, 0, 0)),
        tok_major(C_V), tok_major(C_V), tok_major(N_BRANCH * D_MODEL),
    )
    est = D_MODEL * IN_COLS * 2 + 2 * tm * IN_COLS * 2 + 4 * tm * D_MODEL * 4 + (12 << 20)
    return pl.pallas_call(
        _in_proj_kernel,
        grid=(nt,),
        in_specs=[
            pl.BlockSpec((tm, D_MODEL), row),
            _resident((1, D_MODEL), lambda i: (0, 0)),
            _layer_weight(layer, D_MODEL, IN_COLS),
            pl.BlockSpec((tm, LANES), lambda i: (i % pos_blocks, 0)),
            pl.BlockSpec((tm, LANES), lambda i: (i % pos_blocks, 0)),
            _resident((4, A_QK), lambda i: (0, 0)),
            _resident((MXU_DIM, MXU_DIM), lambda i: (0, 0)),
        ],
        out_specs=out_specs,
        out_shape=out_shapes,
        compiler_params=pltpu.CompilerParams(
            dimension_semantics=("parallel",), vmem_limit_bytes=_vmem_limit(est)),
        name="in_proj",
    )(x, gain, w, cos_t, sin_t, qk_gains, gmat)


LOG2E = math.log2(math.e)
SAFE_LOG2_SCORE = 60.0
NORM_MARGIN = 1.05
A_PAIR = 4


def _diff_attn_kernel(bound_ref, q_ref, k_ref, vt_ref, lamv_ref, subln_ref, o_ref, acc_ref, *,
                      lam_init):
    blk = ATTN_BLOCK
    i = pl.program_id(2)
    n_chain = 2 * A_PAIR
    per_group = MXU_DIM // D_HEAD
    lane = lax.broadcasted_iota(jnp.int32, (1, MXU_DIM), 1)
    qs = []
    for n in range(n_chain):
        g, r = divmod(n, per_group)
        qg = q_ref[:, g * MXU_DIM:(g + 1) * MXU_DIM]
        qs.append(jnp.where((lane >= D_HEAD * r) & (lane < D_HEAD * (r + 1)), qg,
                            jnp.zeros_like(qg)))

    def scores(j, n, diagonal):
        g = n // per_group
        kb = k_ref[pl.ds(pl.multiple_of(j * blk, blk), blk), g * MXU_DIM:(g + 1) * MXU_DIM]
        st = lax.dot_general(kb, qs[n], _NT, preferred_element_type=_F32)
        if diagonal:
            key_chunk = lax.broadcasted_iota(jnp.int32, (blk, blk), 0) // CHUNK
            qry_chunk = lax.broadcasted_iota(jnp.int32, (blk, blk), 1) // CHUNK
            st = jnp.where(key_chunk <= qry_chunk, st, NEG_BIG)
        return st

    def values_t(j, n):
        head = n // 2
        return vt_ref[j, head * A_DV:(head + 1) * A_DV, :]

    def plain_step(j, carry, diagonal):
        out = []
        sts = [scores(j, n, diagonal) for n in range(min(2, n_chain))]
        for n in range(n_chain):
            if n + 2 < n_chain:
                sts.append(scores(j, n + 2, diagonal))
            p = jnp.exp2(sts[n])
            acc_ref[n] += jnp.dot(values_t(j, n), p.astype(_BF), preferred_element_type=_F32)
            out.append(carry[n] + jnp.sum(p, axis=0, keepdims=True))
        return tuple(out)

    def shifted_step(j, carry, diagonal):
        out = []
        for n in range(n_chain):
            m_old, l_old = carry[n]
            st = scores(j, n, diagonal)
            m_new = jnp.maximum(m_old, jnp.max(st, axis=0, keepdims=True))
            alpha = jnp.exp2(m_old - m_new)
            p = jnp.exp2(st - m_new)
            l_new = alpha * l_old + jnp.sum(p, axis=0, keepdims=True)
            pv = jnp.dot(values_t(j, n), p.astype(_BF), preferred_element_type=_F32)
            acc_ref[n] = alpha * acc_ref[n] + pv
            out.append((m_new, l_new))
        return tuple(out)

    def finish(ls):
        lamv = lamv_ref[...]
        lam = (jnp.exp(jnp.sum(lamv[0:1] * lamv[1:2], axis=-1, keepdims=True))
               - jnp.exp(jnp.sum(lamv[2:3] * lamv[3:4], axis=-1, keepdims=True)) + lam_init)
        for head in range(A_PAIR):
            o = (acc_ref[2 * head] / ls[2 * head]
                 - lam * (acc_ref[2 * head + 1] / ls[2 * head + 1]))
            ms = jnp.mean(o * o, axis=0, keepdims=True)
            y = o * lax.rsqrt(ms + EPS) * subln_ref[...] * (1.0 - lam_init)
            o_ref[:, head * A_DV:(head + 1) * A_DV] = y.T.astype(_BF)

    acc_ref[...] = jnp.zeros_like(acc_ref)
    bounded = bound_ref[0] <= SAFE_LOG2_SCORE

    @pl.when(bounded)
    def _():
        init = tuple(jnp.zeros((1, blk), _F32) for _ in range(n_chain))
        carry = lax.fori_loop(0, i, lambda j, c: plain_step(j, c, False), init)
        finish(plain_step(i, carry, True))

    @pl.when(jnp.logical_not(bounded))
    def _():
        init = tuple((jnp.full((1, blk), NEG_BIG, _F32), jnp.zeros((1, blk), _F32))
                     for _ in range(n_chain))
        carry = lax.fori_loop(0, i, lambda j, c: shifted_step(j, c, False), init)
        finish([l for _, l in shifted_step(i, carry, True)])


def _diff_attn(score_bound, aq, ak, avt, lamv, subln_col, batch, seq, lam_init):
    blk = ATTN_BLOCK
    nq = seq // blk
    t = batch * seq
    width = A_PAIR * A_DV
    est = 2 * (seq * width * 2) * 2 + 2 * A_PAIR * (A_DV * blk * 4) + 16 * blk * blk * 4 + (8 << 20)
    return pl.pallas_call(
        functools.partial(_diff_attn_kernel, lam_init=lam_init),
        grid=(batch, A_HEADS // A_PAIR, nq),
        in_specs=[
            pl.BlockSpec(memory_space=pltpu.SMEM),
            pl.BlockSpec((blk, width), lambda b, h, i: (b * nq + i, h)),
            pl.BlockSpec((seq, width), lambda b, h, i: (b, h)),
            pl.BlockSpec((nq, width, blk), lambda b, h, i: (b, h, 0)),
            pl.BlockSpec((4, D_HEAD), lambda b, h, i: (0, 0)),
            pl.BlockSpec((A_DV, 1), lambda b, h, i: (0, 0)),
        ],
        out_specs=pl.BlockSpec((blk, width), lambda b, h, i: (b * nq + i, h)),
        out_shape=jax.ShapeDtypeStruct((t, A_V), _BF),
        scratch_shapes=[pltpu.VMEM((2 * A_PAIR, A_DV, blk), _F32)],
        compiler_params=pltpu.CompilerParams(
            dimension_semantics=("parallel", "parallel", "parallel"),
            vmem_limit_bytes=_vmem_limit(est)),
        name="diff_attn",
    )(score_bound, aq, ak, avt, lamv, subln_col)


def _score_bound(q_gain, k_gain):
    return (D_HEAD * NORM_MARGIN * jnp.max(jnp.abs(q_gain)) * jnp.max(jnp.abs(k_gain))).reshape(1)


B_TABLES = (B_WINDOW - B_QTILE) // B_QTILE + 1
B_GEN_LEN = 2048
B_GROUP = MXU_DIM // D_HEAD


def _build_bias_tables(gen_ref, mb_ref):
    qchunk = lax.broadcasted_iota(jnp.int32, (B_QTILE, B_WINDOW), 0) // CHUNK
    kchunk = lax.broadcasted_iota(jnp.int32, (B_QTILE, B_WINDOW), 1) // CHUNK
    for hh in range(B_GROUP):
        rolled = pltpu.roll(jnp.broadcast_to(gen_ref[hh], (B_QTILE, B_GEN_LEN)), 0, 1,
                            stride=1, stride_axis=0)
        for t in range(B_TABLES):
            lo = B_WINDOW - t * B_QTILE
            dchunk = qchunk - kchunk + (t * B_QTILE) // CHUNK
            visible = (dchunk >= 0) & (dchunk <= B_LOOKBACK)
            mb_ref[hh, t] = jnp.where(visible, rolled[:, lo:lo + B_WINDOW], NEG_BIG)


def _chunk_attn_kernel(bound_ref, q_ref, k_ref, v_ref, gen_ref, o_ref, mb_ref):
    i = pl.program_id(2)

    @pl.when(i == 0)
    def _():
        _build_bias_tables(gen_ref, mb_ref)

    start = pl.multiple_of(jnp.maximum(i * B_QTILE - (B_WINDOW - B_QTILE), 0), B_QTILE)
    table = jnp.minimum(i, B_TABLES - 1)
    lane = lax.broadcasted_iota(jnp.int32, (1, MXU_DIM), 1)
    q = q_ref[...]
    zero = jnp.zeros_like(q)
    kw = k_ref[pl.ds(start, B_WINDOW), :]
    vw = v_ref[pl.ds(start, B_WINDOW), :]
    in_head = [(lane >= D_HEAD * hh) & (lane < D_HEAD * (hh + 1)) for hh in range(B_GROUP)]

    def logits(hh):
        s = lax.dot_general(jnp.where(in_head[hh], q, zero), kw, _NT, preferred_element_type=_F32)
        return s + mb_ref[hh, table]

    def attend(shifted):
        out = None
        s = logits(0)
        for hh in range(B_GROUP):
            s_next = logits(hh + 1) if hh + 1 < B_GROUP else None
            if shifted:
                s = s - jnp.max(s, axis=-1, keepdims=True)
            p = jnp.exp2(s)
            l = jnp.sum(p, axis=-1, keepdims=True)
            o = jnp.dot(p.astype(_BF), vw, preferred_element_type=_F32) / l
            out = o if out is None else jnp.where(in_head[hh], o, out)
            s = s_next
        o_ref[...] = out.astype(_BF)

    bounded = bound_ref[0] <= SAFE_LOG2_SCORE
    pl.when(bounded)(lambda: attend(False))
    pl.when(jnp.logical_not(bounded))(lambda: attend(True))


def _chunk_attn(score_bound, bq, bk, bv, bias_gen, batch, seq):
    nq = seq // B_QTILE
    t = batch * seq
    est = B_GROUP * B_TABLES * B_QTILE * B_WINDOW * 4 + 4 * B_QTILE * B_GEN_LEN * 4 \
        + 4 * seq * MXU_DIM * 2 + (12 << 20)
    return pl.pallas_call(
        _chunk_attn_kernel,
        grid=(batch, B_HEADS // B_GROUP, nq),
        in_specs=[
            pl.BlockSpec(memory_space=pltpu.SMEM),
            pl.BlockSpec((B_QTILE, MXU_DIM), lambda b, h, i: (b * nq + i, h)),
            pl.BlockSpec((seq, MXU_DIM), lambda b, h, i: (b, h)),
            pl.BlockSpec((seq, MXU_DIM), lambda b, h, i: (b, h)),
            pl.BlockSpec((B_GROUP, 1, B_GEN_LEN), lambda b, h, i: (h, 0, 0)),
        ],
        out_specs=pl.BlockSpec((B_QTILE, MXU_DIM), lambda b, h, i: (b * nq + i, h)),
        out_shape=jax.ShapeDtypeStruct((t, B_QKV), _BF),
        scratch_shapes=[pltpu.VMEM((B_GROUP, B_TABLES, B_QTILE, B_WINDOW), _F32)],
        compiler_params=pltpu.CompilerParams(
            dimension_semantics=("parallel", "parallel", "arbitrary"),
            vmem_limit_bytes=_vmem_limit(est)),
        name="chunk_attn",
    )(score_bound, bq, bk, bv, bias_gen)


def _bias_generator(rel_bias):
    b = rel_bias.astype(_F32) * LOG2E
    n_far = B_WINDOW - B_MAX_REL + 1
    mid = b[:, 1:CHUNK - 1 + B_MAX_REL][:, ::-1]
    n_neg = B_GEN_LEN - n_far - mid.shape[1]
    heads = b.shape[0]
    gen = jnp.concatenate([
        jnp.broadcast_to(b[:, -1:], (heads, n_far)), mid,
        jnp.broadcast_to(b[:, :1], (heads, n_neg))], axis=1)
    return gen[:, None, :]


def _retention_kernel(q_ref, kt_ref, v_ref, g_ref, cn_ref, o_ref, decay_ref, *, seq):
    blk = RET_BLOCK
    h = pl.program_id(1)
    hf = jnp.full((1, 1), h, jnp.int32).astype(_F32)
    log_gamma = jnp.log(1.0 - jnp.exp2(-5.0 - hf))
    diff = (lax.broadcasted_iota(jnp.int32, (blk, blk), 0)
            - lax.broadcasted_iota(jnp.int32, (blk, blk), 1)).astype(_F32)
    decay_ref[...] = jnp.where(diff >= 0, jnp.exp(log_gamma * jnp.maximum(diff, 0.0)), 0.0)
    pos = lax.broadcasted_iota(jnp.int32, (blk, LANES), 0).astype(_F32)
    xi = jnp.exp(log_gamma * (pos + 1.0))
    zeta = jnp.exp(log_gamma * (blk - 1.0 - pos))
    block_decay = jnp.exp(log_gamma * float(blk))
    odd = (h % 2) == 1
    q_keep = (lax.broadcasted_iota(jnp.int32, (1, LANES), 1) >= D_HEAD) == odd
    k_keep = (lax.broadcasted_iota(jnp.int32, (LANES, 1), 0) >= D_HEAD) == odd
    gain = cn_ref[0]

    def body(j, state):
        rows = pl.ds(pl.multiple_of(j * blk, blk), blk)
        qb = q_ref[rows, :]
        qb = jnp.where(q_keep, qb, jnp.zeros_like(qb))
        kt = kt_ref[j]
        kt = jnp.where(k_keep, kt, jnp.zeros_like(kt))
        vb = v_ref[rows, :]
        sc = jnp.dot(qb, kt, preferred_element_type=_F32) * decay_ref[...]
        inner = jnp.dot(sc.astype(_BF), vb, preferred_element_type=_F32)
        cross = jnp.dot(qb, state.astype(_BF), preferred_element_type=_F32) * xi
        o = inner + cross
        vz = (vb.astype(_F32) * zeta).astype(_BF)
        new_state = state * block_decay + jnp.dot(kt, vz, preferred_element_type=_F32)
        ms = jnp.mean(o * o, axis=-1, keepdims=True)
        y = o * lax.rsqrt(ms + EPS) * gain
        gg = g_ref[rows, :].astype(_F32)
        o_ref[rows, :] = (gg * jax.nn.sigmoid(gg) * y).astype(_BF)
        return new_state

    lax.fori_loop(0, seq // blk, body, jnp.zeros((LANES, C_DV), _F32), unroll=RET_UNROLL)


def _retention(cq, ckt, cv, cg, cnorm, batch, seq):
    blk = RET_BLOCK
    nb = seq // blk
    t = batch * seq
    est = 2 * 5 * seq * LANES * 2 + 6 * blk * blk * 4 + (8 << 20)
    return pl.pallas_call(
        functools.partial(_retention_kernel, seq=seq),
        grid=(batch, C_HEADS),
        in_specs=[
            pl.BlockSpec((seq, LANES), lambda b, h: (b, h // 2)),
            pl.BlockSpec((nb, LANES, blk), lambda b, h: (b, h // 2, 0)),
            pl.BlockSpec((seq, C_DV), lambda b, h: (b, h)),
            pl.BlockSpec((seq, C_DV), lambda b, h: (b, h)),
            pl.BlockSpec((1, 1, C_DV), lambda b, h: (h, 0, 0)),
        ],
        out_specs=pl.BlockSpec((seq, C_DV), lambda b, h: (b, h)),
        out_shape=jax.ShapeDtypeStruct((t, C_V), _BF),
        scratch_shapes=[pltpu.VMEM((blk, blk), _F32)],
        compiler_params=pltpu.CompilerParams(
            dimension_semantics=("parallel", "parallel"), vmem_limit_bytes=_vmem_limit(est)),
        name="retention",
    )(cq, ckt, cv, cg, cnorm)


def _merge_ffn_kernel(x_ref, ya_ref, yb_ref, yc_ref, gate_ref, wa_ref, wb_ref, wc_ref, wo_ref,
                      gain_ref, wg_ref, wu_ref, wd_ref, o_ref, acc_ref):
    merged = None
    for n, (y_ref, w_ref) in enumerate(((ya_ref, wa_ref), (yb_ref, wb_ref), (yc_ref, wc_ref))):
        br = jnp.dot(y_ref[...], w_ref[...], preferred_element_type=_F32)
        term = gate_ref[:, n * D_MODEL:(n + 1) * D_MODEL].astype(_F32) * br
        merged = term if merged is None else merged + term
    x = x_ref[...] + jnp.dot(merged.astype(_BF), wo_ref[...], preferred_element_type=_F32)
    h = _rms_rows(x, gain_ref[...]).astype(_BF)
    for c in range(D_FF // FF_CHUNK):
        sl = slice(c * FF_CHUNK, (c + 1) * FF_CHUNK)
        g = jnp.dot(h, wg_ref[:, sl], preferred_element_type=_F32)
        u = jnp.dot(h, wu_ref[:, sl], preferred_element_type=_F32)
        a = (g * jax.nn.sigmoid(g) * u).astype(_BF)
        part = jnp.dot(a, wd_ref[sl, :], preferred_element_type=_F32)
        if c == 0:
            acc_ref[...] = part
        else:
            acc_ref[...] += part
    o_ref[...] = x + 0.5 * acc_ref[...]


def _merge_ffn(layer, x, ya, yb, yc, gates, wa, wb, wc, wo, gain, wg, wu, wd):
    t = x.shape[0]
    tm = TOKEN_TILE
    row = lambda i: (i, 0)
    const = lambda i: (0, 0)
    est = (5 * D_MODEL * D_MODEL + 3 * D_MODEL * D_FF) * 2 \
        + 2 * tm * (2 * D_MODEL * 4 + 3 * 512 * 2 + 3 * D_MODEL * 2) \
        + 8 * tm * D_MODEL * 4 + 4 * tm * FF_CHUNK * 4 + (8 << 20)
    return pl.pallas_call(
        _merge_ffn_kernel,
        grid=(t // tm,),
        in_specs=[
            pl.BlockSpec((tm, D_MODEL), row),
            pl.BlockSpec((tm, A_V), row),
            pl.BlockSpec((tm, B_QKV), row),
            pl.BlockSpec((tm, C_V), row),
            pl.BlockSpec((tm, N_BRANCH * D_MODEL), row),
            _layer_weight(layer, A_V, D_MODEL),
            _layer_weight(layer, B_QKV, D_MODEL),
            _layer_weight(layer, C_V, D_MODEL),
            _layer_weight(layer, D_MODEL, D_MODEL),
            _resident((1, D_MODEL), const),
            _layer_weight(layer, D_MODEL, D_FF),
            _layer_weight(layer, D_MODEL, D_FF),
            _layer_weight(layer, D_FF, D_MODEL),
        ],
        out_specs=pl.BlockSpec((tm, D_MODEL), row),
        out_shape=jax.ShapeDtypeStruct((t, D_MODEL), _F32),
        scratch_shapes=[pltpu.VMEM((tm, D_MODEL), _F32)],
        compiler_params=pltpu.CompilerParams(
            dimension_semantics=("parallel",), vmem_limit_bytes=_vmem_limit(est)),
        name="merge_ffn",
    )(x, ya, yb, yc, gates, wa, wb, wc, wo, gain, wg, wu, wd)


def _rope_tables(seq):
    half = D_HEAD // 2
    inv = ROPE_THETA ** (-jnp.arange(half, dtype=_F32) / half)
    ang = jnp.arange(seq, dtype=_F32)[:, None] * inv[None, :]
    cos, sin = jnp.cos(ang), jnp.sin(ang)
    cos_t = jnp.concatenate([cos, cos] * (LANES // D_HEAD), axis=1)
    sin_t = jnp.concatenate([-sin, sin] * (LANES // D_HEAD), axis=1)
    return cos_t, sin_t


def _head_mean_matrix():
    g = np.kron(np.eye(MXU_DIM // D_HEAD), np.full((D_HEAD, D_HEAD), 1.0 / D_HEAD))
    return jnp.asarray(g, dtype=_BF)


def kernel(x, ffn1_norm, ffn1_w_gate, ffn1_w_up, ffn1_w_down, mix_norm, w_in, a_q_norm, a_k_norm, a_lambda_q1, a_lambda_k1, a_lambda_q2, a_lambda_k2, a_subln, b_q_norm, b_k_norm, b_rel_bias, c_out_norm, w_branch_a, w_branch_b, w_branch_c, w_out, ffn2_norm, ffn2_w_gate, ffn2_w_up, ffn2_w_down):
    batch, seq, d = x.shape
    assert d == D_MODEL and seq % ATTN_BLOCK == 0 and seq % RET_BLOCK == 0
    assert seq % TOKEN_TILE == 0 and seq >= B_WINDOW and ATTN_BLOCK == TOKEN_TILE == RET_BLOCK
    scale = D_HEAD ** -0.5
    cos_t, sin_t = _rope_tables(seq)
    gmat = _head_mean_matrix()
    bf = lambda w: w.astype(_BF)
    f1g, f1u, f1d = bf(ffn1_w_gate), bf(ffn1_w_up), bf(ffn1_w_down)
    f2g, f2u, f2d = bf(ffn2_w_gate), bf(ffn2_w_up), bf(ffn2_w_down)
    win, wba, wbb, wbc, wo = bf(w_in), bf(w_branch_a), bf(w_branch_b), bf(w_branch_c), bf(w_out)
    xt = x.reshape(batch * seq, D_MODEL)
    for l in range(DEPTH):
        lam_init = 0.8 - 0.6 * math.exp(-0.3 * l)
        xt = _ffn(l, xt, ffn1_norm[l][None, :], f1g, f1u, f1d)

        heads = A_QK // D_HEAD
        qk_gains = jnp.stack([
            jnp.tile(a_q_norm[l].astype(_F32) * (scale * LOG2E), heads),
            jnp.tile(a_k_norm[l].astype(_F32), heads),
            jnp.tile(b_q_norm[l].astype(_F32) * (scale * LOG2E), heads),
            jnp.tile(b_k_norm[l].astype(_F32), heads),
        ])
        aq, ak, avt, bq, bk, bv, cq, ckt, cv, cg, gates = _in_proj(
            l, xt, mix_norm[l][None, :], win, cos_t, sin_t, qk_gains, gmat, seq)

        lamv = jnp.stack([a_lambda_q1[l], a_lambda_k1[l], a_lambda_q2[l], a_lambda_k2[l]]).astype(_F32)
        a_bound = _score_bound(qk_gains[0, :D_HEAD], qk_gains[1, :D_HEAD])
        ya = _diff_attn(a_bound, aq, ak, avt, lamv, a_subln[l].astype(_F32)[:, None], batch, seq, lam_init)
        bias_gen = _bias_generator(b_rel_bias[l])
        b_bound = _score_bound(qk_gains[2, :D_HEAD], qk_gains[3, :D_HEAD]) + jnp.max(jnp.abs(bias_gen))
        yb = _chunk_attn(b_bound, bq, bk, bv, bias_gen, batch, seq)
        yc = _retention(cq, ckt, cv, cg, c_out_norm[l].astype(_F32)[:, None, :], batch, seq)

        xt = _merge_ffn(l, xt, ya, yb, yc, gates, wba, wbb, wbc, wo, ffn2_norm[l][None, :],
                        f2g, f2u, f2d)
    return xt.reshape(batch, seq, D_MODEL)
```

```python
import functools
import math

import numpy as np
import jax
import jax.numpy as jnp
from jax import lax
from jax.experimental import pallas as pl
from jax.experimental.pallas import tpu as pltpu

D_MODEL = 1024
DEPTH = 2
CHUNK = 64
D_HEAD = 64
ROPE_THETA = 10000.0
EPS = 1e-6
A_HEADS = 4
A_DV = 2 * D_HEAD
B_HEADS = 8
B_LOOKBACK = 8
B_MAX_REL = 256
C_HEADS = 4
C_DV = 2 * D_HEAD
N_BRANCH = 3
D_FF = 2816

A_QK = A_HEADS * 2 * D_HEAD
A_V = A_HEADS * A_DV
B_QKV = B_HEADS * D_HEAD
C_QK = C_HEADS * D_HEAD
C_V = C_HEADS * C_DV
SPLITS = (A_QK, A_QK, A_V, B_QKV, B_QKV, B_QKV, C_QK, C_QK, C_V, C_V, N_BRANCH * D_MODEL)
IN_COLS = sum(SPLITS)
_OFFS = tuple(int(v) for v in np.cumsum((0,) + SPLITS))

LANES = 128
MXU_DIM = 256
VMEM_BYTES_V7X = 64 * 1024 * 1024
VMEM_HEADROOM = 6 * 1024 * 1024
VMEM_TEMPORARIES = 12 * 1024 * 1024

TOKEN_TILE = 512
FF_CHUNK = MXU_DIM
PROJ_CHUNK = 512
ATTN_BLOCK = 512
B_QTILE = 256
B_WINDOW = B_QTILE + B_LOOKBACK * CHUNK
RET_BLOCK = 512
RET_UNROLL = 16
NEG_BIG = -1e30

_BF = jnp.bfloat16
_F32 = jnp.float32
_NT = (((1,), (1,)), ((), ()))


def _vmem_limit(estimate_bytes):
    return int(min(VMEM_BYTES_V7X - VMEM_HEADROOM, max(32 * 1024 * 1024, estimate_bytes)))


def _resident(shape, index_map):
    return pl.BlockSpec(shape, index_map, pipeline_mode=pl.Buffered(1))


def _layer_weight(layer, rows, cols):
    return pl.BlockSpec((None, rows, cols), lambda i: (layer, 0, 0), pipeline_mode=pl.Buffered(1))


def _rms_rows(x, gain):
    ms = jnp.mean(x * x, axis=-1, keepdims=True)
    return x * lax.rsqrt(ms + EPS) * gain


def _ffn_kernel(x_ref, gain_ref, wg_ref, wu_ref, wd_ref, o_ref, acc_ref):
    x = x_ref[...]
    h = _rms_rows(x, gain_ref[...]).astype(_BF)
    for c in range(D_FF // FF_CHUNK):
        sl = slice(c * FF_CHUNK, (c + 1) * FF_CHUNK)
        g = jnp.dot(h, wg_ref[:, sl], preferred_element_type=_F32)
        u = jnp.dot(h, wu_ref[:, sl], preferred_element_type=_F32)
        a = (g * jax.nn.sigmoid(g) * u).astype(_BF)
        part = jnp.dot(a, wd_ref[sl, :], preferred_element_type=_F32)
        if c == 0:
            acc_ref[...] = part
        else:
            acc_ref[...] += part
    o_ref[...] = x + 0.5 * acc_ref[...]


def _ffn(layer, x, gain, wg, wu, wd):
    t = x.shape[0]
    tm = TOKEN_TILE
    est = 3 * D_MODEL * D_FF * 2 + 5 * tm * D_MODEL * 4 + 4 * tm * FF_CHUNK * 4 + VMEM_TEMPORARIES
    return pl.pallas_call(
        _ffn_kernel,
        grid=(t // tm,),
        in_specs=[
            pl.BlockSpec((tm, D_MODEL), lambda i: (i, 0)),
            _resident((1, D_MODEL), lambda i: (0, 0)),
            _layer_weight(layer, D_MODEL, D_FF),
            _layer_weight(layer, D_MODEL, D_FF),
            _layer_weight(layer, D_FF, D_MODEL),
        ],
        out_specs=pl.BlockSpec((tm, D_MODEL), lambda i: (i, 0)),
        out_shape=jax.ShapeDtypeStruct((t, D_MODEL), _F32),
        scratch_shapes=[pltpu.VMEM((tm, D_MODEL), _F32)],
        compiler_params=pltpu.CompilerParams(
            dimension_semantics=("parallel",), vmem_limit_bytes=_vmem_limit(est)),
        name="ffn",
    )(x, gain, wg, wu, wd)


def _swap_halves(x):
    lane = lax.broadcasted_iota(jnp.int32, (1, LANES), 1)
    upper = (lane & (D_HEAD // 2)) != 0
    outs = []
    for s in range(x.shape[1] // LANES):
        xs = x[:, s * LANES:(s + 1) * LANES]
        from_below = pltpu.roll(xs, D_HEAD // 2, 1)
        from_above = pltpu.roll(xs, LANES - D_HEAD // 2, 1)
        outs.append(jnp.where(upper, from_below, from_above))
    return outs[0] if len(outs) == 1 else jnp.concatenate(outs, axis=1)


def _tile_lanes(t, n):
    reps = n // t.shape[1]
    return t if reps == 1 else jnp.concatenate([t] * reps, axis=1)


def _rope(x, cos, sin):
    n = x.shape[1]
    return x * _tile_lanes(cos, n) + _swap_halves(x) * _tile_lanes(sin, n)


def _head_ms(p, gmat):
    sq = (p * p).astype(_BF)
    outs = [jnp.dot(sq[:, s:s + MXU_DIM], gmat, preferred_element_type=_F32)
            for s in range(0, p.shape[1], MXU_DIM)]
    return outs[0] if len(outs) == 1 else jnp.concatenate(outs, axis=1)


def _in_proj_kernel(x_ref, gain_ref, w_ref, cos_ref, sin_ref, qkg_ref, gmat_ref,
                    aq_ref, ak_ref, avt_ref, bq_ref, bk_ref, bv_ref,
                    cq_ref, ckt_ref, cv_ref, cg_ref, gate_ref):
    h = _rms_rows(x_ref[...], gain_ref[...]).astype(_BF)
    cos = cos_ref[...]
    sin = sin_ref[...]
    gmat = gmat_ref[...]

    def proj(lo, width):
        return jnp.dot(h, w_ref[:, lo:lo + width], preferred_element_type=_F32)

    def normed(p, row):
        return p * lax.rsqrt(_head_ms(p, gmat) + EPS) * qkg_ref[row:row + 1, :]

    aq_ref[...] = _rope(normed(proj(_OFFS[0], A_QK), 0), cos, sin).astype(_BF)
    ak_ref[...] = _rope(normed(proj(_OFFS[1], A_QK), 1), cos, sin).astype(_BF)
    avt_ref[0] = proj(_OFFS[2], A_V).T.astype(_BF)
    bq_ref[...] = normed(proj(_OFFS[3], B_QKV), 2).astype(_BF)
    bk_ref[...] = normed(proj(_OFFS[4], B_QKV), 3).astype(_BF)
    bv_ref[...] = proj(_OFFS[5], B_QKV).astype(_BF)
    cq_ref[...] = _rope(proj(_OFFS[6], C_QK), cos, sin).astype(_BF)
    ckt_ref[0] = (_rope(proj(_OFFS[7], C_QK), cos, sin) * (D_HEAD ** -0.5)).T.astype(_BF)
    cv_ref[...] = proj(_OFFS[8], C_V).astype(_BF)
    cg_ref[...] = proj(_OFFS[9], C_V).astype(_BF)
    for c in range(N_BRANCH * D_MODEL // PROJ_CHUNK):
        lo = c * PROJ_CHUNK
        gate_ref[:, lo:lo + PROJ_CHUNK] = jax.nn.sigmoid(
            proj(_OFFS[10] + lo, PROJ_CHUNK)).astype(_BF)


def _in_proj(layer, x, gain, w, cos_t, sin_t, qk_gains, gmat, seq):
    t = x.shape[0]
    tm = TOKEN_TILE
    nt = t // tm
    pos_blocks = seq // tm
    row = lambda i: (i, 0)
    tok_major = lambda n: pl.BlockSpec((tm, n), row)
    out_shapes = (
        jax.ShapeDtypeStruct((t, A_QK), _BF),
        jax.ShapeDtypeStruct((t, A_QK), _BF),
        jax.ShapeDtypeStruct((nt, A_V, tm), _BF),
        jax.ShapeDtypeStruct((t, B_QKV), _BF),
        jax.ShapeDtypeStruct((t, B_QKV), _BF),
        jax.ShapeDtypeStruct((t, B_QKV), _BF),
        jax.ShapeDtypeStruct((t, C_QK), _BF),
        jax.ShapeDtypeStruct((nt, C_QK, tm), _BF),
        jax.ShapeDtypeStruct((t, C_V), _BF),
        jax.ShapeDtypeStruct((t, C_V), _BF),
        jax.ShapeDtypeStruct((t, N_BRANCH * D_MODEL), _BF),
    )
    out_specs = (
        tok_major(A_QK), tok_major(A_QK),
        pl.BlockSpec((1, A_V, tm), lambda i: (i, 0, 0)),
        tok_major(B_QKV), tok_major(B_QKV), tok_major(B_QKV),
        tok_major(C_QK),
        pl.BlockSpec((1, C_QK, tm), lambda i: (i, 0, 0)),
        tok_major(C_V), tok_major(C_V), tok_major(N_BRANCH * D_MODEL),
    )
    est = D_MODEL * IN_COLS * 2 + 2 * tm * IN_COLS * 2 + 4 * tm * D_MODEL * 4 + VMEM_TEMPORARIES
    return pl.pallas_call(
        _in_proj_kernel,
        grid=(nt,),
        in_specs=[
            pl.BlockSpec((tm, D_MODEL), row),
            _resident((1, D_MODEL), lambda i: (0, 0)),
            _layer_weight(layer, D_MODEL, IN_COLS),
            pl.BlockSpec((tm, LANES), lambda i: (i % pos_blocks, 0)),
            pl.BlockSpec((tm, LANES), lambda i: (i % pos_blocks, 0)),
            _resident((4, A_QK), lambda i: (0, 0)),
            _resident((MXU_DIM, MXU_DIM), lambda i: (0, 0)),
        ],
        out_specs=out_specs,
        out_shape=out_shapes,
        compiler_params=pltpu.CompilerParams(
            dimension_semantics=("parallel",), vmem_limit_bytes=_vmem_limit(est)),
        name="in_proj",
    )(x, gain, w, cos_t, sin_t, qk_gains, gmat)


LOG2E = math.log2(math.e)
SAFE_LOG2_SCORE = 60.0
NORM_MARGIN = 1.05
A_PAIR = 4


def _diff_attn_kernel(bound_ref, q_ref, k_ref, vt_ref, lamv_ref, subln_ref, o_ref, acc_ref, *,
                      lam_init):
    blk = ATTN_BLOCK
    i = pl.program_id(2)
    n_chain = 2 * A_PAIR
    per_group = MXU_DIM // D_HEAD
    lane = lax.broadcasted_iota(jnp.int32, (1, MXU_DIM), 1)
    qs = []
    for n in range(n_chain):
        g, r = divmod(n, per_group)
        qg = q_ref[:, g * MXU_DIM:(g + 1) * MXU_DIM]
        qs.append(jnp.where((lane >= D_HEAD * r) & (lane < D_HEAD * (r + 1)), qg,
                            jnp.zeros_like(qg)))

    def scores(j, n, diagonal):
        g = n // per_group
        kb = k_ref[pl.ds(pl.multiple_of(j * blk, blk), blk), g * MXU_DIM:(g + 1) * MXU_DIM]
        st = lax.dot_general(kb, qs[n], _NT, preferred_element_type=_F32)
        if diagonal:
            key_chunk = lax.broadcasted_iota(jnp.int32, (blk, blk), 0) // CHUNK
            qry_chunk = lax.broadcasted_iota(jnp.int32, (blk, blk), 1) // CHUNK
            st = jnp.where(key_chunk <= qry_chunk, st, NEG_BIG)
        return st

    def values_t(j, n):
        head = n // 2
        return vt_ref[j, head * A_DV:(head + 1) * A_DV, :]

    def plain_step(j, carry, diagonal):
        out = [None] * n_chain
        groups = [list(range(g * per_group, (g + 1) * per_group))
                  for g in range(n_chain // per_group)]
        sts = {n: scores(j, n, diagonal) for n in groups[0]}
        for gi, chains in enumerate(groups):
            if gi + 1 < len(groups):
                sts.update({n: scores(j, n, diagonal) for n in groups[gi + 1]})
            for n in chains:
                p = jnp.exp2(sts[n])
                out[n] = carry[n] + jnp.sum(p, axis=0, keepdims=True)
                acc_ref[n] += jnp.dot(values_t(j, n), p.astype(_BF), preferred_element_type=_F32)
        return tuple(out)

    def shifted_step(j, carry, diagonal):
        out = []
        for n in range(n_chain):
            m_old, l_old = carry[n]
            st = scores(j, n, diagonal)
            m_new = jnp.maximum(m_old, jnp.max(st, axis=0, keepdims=True))
            alpha = jnp.exp2(m_old - m_new)
            p = jnp.exp2(st - m_new)
            l_new = alpha * l_old + jnp.sum(p, axis=0, keepdims=True)
            pv = jnp.dot(values_t(j, n), p.astype(_BF), preferred_element_type=_F32)
            acc_ref[n] = alpha * acc_ref[n] + pv
            out.append((m_new, l_new))
        return tuple(out)

    def finish(ls):
        lamv = lamv_ref[...]
        lam = (jnp.exp(jnp.sum(lamv[0:1] * lamv[1:2], axis=-1, keepdims=True))
               - jnp.exp(jnp.sum(lamv[2:3] * lamv[3:4], axis=-1, keepdims=True)) + lam_init)
        for head in range(A_PAIR):
            o = (acc_ref[2 * head] / ls[2 * head]
                 - lam * (acc_ref[2 * head + 1] / ls[2 * head + 1]))
            ms = jnp.mean(o * o, axis=0, keepdims=True)
            y = o * lax.rsqrt(ms + EPS) * subln_ref[...] * (1.0 - lam_init)
            o_ref[:, head * A_DV:(head + 1) * A_DV] = y.T.astype(_BF)

    acc_ref[...] = jnp.zeros_like(acc_ref)
    bounded = bound_ref[0] <= SAFE_LOG2_SCORE

    @pl.when(bounded)
    def _():
        init = tuple(jnp.zeros((1, blk), _F32) for _ in range(n_chain))
        carry = lax.fori_loop(0, i, lambda j, c: plain_step(j, c, False), init)
        finish(plain_step(i, carry, True))

    @pl.when(jnp.logical_not(bounded))
    def _():
        init = tuple((jnp.full((1, blk), NEG_BIG, _F32), jnp.zeros((1, blk), _F32))
                     for _ in range(n_chain))
        carry = lax.fori_loop(0, i, lambda j, c: shifted_step(j, c, False), init)
        finish([l for _, l in shifted_step(i, carry, True)])


def _diff_attn(score_bound, aq, ak, avt, lamv, subln_col, batch, seq, lam_init):
    blk = ATTN_BLOCK
    nq = seq // blk
    t = batch * seq
    width = A_PAIR * A_DV
    est = 2 * (seq * width * 2) * 2 + 2 * A_PAIR * (A_DV * blk * 4) + 16 * blk * blk * 4 \
        + VMEM_TEMPORARIES
    return pl.pallas_call(
        functools.partial(_diff_attn_kernel, lam_init=lam_init),
        grid=(batch, A_HEADS // A_PAIR, nq),
        in_specs=[
            pl.BlockSpec(memory_space=pltpu.SMEM),
            pl.BlockSpec((blk, width), lambda b, h, i: (b * nq + i, h)),
            pl.BlockSpec((seq, width), lambda b, h, i: (b, h)),
            pl.BlockSpec((nq, width, blk), lambda b, h, i: (b, h, 0)),
            pl.BlockSpec((4, D_HEAD), lambda b, h, i: (0, 0)),
            pl.BlockSpec((A_DV, 1), lambda b, h, i: (0, 0)),
        ],
        out_specs=pl.BlockSpec((blk, width), lambda b, h, i: (b * nq + i, h)),
        out_shape=jax.ShapeDtypeStruct((t, A_V), _BF),
        scratch_shapes=[pltpu.VMEM((2 * A_PAIR, A_DV, blk), _F32)],
        compiler_params=pltpu.CompilerParams(
            dimension_semantics=("parallel", "parallel", "parallel"),
            vmem_limit_bytes=_vmem_limit(est)),
        name="diff_attn",
    )(score_bound, aq, ak, avt, lamv, subln_col)


def _score_bound(q_gain, k_gain):
    return (D_HEAD * NORM_MARGIN * jnp.max(jnp.abs(q_gain)) * jnp.max(jnp.abs(k_gain))).reshape(1)


B_TABLES = (B_WINDOW - B_QTILE) // B_QTILE + 1
B_GEN_LEN = 2048
B_GROUP = MXU_DIM // D_HEAD


def _build_bias_tables(gen_ref, mb_ref):
    qchunk = lax.broadcasted_iota(jnp.int32, (B_QTILE, B_WINDOW), 0) // CHUNK
    kchunk = lax.broadcasted_iota(jnp.int32, (B_QTILE, B_WINDOW), 1) // CHUNK
    for hh in range(B_GROUP):
        rolled = pltpu.roll(jnp.broadcast_to(gen_ref[hh], (B_QTILE, B_GEN_LEN)), 0, 1,
                            stride=1, stride_axis=0)
        for t in range(B_TABLES):
            lo = B_WINDOW - t * B_QTILE
            dchunk = qchunk - kchunk + (t * B_QTILE) // CHUNK
            visible = (dchunk >= 0) & (dchunk <= B_LOOKBACK)
            mb_ref[hh, t] = jnp.where(visible, rolled[:, lo:lo + B_WINDOW], NEG_BIG)


def _chunk_attn_kernel(bound_ref, q_ref, k_ref, v_ref, gen_ref, o_ref, mb_ref):
    i = pl.program_id(2)

    @pl.when(i == 0)
    def _():
        _build_bias_tables(gen_ref, mb_ref)

    start = pl.multiple_of(jnp.maximum(i * B_QTILE - (B_WINDOW - B_QTILE), 0), B_QTILE)
    table = jnp.minimum(i, B_TABLES - 1)
    lane = lax.broadcasted_iota(jnp.int32, (1, MXU_DIM), 1)
    q = q_ref[...]
    zero = jnp.zeros_like(q)
    kw = k_ref[pl.ds(start, B_WINDOW), :]
    vw = v_ref[pl.ds(start, B_WINDOW), :]
    in_head = [(lane >= D_HEAD * hh) & (lane < D_HEAD * (hh + 1)) for hh in range(B_GROUP)]

    def logits(hh):
        s = lax.dot_general(jnp.where(in_head[hh], q, zero), kw, _NT, preferred_element_type=_F32)
        return s + mb_ref[hh, table]

    def attend(shifted):
        out = None
        s = logits(0)
        for hh in range(B_GROUP):
            s_next = logits(hh + 1) if hh + 1 < B_GROUP else None
            if shifted:
                s = s - jnp.max(s, axis=-1, keepdims=True)
            p = jnp.exp2(s)
            l = jnp.sum(p, axis=-1, keepdims=True)
            o = jnp.dot(p.astype(_BF), vw, preferred_element_type=_F32) / l
            out = o if out is None else jnp.where(in_head[hh], o, out)
            s = s_next
        o_ref[...] = out.astype(_BF)

    bounded = bound_ref[0] <= SAFE_LOG2_SCORE
    pl.when(bounded)(lambda: attend(False))
    pl.when(jnp.logical_not(bounded))(lambda: attend(True))


def _chunk_attn(score_bound, bq, bk, bv, bias_gen, batch, seq):
    nq = seq // B_QTILE
    t = batch * seq
    est = B_GROUP * B_TABLES * B_QTILE * B_WINDOW * 4 + 4 * B_QTILE * B_GEN_LEN * 4 \
        + 4 * seq * MXU_DIM * 2 + VMEM_TEMPORARIES
    return pl.pallas_call(
        _chunk_attn_kernel,
        grid=(batch, B_HEADS // B_GROUP, nq),
        in_specs=[
            pl.BlockSpec(memory_space=pltpu.SMEM),
            pl.BlockSpec((B_QTILE, MXU_DIM), lambda b, h, i: (b * nq + i, h)),
            pl.BlockSpec((seq, MXU_DIM), lambda b, h, i: (b, h)),
            pl.BlockSpec((seq, MXU_DIM), lambda b, h, i: (b, h)),
            pl.BlockSpec((B_GROUP, 1, B_GEN_LEN), lambda b, h, i: (h, 0, 0)),
        ],
        out_specs=pl.BlockSpec((B_QTILE, MXU_DIM), lambda b, h, i: (b * nq + i, h)),
        out_shape=jax.ShapeDtypeStruct((t, B_QKV), _BF),
        scratch_shapes=[pltpu.VMEM((B_GROUP, B_TABLES, B_QTILE, B_WINDOW), _F32)],
        compiler_params=pltpu.CompilerParams(
            dimension_semantics=("parallel", "parallel", "arbitrary"),
            vmem_limit_bytes=_vmem_limit(est)),
        name="chunk_attn",
    )(score_bound, bq, bk, bv, bias_gen)


def _bias_generator(rel_bias):
    b = rel_bias.astype(_F32) * LOG2E
    n_far = B_WINDOW - B_MAX_REL + 1
    mid = b[:, 1:CHUNK - 1 + B_MAX_REL][:, ::-1]
    n_neg = B_GEN_LEN - n_far - mid.shape[1]
    heads = b.shape[0]
    gen = jnp.concatenate([
        jnp.broadcast_to(b[:, -1:], (heads, n_far)), mid,
        jnp.broadcast_to(b[:, :1], (heads, n_neg))], axis=1)
    return gen[:, None, :]


def _retention_kernel(q_ref, kt_ref, v_ref, g_ref, cn_ref, o_ref, decay_ref, *, seq):
    blk = RET_BLOCK
    h = pl.program_id(1)
    hf = jnp.full((1, 1), h, jnp.int32).astype(_F32)
    log_gamma = jnp.log(1.0 - jnp.exp2(-5.0 - hf))
    diff = (lax.broadcasted_iota(jnp.int32, (blk, blk), 0)
            - lax.broadcasted_iota(jnp.int32, (blk, blk), 1)).astype(_F32)
    decay_ref[...] = jnp.where(diff >= 0, jnp.exp(log_gamma * jnp.maximum(diff, 0.0)), 0.0)
    pos = lax.broadcasted_iota(jnp.int32, (blk, LANES), 0).astype(_F32)
    xi = jnp.exp(log_gamma * (pos + 1.0))
    zeta = jnp.exp(log_gamma * (blk - 1.0 - pos))
    block_decay = jnp.exp(log_gamma * float(blk))
    odd = (h % 2) == 1
    q_keep = (lax.broadcasted_iota(jnp.int32, (1, LANES), 1) >= D_HEAD) == odd
    k_keep = (lax.broadcasted_iota(jnp.int32, (LANES, 1), 0) >= D_HEAD) == odd
    gain = cn_ref[0]

    def body(j, state):
        rows = pl.ds(pl.multiple_of(j * blk, blk), blk)
        qb = q_ref[rows, :]
        qb = jnp.where(q_keep, qb, jnp.zeros_like(qb))
        kt = kt_ref[j]
        kt = jnp.where(k_keep, kt, jnp.zeros_like(kt))
        vb = v_ref[rows, :]
        sc = jnp.dot(qb, kt, preferred_element_type=_F32) * decay_ref[...]
        inner = jnp.dot(sc.astype(_BF), vb, preferred_element_type=_F32)
        cross = jnp.dot(qb, state.astype(_BF), preferred_element_type=_F32) * xi
        o = inner + cross
        vz = (vb.astype(_F32) * zeta).astype(_BF)
        new_state = state * block_decay + jnp.dot(kt, vz, preferred_element_type=_F32)
        ms = jnp.mean(o * o, axis=-1, keepdims=True)
        y = o * lax.rsqrt(ms + EPS) * gain
        gg = g_ref[rows, :].astype(_F32)
        o_ref[rows, :] = (gg * jax.nn.sigmoid(gg) * y).astype(_BF)
        return new_state

    lax.fori_loop(0, seq // blk, body, jnp.zeros((LANES, C_DV), _F32), unroll=RET_UNROLL)


def _retention(cq, ckt, cv, cg, cnorm, batch, seq):
    blk = RET_BLOCK
    nb = seq // blk
    t = batch * seq
    est = 2 * 5 * seq * LANES * 2 + 6 * blk * blk * 4 + VMEM_TEMPORARIES
    return pl.pallas_call(
        functools.partial(_retention_kernel, seq=seq),
        grid=(batch, C_HEADS),
        in_specs=[
            pl.BlockSpec((seq, LANES), lambda b, h: (b, h // 2)),
            pl.BlockSpec((nb, LANES, blk), lambda b, h: (b, h // 2, 0)),
            pl.BlockSpec((seq, C_DV), lambda b, h: (b, h)),
            pl.BlockSpec((seq, C_DV), lambda b, h: (b, h)),
            pl.BlockSpec((1, 1, C_DV), lambda b, h: (h, 0, 0)),
        ],
        out_specs=pl.BlockSpec((seq, C_DV), lambda b, h: (b, h)),
        out_shape=jax.ShapeDtypeStruct((t, C_V), _BF),
        scratch_shapes=[pltpu.VMEM((blk, blk), _F32)],
        compiler_params=pltpu.CompilerParams(
            dimension_semantics=("parallel", "parallel"), vmem_limit_bytes=_vmem_limit(est)),
        name="retention",
    )(cq, ckt, cv, cg, cnorm)


def _merge_ffn_kernel(x_ref, ya_ref, yb_ref, yc_ref, gate_ref, wa_ref, wb_ref, wc_ref, wo_ref,
                      gain_ref, wg_ref, wu_ref, wd_ref, o_ref, acc_ref):
    merged = None
    for n, (y_ref, w_ref) in enumerate(((ya_ref, wa_ref), (yb_ref, wb_ref), (yc_ref, wc_ref))):
        br = jnp.dot(y_ref[...], w_ref[...], preferred_element_type=_F32)
        term = gate_ref[:, n * D_MODEL:(n + 1) * D_MODEL].astype(_F32) * br
        merged = term if merged is None else merged + term
    x = x_ref[...] + jnp.dot(merged.astype(_BF), wo_ref[...], preferred_element_type=_F32)
    h = _rms_rows(x, gain_ref[...]).astype(_BF)
    for c in range(D_FF // FF_CHUNK):
        sl = slice(c * FF_CHUNK, (c + 1) * FF_CHUNK)
        g = jnp.dot(h, wg_ref[:, sl], preferred_element_type=_F32)
        u = jnp.dot(h, wu_ref[:, sl], preferred_element_type=_F32)
        a = (g * jax.nn.sigmoid(g) * u).astype(_BF)
        part = jnp.dot(a, wd_ref[sl, :], preferred_element_type=_F32)
        if c == 0:
            acc_ref[...] = part
        else:
            acc_ref[...] += part
    o_ref[...] = x + 0.5 * acc_ref[...]


def _merge_ffn(layer, x, ya, yb, yc, gates, wa, wb, wc, wo, gain, wg, wu, wd):
    t = x.shape[0]
    tm = TOKEN_TILE
    row = lambda i: (i, 0)
    const = lambda i: (0, 0)
    est = (5 * D_MODEL * D_MODEL + 3 * D_MODEL * D_FF) * 2 \
        + 2 * tm * (2 * D_MODEL * 4 + (A_V + B_QKV + C_V) * 2 + N_BRANCH * D_MODEL * 2) \
        + 8 * tm * D_MODEL * 4 + 4 * tm * FF_CHUNK * 4 + VMEM_TEMPORARIES
    return pl.pallas_call(
        _merge_ffn_kernel,
        grid=(t // tm,),
        in_specs=[
            pl.BlockSpec((tm, D_MODEL), row),
            pl.BlockSpec((tm, A_V), row),
            pl.BlockSpec((tm, B_QKV), row),
            pl.BlockSpec((tm, C_V), row),
            pl.BlockSpec((tm, N_BRANCH * D_MODEL), row),
            _layer_weight(layer, A_V, D_MODEL),
            _layer_weight(layer, B_QKV, D_MODEL),
            _layer_weight(layer, C_V, D_MODEL),
            _layer_weight(layer, D_MODEL, D_MODEL),
            _resident((1, D_MODEL), const),
            _layer_weight(layer, D_MODEL, D_FF),
            _layer_weight(layer, D_MODEL, D_FF),
            _layer_weight(layer, D_FF, D_MODEL),
        ],
        out_specs=pl.BlockSpec((tm, D_MODEL), row),
        out_shape=jax.ShapeDtypeStruct((t, D_MODEL), _F32),
        scratch_shapes=[pltpu.VMEM((tm, D_MODEL), _F32)],
        compiler_params=pltpu.CompilerParams(
            dimension_semantics=("parallel",), vmem_limit_bytes=_vmem_limit(est)),
        name="merge_ffn",
    )(x, ya, yb, yc, gates, wa, wb, wc, wo, gain, wg, wu, wd)


def _rope_tables(seq):
    half = D_HEAD // 2
    inv = ROPE_THETA ** (-jnp.arange(half, dtype=_F32) / half)
    ang = jnp.arange(seq, dtype=_F32)[:, None] * inv[None, :]
    cos, sin = jnp.cos(ang), jnp.sin(ang)
    cos_t = jnp.concatenate([cos, cos] * (LANES // D_HEAD), axis=1)
    sin_t = jnp.concatenate([-sin, sin] * (LANES // D_HEAD), axis=1)
    return cos_t, sin_t


def _head_mean_matrix():
    g = np.kron(np.eye(MXU_DIM // D_HEAD), np.full((D_HEAD, D_HEAD), 1.0 / D_HEAD))
    return jnp.asarray(g, dtype=_BF)


def kernel(x, ffn1_norm, ffn1_w_gate, ffn1_w_up, ffn1_w_down, mix_norm, w_in, a_q_norm, a_k_norm, a_lambda_q1, a_lambda_k1, a_lambda_q2, a_lambda_k2, a_subln, b_q_norm, b_k_norm, b_rel_bias, c_out_norm, w_branch_a, w_branch_b, w_branch_c, w_out, ffn2_norm, ffn2_w_gate, ffn2_w_up, ffn2_w_down):
    batch, seq, d = x.shape
    assert d == D_MODEL and seq % ATTN_BLOCK == 0 and seq % RET_BLOCK == 0
    assert seq % TOKEN_TILE == 0 and seq >= B_WINDOW and ATTN_BLOCK == TOKEN_TILE == RET_BLOCK
    scale = D_HEAD ** -0.5
    cos_t, sin_t = _rope_tables(seq)
    gmat = _head_mean_matrix()
    bf = lambda w: w.astype(_BF)
    f1g, f1u, f1d = bf(ffn1_w_gate), bf(ffn1_w_up), bf(ffn1_w_down)
    f2g, f2u, f2d = bf(ffn2_w_gate), bf(ffn2_w_up), bf(ffn2_w_down)
    win, wba, wbb, wbc, wo = bf(w_in), bf(w_branch_a), bf(w_branch_b), bf(w_branch_c), bf(w_out)
    xt = x.reshape(batch * seq, D_MODEL)
    for l in range(DEPTH):
        lam_init = 0.8 - 0.6 * math.exp(-0.3 * l)
        xt = _ffn(l, xt, ffn1_norm[l][None, :], f1g, f1u, f1d)

        heads = A_QK // D_HEAD
        qk_gains = jnp.stack([
            jnp.tile(a_q_norm[l].astype(_F32) * (scale * LOG2E), heads),
            jnp.tile(a_k_norm[l].astype(_F32), heads),
            jnp.tile(b_q_norm[l].astype(_F32) * (scale * LOG2E), heads),
            jnp.tile(b_k_norm[l].astype(_F32), heads),
        ])
        aq, ak, avt, bq, bk, bv, cq, ckt, cv, cg, gates = _in_proj(
            l, xt, mix_norm[l][None, :], win, cos_t, sin_t, qk_gains, gmat, seq)

        lamv = jnp.stack([a_lambda_q1[l], a_lambda_k1[l], a_lambda_q2[l], a_lambda_k2[l]]).astype(_F32)
        a_bound = _score_bound(qk_gains[0, :D_HEAD], qk_gains[1, :D_HEAD])
        ya = _diff_attn(a_bound, aq, ak, avt, lamv, a_subln[l].astype(_F32)[:, None], batch, seq, lam_init)
        bias_gen = _bias_generator(b_rel_bias[l])
        b_bound = _score_bound(qk_gains[2, :D_HEAD], qk_gains[3, :D_HEAD]) + jnp.max(jnp.abs(bias_gen))
        yb = _chunk_attn(b_bound, bq, bk, bv, bias_gen, batch, seq)
        yc = _retention(cq, ckt, cv, cg, c_out_norm[l].astype(_F32)[:, None, :], batch, seq)

        xt = _merge_ffn(l, xt, ya, yb, yc, gates, wba, wbb, wbc, wo, ffn2_norm[l][None, :],
                        f2g, f2u, f2d)
    return xt.reshape(batch, seq, D_MODEL)
```

```python
import functools
import math

import numpy as np
import jax
import jax.numpy as jnp
from jax import lax
from jax.experimental import pallas as pl
from jax.experimental.pallas import tpu as pltpu

D_MODEL = 1024
DEPTH = 2
CHUNK = 64
D_HEAD = 64
ROPE_THETA = 10000.0
EPS = 1e-6
A_HEADS = 4
A_DV = 2 * D_HEAD
B_HEADS = 8
B_LOOKBACK = 8
B_MAX_REL = 256
C_HEADS = 4
C_DV = 2 * D_HEAD
N_BRANCH = 3
D_FF = 2816

A_QK = A_HEADS * 2 * D_HEAD
A_V = A_HEADS * A_DV
B_QKV = B_HEADS * D_HEAD
C_QK = C_HEADS * D_HEAD
C_V = C_HEADS * C_DV
SPLITS = (A_QK, A_QK, A_V, B_QKV, B_QKV, B_QKV, C_QK, C_QK, C_V, C_V, N_BRANCH * D_MODEL)
IN_COLS = sum(SPLITS)
_OFFS = tuple(int(v) for v in np.cumsum((0,) + SPLITS))

LANES = 128
MXU_DIM = 256
VMEM_BYTES_V7X = 64 * 1024 * 1024
VMEM_HEADROOM = 6 * 1024 * 1024
VMEM_TEMPORARIES = 12 * 1024 * 1024

TOKEN_TILE = 512
FF_CHUNK = MXU_DIM
PROJ_CHUNK = 512
ATTN_BLOCK = 512
B_QTILE = 256
B_WINDOW = B_QTILE + B_LOOKBACK * CHUNK
RET_BLOCK = 512
RET_UNROLL = 16
NEG_BIG = -1e30

_BF = jnp.bfloat16
_F32 = jnp.float32
_NT = (((1,), (1,)), ((), ()))


def _vmem_limit(estimate_bytes):
    return int(min(VMEM_BYTES_V7X - VMEM_HEADROOM, max(32 * 1024 * 1024, estimate_bytes)))


def _resident(shape, index_map):
    return pl.BlockSpec(shape, index_map, pipeline_mode=pl.Buffered(1))


def _layer_weight(layer, rows, cols):
    return pl.BlockSpec((None, rows, cols), lambda i: (layer, 0, 0), pipeline_mode=pl.Buffered(1))


def _rms_rows(x, gain):
    ms = jnp.mean(x * x, axis=-1, keepdims=True)
    return x * lax.rsqrt(ms + EPS) * gain


def _ffn_kernel(x_ref, gain_ref, wg_ref, wu_ref, wd_ref, o_ref, acc_ref):
    x = x_ref[...]
    h = _rms_rows(x, gain_ref[...]).astype(_BF)
    for c in range(D_FF // FF_CHUNK):
        sl = slice(c * FF_CHUNK, (c + 1) * FF_CHUNK)
        g = jnp.dot(h, wg_ref[:, sl], preferred_element_type=_F32)
        u = jnp.dot(h, wu_ref[:, sl], preferred_element_type=_F32)
        a = (g * jax.nn.sigmoid(g) * u).astype(_BF)
        part = jnp.dot(a, wd_ref[sl, :], preferred_element_type=_F32)
        if c == 0:
            acc_ref[...] = part
        else:
            acc_ref[...] += part
    o_ref[...] = x + 0.5 * acc_ref[...]


def _ffn(layer, x, gain, wg, wu, wd):
    t = x.shape[0]
    tm = TOKEN_TILE
    est = 3 * D_MODEL * D_FF * 2 + 5 * tm * D_MODEL * 4 + 4 * tm * FF_CHUNK * 4 + VMEM_TEMPORARIES
    return pl.pallas_call(
        _ffn_kernel,
        grid=(t // tm,),
        in_specs=[
            pl.BlockSpec((tm, D_MODEL), lambda i: (i, 0)),
            _resident((1, D_MODEL), lambda i: (0, 0)),
            _layer_weight(layer, D_MODEL, D_FF),
            _layer_weight(layer, D_MODEL, D_FF),
            _layer_weight(layer, D_FF, D_MODEL),
        ],
        out_specs=pl.BlockSpec((tm, D_MODEL), lambda i: (i, 0)),
        out_shape=jax.ShapeDtypeStruct((t, D_MODEL), _F32),
        scratch_shapes=[pltpu.VMEM((tm, D_MODEL), _F32)],
        compiler_params=pltpu.CompilerParams(
            dimension_semantics=("parallel",), vmem_limit_bytes=_vmem_limit(est)),
        name="ffn",
    )(x, gain, wg, wu, wd)


def _swap_halves(x):
    lane = lax.broadcasted_iota(jnp.int32, (1, LANES), 1)
    upper = (lane & (D_HEAD // 2)) != 0
    outs = []
    for s in range(x.shape[1] // LANES):
        xs = x[:, s * LANES:(s + 1) * LANES]
        from_below = pltpu.roll(xs, D_HEAD // 2, 1)
        from_above = pltpu.roll(xs, LANES - D_HEAD // 2, 1)
        outs.append(jnp.where(upper, from_below, from_above))
    return outs[0] if len(outs) == 1 else jnp.concatenate(outs, axis=1)


def _tile_lanes(t, n):
    reps = n // t.shape[1]
    return t if reps == 1 else jnp.concatenate([t] * reps, axis=1)


def _rope(x, cos, sin):
    n = x.shape[1]
    return x * _tile_lanes(cos, n) + _swap_halves(x) * _tile_lanes(sin, n)


def _head_ms(p, gmat):
    sq = (p * p).astype(_BF)
    outs = [jnp.dot(sq[:, s:s + MXU_DIM], gmat, preferred_element_type=_F32)
            for s in range(0, p.shape[1], MXU_DIM)]
    return outs[0] if len(outs) == 1 else jnp.concatenate(outs, axis=1)


def _in_proj_kernel(x_ref, gain_ref, w_ref, cos_ref, sin_ref, qkg_ref, gmat_ref,
                    aq_ref, ak_ref, avt_ref, bq_ref, bk_ref, bv_ref,
                    cq_ref, ckt_ref, cv_ref, cg_ref, gate_ref):
    h = _rms_rows(x_ref[...], gain_ref[...]).astype(_BF)
    cos = cos_ref[...]
    sin = sin_ref[...]
    gmat = gmat_ref[...]

    def proj(lo, width):
        return jnp.dot(h, w_ref[:, lo:lo + width], preferred_element_type=_F32)

    def normed(p, row):
        return p * lax.rsqrt(_head_ms(p, gmat) + EPS) * qkg_ref[row:row + 1, :]

    aq_ref[...] = _rope(normed(proj(_OFFS[0], A_QK), 0), cos, sin).astype(_BF)
    ak_ref[...] = _rope(normed(proj(_OFFS[1], A_QK), 1), cos, sin).astype(_BF)
    avt_ref[0] = proj(_OFFS[2], A_V).T.astype(_BF)
    bq_ref[...] = normed(proj(_OFFS[3], B_QKV), 2).astype(_BF)
    bk_ref[...] = normed(proj(_OFFS[4], B_QKV), 3).astype(_BF)
    bv_ref[...] = proj(_OFFS[5], B_QKV).astype(_BF)
    cq_ref[...] = _rope(proj(_OFFS[6], C_QK), cos, sin).astype(_BF)
    ckt_ref[0] = (_rope(proj(_OFFS[7], C_QK), cos, sin) * (D_HEAD ** -0.5)).T.astype(_BF)
    cv_ref[...] = proj(_OFFS[8], C_V).astype(_BF)
    cg_ref[...] = proj(_OFFS[9], C_V).astype(_BF)
    for c in range(N_BRANCH * D_MODEL // PROJ_CHUNK):
        lo = c * PROJ_CHUNK
        gate_ref[:, lo:lo + PROJ_CHUNK] = jax.nn.sigmoid(
            proj(_OFFS[10] + lo, PROJ_CHUNK)).astype(_BF)


def _in_proj(layer, x, gain, w, cos_t, sin_t, qk_gains, gmat, seq):
    t = x.shape[0]
    tm = TOKEN_TILE
    nt = t // tm
    pos_blocks = seq // tm
    row = lambda i: (i, 0)
    tok_major = lambda n: pl.BlockSpec((tm, n), row)
    out_shapes = (
        jax.ShapeDtypeStruct((t, A_QK), _BF),
        jax.ShapeDtypeStruct((t, A_QK), _BF),
        jax.ShapeDtypeStruct((nt, A_V, tm), _BF),
        jax.ShapeDtypeStruct((t, B_QKV), _BF),
        jax.ShapeDtypeStruct((t, B_QKV), _BF),
        jax.ShapeDtypeStruct((t, B_QKV), _BF),
        jax.ShapeDtypeStruct((t, C_QK), _BF),
        jax.ShapeDtypeStruct((nt, C_QK, tm), _BF),
        jax.ShapeDtypeStruct((t, C_V), _BF),
        jax.ShapeDtypeStruct((t, C_V), _BF),
        jax.ShapeDtypeStruct((t, N_BRANCH * D_MODEL), _BF),
    )
    out_specs = (
        tok_major(A_QK), tok_major(A_QK),
        pl.BlockSpec((1, A_V, tm), lambda i: (i, 0, 0)),
        tok_major(B_QKV), tok_major(B_QKV), tok_major(B_QKV),
        tok_major(C_QK),
        pl.BlockSpec((1, C_QK, tm), lambda i: (i, 0, 0)),
        tok_major(C_V), tok_major(C_V), tok_major(N_BRANCH * D_MODEL),
    )
    est = D_MODEL * IN_COLS * 2 + 2 * tm * IN_COLS * 2 + 4 * tm * D_MODEL * 4 + VMEM_TEMPORARIES
    return pl.pallas_call(
        _in_proj_kernel,
        grid=(nt,),
        in_specs=[
            pl.BlockSpec((tm, D_MODEL), row),
            _resident((1, D_MODEL), lambda i: (0, 0)),
            _layer_weight(layer, D_MODEL, IN_COLS),
            pl.BlockSpec((tm, LANES), lambda i: (i % pos_blocks, 0)),
            pl.BlockSpec((tm, LANES), lambda i: (i % pos_blocks, 0)),
            _resident((4, A_QK), lambda i: (0, 0)),
            _resident((MXU_DIM, MXU_DIM), lambda i: (0, 0)),
        ],
        out_specs=out_specs,
        out_shape=out_shapes,
        compiler_params=pltpu.CompilerParams(
            dimension_semantics=("parallel",), vmem_limit_bytes=_vmem_limit(est)),
        name="in_proj",
    )(x, gain, w, cos_t, sin_t, qk_gains, gmat)


LOG2E = math.log2(math.e)
SAFE_LOG2_SCORE = 60.0
NORM_MARGIN = 1.05
A_PAIR = 4


def _diff_attn_kernel(bound_ref, q_ref, k_ref, vt_ref, lamv_ref, subln_ref, o_ref, acc_ref, *,
                      lam_init):
    blk = ATTN_BLOCK
    i = pl.program_id(2)
    n_chain = 2 * A_PAIR
    per_group = MXU_DIM // D_HEAD
    lane = lax.broadcasted_iota(jnp.int32, (1, MXU_DIM), 1)
    qs = []
    for n in range(n_chain):
        g, r = divmod(n, per_group)
        qg = q_ref[:, g * MXU_DIM:(g + 1) * MXU_DIM]
        qs.append(jnp.where((lane >= D_HEAD * r) & (lane < D_HEAD * (r + 1)), qg,
                            jnp.zeros_like(qg)))
    half = blk // 2

    def scores(j, n, diagonal):
        g = n // per_group
        kb = k_ref[pl.ds(pl.multiple_of(j * blk, blk), blk), g * MXU_DIM:(g + 1) * MXU_DIM]
        st = lax.dot_general(kb, qs[n], _NT, preferred_element_type=_F32)
        if diagonal:
            key_chunk = lax.broadcasted_iota(jnp.int32, (blk, blk), 0) // CHUNK
            qry_chunk = lax.broadcasted_iota(jnp.int32, (blk, blk), 1) // CHUNK
            st = jnp.where(key_chunk <= qry_chunk, st, NEG_BIG)
        return st

    def diagonal_scores(j, n):
        g = n // per_group
        row0 = pl.multiple_of(j * blk, blk)
        cols = slice(g * MXU_DIM, (g + 1) * MXU_DIM)
        key_chunk = lax.broadcasted_iota(jnp.int32, (half, blk), 0) // CHUNK
        qry_chunk = lax.broadcasted_iota(jnp.int32, (half, blk), 1) // CHUNK
        first = lax.dot_general(k_ref[pl.ds(row0, half), cols], qs[n], _NT,
                                preferred_element_type=_F32)
        first = jnp.where(key_chunk <= qry_chunk, first, NEG_BIG)
        second = lax.dot_general(k_ref[pl.ds(row0 + half, half), cols], qs[n][half:, :], _NT,
                                 preferred_element_type=_F32)
        key_chunk2 = lax.broadcasted_iota(jnp.int32, (half, half), 0) // CHUNK
        qry_chunk2 = lax.broadcasted_iota(jnp.int32, (half, half), 1) // CHUNK
        second = jnp.where(key_chunk2 <= qry_chunk2, second, NEG_BIG)
        return first, second

    def values_t(j, n):
        head = n // 2
        return vt_ref[j, head * A_DV:(head + 1) * A_DV, :]

    def plain_steps(blocks, carry):
        out = list(carry)
        work = [(j, diagonal, list(range(g * per_group, (g + 1) * per_group)))
                for j, diagonal in blocks for g in range(n_chain // per_group)]

        def issue(j, diagonal, chains):
            return {n: (diagonal_scores(j, n) if diagonal else scores(j, n, False))
                    for n in chains}

        pending = issue(*work[0])
        for idx, (j, diagonal, chains) in enumerate(work):
            sts = pending
            if idx + 1 < len(work):
                pending = issue(*work[idx + 1])
            for n in chains:
                if diagonal:
                    first, second = (jnp.exp2(s) for s in sts[n])
                    widened = jnp.concatenate([jnp.zeros_like(second), second], axis=1)
                    out[n] = (out[n] + jnp.sum(first, axis=0, keepdims=True)
                              + jnp.sum(widened, axis=0, keepdims=True))
                    vt = values_t(j, n)
                    acc_ref[n] += jnp.dot(vt[:, :half], first.astype(_BF),
                                          preferred_element_type=_F32)
                    acc_ref[n, :, half:] += jnp.dot(vt[:, half:], second.astype(_BF),
                                                    preferred_element_type=_F32)
                else:
                    p = jnp.exp2(sts[n])
                    out[n] = out[n] + jnp.sum(p, axis=0, keepdims=True)
                    acc_ref[n] += jnp.dot(values_t(j, n), p.astype(_BF),
                                          preferred_element_type=_F32)
        return tuple(out)

    def shifted_step(j, carry, diagonal):
        out = []
        for n in range(n_chain):
            m_old, l_old = carry[n]
            st = scores(j, n, diagonal)
            m_new = jnp.maximum(m_old, jnp.max(st, axis=0, keepdims=True))
            alpha = jnp.exp2(m_old - m_new)
            p = jnp.exp2(st - m_new)
            l_new = alpha * l_old + jnp.sum(p, axis=0, keepdims=True)
            pv = jnp.dot(values_t(j, n), p.astype(_BF), preferred_element_type=_F32)
            acc_ref[n] = alpha * acc_ref[n] + pv
            out.append((m_new, l_new))
        return tuple(out)

    def finish(ls):
        lamv = lamv_ref[...]
        lam = (jnp.exp(jnp.sum(lamv[0:1] * lamv[1:2], axis=-1, keepdims=True))
               - jnp.exp(jnp.sum(lamv[2:3] * lamv[3:4], axis=-1, keepdims=True)) + lam_init)
        for head in range(A_PAIR):
            o = (acc_ref[2 * head] / ls[2 * head]
                 - lam * (acc_ref[2 * head + 1] / ls[2 * head + 1]))
            ms = jnp.mean(o * o, axis=0, keepdims=True)
            y = o * lax.rsqrt(ms + EPS) * subln_ref[...] * (1.0 - lam_init)
            o_ref[:, head * A_DV:(head + 1) * A_DV] = y.T.astype(_BF)

    acc_ref[...] = jnp.zeros_like(acc_ref)
    bounded = bound_ref[0] <= SAFE_LOG2_SCORE

    @pl.when(bounded)
    def _():
        init = tuple(jnp.zeros((1, blk), _F32) for _ in range(n_chain))
        carry = lax.fori_loop(
            0, i // 2, lambda jj, c: plain_steps([(2 * jj, False), (2 * jj + 1, False)], c), init)
        odd = (i % 2) == 1
        pl.when(odd)(lambda: finish(plain_steps([(i - 1, False), (i, True)], carry)))
        pl.when(jnp.logical_not(odd))(lambda: finish(plain_steps([(i, True)], carry)))

    @pl.when(jnp.logical_not(bounded))
    def _():
        init = tuple((jnp.full((1, blk), NEG_BIG, _F32), jnp.zeros((1, blk), _F32))
                     for _ in range(n_chain))
        carry = lax.fori_loop(0, i, lambda j, c: shifted_step(j, c, False), init)
        finish([l for _, l in shifted_step(i, carry, True)])


def _diff_attn(score_bound, aq, ak, avt, lamv, subln_col, batch, seq, lam_init):
    blk = ATTN_BLOCK
    nq = seq // blk
    t = batch * seq
    width = A_PAIR * A_DV
    est = 2 * (seq * width * 2) * 2 + 2 * A_PAIR * (A_DV * blk * 4) + 16 * blk * blk * 4 \
        + VMEM_TEMPORARIES
    return pl.pallas_call(
        functools.partial(_diff_attn_kernel, lam_init=lam_init),
        grid=(batch, A_HEADS // A_PAIR, nq),
        in_specs=[
            pl.BlockSpec(memory_space=pltpu.SMEM),
            pl.BlockSpec((blk, width), lambda b, h, i: (b * nq + i, h)),
            pl.BlockSpec((seq, width), lambda b, h, i: (b, h)),
            pl.BlockSpec((nq, width, blk), lambda b, h, i: (b, h, 0)),
            pl.BlockSpec((4, D_HEAD), lambda b, h, i: (0, 0)),
            pl.BlockSpec((A_DV, 1), lambda b, h, i: (0, 0)),
        ],
        out_specs=pl.BlockSpec((blk, width), lambda b, h, i: (b * nq + i, h)),
        out_shape=jax.ShapeDtypeStruct((t, A_V), _BF),
        scratch_shapes=[pltpu.VMEM((2 * A_PAIR, A_DV, blk), _F32)],
        compiler_params=pltpu.CompilerParams(
            dimension_semantics=("parallel", "parallel", "parallel"),
            vmem_limit_bytes=_vmem_limit(est)),
        name="diff_attn",
    )(score_bound, aq, ak, avt, lamv, subln_col)


def _score_bound(q_gain, k_gain):
    return (D_HEAD * NORM_MARGIN * jnp.max(jnp.abs(q_gain)) * jnp.max(jnp.abs(k_gain))).reshape(1)


B_TABLES = (B_WINDOW - B_QTILE) // B_QTILE + 1
B_GEN_LEN = 2048
B_GROUP = MXU_DIM // D_HEAD


def _build_bias_tables(gen_ref, mb_ref):
    qchunk = lax.broadcasted_iota(jnp.int32, (B_QTILE, B_WINDOW), 0) // CHUNK
    kchunk = lax.broadcasted_iota(jnp.int32, (B_QTILE, B_WINDOW), 1) // CHUNK
    for hh in range(B_GROUP):
        rolled = pltpu.roll(jnp.broadcast_to(gen_ref[hh], (B_QTILE, B_GEN_LEN)), 0, 1,
                            stride=1, stride_axis=0)
        for t in range(B_TABLES):
            lo = B_WINDOW - t * B_QTILE
            dchunk = qchunk - kchunk + (t * B_QTILE) // CHUNK
            visible = (dchunk >= 0) & (dchunk <= B_LOOKBACK)
            mb_ref[hh, t] = jnp.where(visible, rolled[:, lo:lo + B_WINDOW], NEG_BIG)


def _chunk_attn_kernel(bound_ref, q_ref, k_ref, v_ref, gen_ref, o_ref, mb_ref):
    i = pl.program_id(2)

    @pl.when(i == 0)
    def _():
        _build_bias_tables(gen_ref, mb_ref)

    start = pl.multiple_of(jnp.maximum(i * B_QTILE - (B_WINDOW - B_QTILE), 0), B_QTILE)
    table = jnp.minimum(i, B_TABLES - 1)
    lane = lax.broadcasted_iota(jnp.int32, (1, MXU_DIM), 1)
    q = q_ref[...]
    zero = jnp.zeros_like(q)
    kw = k_ref[pl.ds(start, B_WINDOW), :]
    vw = v_ref[pl.ds(start, B_WINDOW), :]
    in_head = [(lane >= D_HEAD * hh) & (lane < D_HEAD * (hh + 1)) for hh in range(B_GROUP)]

    def logits(hh):
        s = lax.dot_general(jnp.where(in_head[hh], q, zero), kw, _NT, preferred_element_type=_F32)
        return s + mb_ref[hh, table]

    def attend(shifted):
        out = None
        s = logits(0)
        for hh in range(B_GROUP):
            s_next = logits(hh + 1) if hh + 1 < B_GROUP else None
            if shifted:
                s = s - jnp.max(s, axis=-1, keepdims=True)
            p = jnp.exp2(s)
            l = jnp.sum(p, axis=-1, keepdims=True)
            o = jnp.dot(p.astype(_BF), vw, preferred_element_type=_F32) / l
            out = o if out is None else jnp.where(in_head[hh], o, out)
            s = s_next
        o_ref[...] = out.astype(_BF)

    bounded = bound_ref[0] <= SAFE_LOG2_SCORE
    pl.when(bounded)(lambda: attend(False))
    pl.when(jnp.logical_not(bounded))(lambda: attend(True))


def _chunk_attn(score_bound, bq, bk, bv, bias_gen, batch, seq):
    nq = seq // B_QTILE
    t = batch * seq
    est = B_GROUP * B_TABLES * B_QTILE * B_WINDOW * 4 + 4 * B_QTILE * B_GEN_LEN * 4 \
        + 4 * seq * MXU_DIM * 2 + VMEM_TEMPORARIES
    return pl.pallas_call(
        _chunk_attn_kernel,
        grid=(batch, B_HEADS // B_GROUP, nq),
        in_specs=[
            pl.BlockSpec(memory_space=pltpu.SMEM),
            pl.BlockSpec((B_QTILE, MXU_DIM), lambda b, h, i: (b * nq + i, h)),
            pl.BlockSpec((seq, MXU_DIM), lambda b, h, i: (b, h)),
            pl.BlockSpec((seq, MXU_DIM), lambda b, h, i: (b, h)),
            pl.BlockSpec((B_GROUP, 1, B_GEN_LEN), lambda b, h, i: (h, 0, 0)),
        ],
        out_specs=pl.BlockSpec((B_QTILE, MXU_DIM), lambda b, h, i: (b * nq + i, h)),
        out_shape=jax.ShapeDtypeStruct((t, B_QKV), _BF),
        scratch_shapes=[pltpu.VMEM((B_GROUP, B_TABLES, B_QTILE, B_WINDOW), _F32)],
        compiler_params=pltpu.CompilerParams(
            dimension_semantics=("parallel", "parallel", "arbitrary"),
            vmem_limit_bytes=_vmem_limit(est)),
        name="chunk_attn",
    )(score_bound, bq, bk, bv, bias_gen)


def _bias_generator(rel_bias):
    b = rel_bias.astype(_F32) * LOG2E
    n_far = B_WINDOW - B_MAX_REL + 1
    mid = b[:, 1:CHUNK - 1 + B_MAX_REL][:, ::-1]
    n_neg = B_GEN_LEN - n_far - mid.shape[1]
    heads = b.shape[0]
    gen = jnp.concatenate([
        jnp.broadcast_to(b[:, -1:], (heads, n_far)), mid,
        jnp.broadcast_to(b[:, :1], (heads, n_neg))], axis=1)
    return gen[:, None, :]


def _retention_kernel(q_ref, kt_ref, v_ref, g_ref, cn_ref, o_ref, decay_ref, *, seq):
    blk = RET_BLOCK
    h = pl.program_id(1)
    hf = jnp.full((1, 1), h, jnp.int32).astype(_F32)
    log_gamma = jnp.log(1.0 - jnp.exp2(-5.0 - hf))
    diff = (lax.broadcasted_iota(jnp.int32, (blk, blk), 0)
            - lax.broadcasted_iota(jnp.int32, (blk, blk), 1)).astype(_F32)
    decay_ref[...] = jnp.where(diff >= 0, jnp.exp(log_gamma * jnp.maximum(diff, 0.0)), 0.0)
    pos = lax.broadcasted_iota(jnp.int32, (blk, LANES), 0).astype(_F32)
    xi = jnp.exp(log_gamma * (pos + 1.0))
    zeta = jnp.exp(log_gamma * (blk - 1.0 - pos))
    block_decay = jnp.exp(log_gamma * float(blk))
    odd = (h % 2) == 1
    q_keep = (lax.broadcasted_iota(jnp.int32, (1, LANES), 1) >= D_HEAD) == odd
    k_keep = (lax.broadcasted_iota(jnp.int32, (LANES, 1), 0) >= D_HEAD) == odd
    gain = cn_ref[0]

    def body(j, state):
        rows = pl.ds(pl.multiple_of(j * blk, blk), blk)
        qb = q_ref[rows, :]
        qb = jnp.where(q_keep, qb, jnp.zeros_like(qb))
        kt = kt_ref[j]
        kt = jnp.where(k_keep, kt, jnp.zeros_like(kt))
        vb = v_ref[rows, :]
        sc = jnp.dot(qb, kt, preferred_element_type=_F32) * decay_ref[...]
        inner = jnp.dot(sc.astype(_BF), vb, preferred_element_type=_F32)
        cross = jnp.dot(qb, state.astype(_BF), preferred_element_type=_F32) * xi
        o = inner + cross
        vz = (vb.astype(_F32) * zeta).astype(_BF)
        new_state = state * block_decay + jnp.dot(kt, vz, preferred_element_type=_F32)
        ms = jnp.mean(o * o, axis=-1, keepdims=True)
        y = o * lax.rsqrt(ms + EPS) * gain
        gg = g_ref[rows, :].astype(_F32)
        o_ref[rows, :] = (gg * jax.nn.sigmoid(gg) * y).astype(_BF)
        return new_state

    lax.fori_loop(0, seq // blk, body, jnp.zeros((LANES, C_DV), _F32), unroll=RET_UNROLL)


def _retention(cq, ckt, cv, cg, cnorm, batch, seq):
    blk = RET_BLOCK
    nb = seq // blk
    t = batch * seq
    est = 2 * 5 * seq * LANES * 2 + 6 * blk * blk * 4 + VMEM_TEMPORARIES
    return pl.pallas_call(
        functools.partial(_retention_kernel, seq=seq),
        grid=(batch, C_HEADS),
        in_specs=[
            pl.BlockSpec((seq, LANES), lambda b, h: (b, h // 2)),
            pl.BlockSpec((nb, LANES, blk), lambda b, h: (b, h // 2, 0)),
            pl.BlockSpec((seq, C_DV), lambda b, h: (b, h)),
            pl.BlockSpec((seq, C_DV), lambda b, h: (b, h)),
            pl.BlockSpec((1, 1, C_DV), lambda b, h: (h, 0, 0)),
        ],
        out_specs=pl.BlockSpec((seq, C_DV), lambda b, h: (b, h)),
        out_shape=jax.ShapeDtypeStruct((t, C_V), _BF),
        scratch_shapes=[pltpu.VMEM((blk, blk), _F32)],
        compiler_params=pltpu.CompilerParams(
            dimension_semantics=("parallel", "parallel"), vmem_limit_bytes=_vmem_limit(est)),
        name="retention",
    )(cq, ckt, cv, cg, cnorm)


def _merge_ffn_kernel(x_ref, ya_ref, yb_ref, yc_ref, gate_ref, wa_ref, wb_ref, wc_ref, wo_ref,
                      gain_ref, wg_ref, wu_ref, wd_ref, o_ref, acc_ref):
    merged = None
    for n, (y_ref, w_ref) in enumerate(((ya_ref, wa_ref), (yb_ref, wb_ref), (yc_ref, wc_ref))):
        br = jnp.dot(y_ref[...], w_ref[...], preferred_element_type=_F32)
        term = gate_ref[:, n * D_MODEL:(n + 1) * D_MODEL].astype(_F32) * br
        merged = term if merged is None else merged + term
    x = x_ref[...] + jnp.dot(merged.astype(_BF), wo_ref[...], preferred_element_type=_F32)
    h = _rms_rows(x, gain_ref[...]).astype(_BF)
    for c in range(D_FF // FF_CHUNK):
        sl = slice(c * FF_CHUNK, (c + 1) * FF_CHUNK)
        g = jnp.dot(h, wg_ref[:, sl], preferred_element_type=_F32)
        u = jnp.dot(h, wu_ref[:, sl], preferred_element_type=_F32)
        a = (g * jax.nn.sigmoid(g) * u).astype(_BF)
        part = jnp.dot(a, wd_ref[sl, :], preferred_element_type=_F32)
        if c == 0:
            acc_ref[...] = part
        else:
            acc_ref[...] += part
    o_ref[...] = x + 0.5 * acc_ref[...]


def _merge_ffn(layer, x, ya, yb, yc, gates, wa, wb, wc, wo, gain, wg, wu, wd):
    t = x.shape[0]
    tm = TOKEN_TILE
    row = lambda i: (i, 0)
    const = lambda i: (0, 0)
    est = (5 * D_MODEL * D_MODEL + 3 * D_MODEL * D_FF) * 2 \
        + 2 * tm * (2 * D_MODEL * 4 + (A_V + B_QKV + C_V) * 2 + N_BRANCH * D_MODEL * 2) \
        + 8 * tm * D_MODEL * 4 + 4 * tm * FF_CHUNK * 4 + VMEM_TEMPORARIES
    return pl.pallas_call(
        _merge_ffn_kernel,
        grid=(t // tm,),
        in_specs=[
            pl.BlockSpec((tm, D_MODEL), row),
            pl.BlockSpec((tm, A_V), row),
            pl.BlockSpec((tm, B_QKV), row),
            pl.BlockSpec((tm, C_V), row),
            pl.BlockSpec((tm, N_BRANCH * D_MODEL), row),
            _layer_weight(layer, A_V, D_MODEL),
            _layer_weight(layer, B_QKV, D_MODEL),
            _layer_weight(layer, C_V, D_MODEL),
            _layer_weight(layer, D_MODEL, D_MODEL),
            _resident((1, D_MODEL), const),
            _layer_weight(layer, D_MODEL, D_FF),
            _layer_weight(layer, D_MODEL, D_FF),
            _layer_weight(layer, D_FF, D_MODEL),
        ],
        out_specs=pl.BlockSpec((tm, D_MODEL), row),
        out_shape=jax.ShapeDtypeStruct((t, D_MODEL), _F32),
        scratch_shapes=[pltpu.VMEM((tm, D_MODEL), _F32)],
        compiler_params=pltpu.CompilerParams(
            dimension_semantics=("parallel",), vmem_limit_bytes=_vmem_limit(est)),
        name="merge_ffn",
    )(x, ya, yb, yc, gates, wa, wb, wc, wo, gain, wg, wu, wd)


def _rope_tables(seq):
    half = D_HEAD // 2
    inv = ROPE_THETA ** (-jnp.arange(half, dtype=_F32) / half)
    ang = jnp.arange(seq, dtype=_F32)[:, None] * inv[None, :]
    cos, sin = jnp.cos(ang), jnp.sin(ang)
    cos_t = jnp.concatenate([cos, cos] * (LANES // D_HEAD), axis=1)
    sin_t = jnp.concatenate([-sin, sin] * (LANES // D_HEAD), axis=1)
    return cos_t, sin_t


def _head_mean_matrix():
    g = np.kron(np.eye(MXU_DIM // D_HEAD), np.full((D_HEAD, D_HEAD), 1.0 / D_HEAD))
    return jnp.asarray(g, dtype=_BF)


def kernel(x, ffn1_norm, ffn1_w_gate, ffn1_w_up, ffn1_w_down, mix_norm, w_in, a_q_norm, a_k_norm, a_lambda_q1, a_lambda_k1, a_lambda_q2, a_lambda_k2, a_subln, b_q_norm, b_k_norm, b_rel_bias, c_out_norm, w_branch_a, w_branch_b, w_branch_c, w_out, ffn2_norm, ffn2_w_gate, ffn2_w_up, ffn2_w_down):
    batch, seq, d = x.shape
    assert d == D_MODEL and seq % ATTN_BLOCK == 0 and seq % RET_BLOCK == 0
    assert seq % TOKEN_TILE == 0 and seq >= B_WINDOW and ATTN_BLOCK == TOKEN_TILE == RET_BLOCK
    scale = D_HEAD ** -0.5
    cos_t, sin_t = _rope_tables(seq)
    gmat = _head_mean_matrix()
    bf = lambda w: w.astype(_BF)
    f1g, f1u, f1d = bf(ffn1_w_gate), bf(ffn1_w_up), bf(ffn1_w_down)
    f2g, f2u, f2d = bf(ffn2_w_gate), bf(ffn2_w_up), bf(ffn2_w_down)
    win, wba, wbb, wbc, wo = bf(w_in), bf(w_branch_a), bf(w_branch_b), bf(w_branch_c), bf(w_out)
    xt = x.reshape(batch * seq, D_MODEL)
    for l in range(DEPTH):
        lam_init = 0.8 - 0.6 * math.exp(-0.3 * l)
        xt = _ffn(l, xt, ffn1_norm[l][None, :], f1g, f1u, f1d)

        heads = A_QK // D_HEAD
        qk_gains = jnp.stack([
            jnp.tile(a_q_norm[l].astype(_F32) * (scale * LOG2E), heads),
            jnp.tile(a_k_norm[l].astype(_F32), heads),
            jnp.tile(b_q_norm[l].astype(_F32) * (scale * LOG2E), heads),
            jnp.tile(b_k_norm[l].astype(_F32), heads),
        ])
        aq, ak, avt, bq, bk, bv, cq, ckt, cv, cg, gates = _in_proj(
            l, xt, mix_norm[l][None, :], win, cos_t, sin_t, qk_gains, gmat, seq)

        lamv = jnp.stack([a_lambda_q1[l], a_lambda_k1[l], a_lambda_q2[l], a_lambda_k2[l]]).astype(_F32)
        a_bound = _score_bound(qk_gains[0, :D_HEAD], qk_gains[1, :D_HEAD])
        ya = _diff_attn(a_bound, aq, ak, avt, lamv, a_subln[l].astype(_F32)[:, None], batch, seq, lam_init)
        bias_gen = _bias_generator(b_rel_bias[l])
        b_bound = _score_bound(qk_gains[2, :D_HEAD], qk_gains[3, :D_HEAD]) + jnp.max(jnp.abs(bias_gen))
        yb = _chunk_attn(b_bound, bq, bk, bv, bias_gen, batch, seq)
        yc = _retention(cq, ckt, cv, cg, c_out_norm[l].astype(_F32)[:, None, :], batch, seq)

        xt = _merge_ffn(l, xt, ya, yb, yc, gates, wba, wbb, wbc, wo, ffn2_norm[l][None, :],
                        f2g, f2u, f2d)
    return xt.reshape(batch, seq, D_MODEL)
```

```python
import functools
import math

import numpy as np
import jax
import jax.numpy as jnp
from jax import lax
from jax.experimental import pallas as pl
from jax.experimental.pallas import tpu as pltpu

D_MODEL = 1024
DEPTH = 2
CHUNK = 64
D_HEAD = 64
ROPE_THETA = 10000.0
EPS = 1e-6
A_HEADS = 4
A_DV = 2 * D_HEAD
B_HEADS = 8
B_LOOKBACK = 8
B_MAX_REL = 256
C_HEADS = 4
C_DV = 2 * D_HEAD
N_BRANCH = 3
D_FF = 2816

A_QK = A_HEADS * 2 * D_HEAD
A_V = A_HEADS * A_DV
B_QKV = B_HEADS * D_HEAD
C_QK = C_HEADS * D_HEAD
C_V = C_HEADS * C_DV
SPLITS = (A_QK, A_QK, A_V, B_QKV, B_QKV, B_QKV, C_QK, C_QK, C_V, C_V, N_BRANCH * D_MODEL)
IN_COLS = sum(SPLITS)
_OFFS = tuple(int(v) for v in np.cumsum((0,) + SPLITS))

LANES = 128
MXU_DIM = 256
VMEM_BYTES_V7X = 64 * 1024 * 1024
VMEM_HEADROOM = 6 * 1024 * 1024
VMEM_TEMPORARIES = 12 * 1024 * 1024

TOKEN_TILE = 512
FF_CHUNK = MXU_DIM
PROJ_CHUNK = 512
ATTN_BLOCK = 512
B_QTILE = 256
B_WINDOW = B_QTILE + B_LOOKBACK * CHUNK
RET_BLOCK = 512
RET_UNROLL = 16
NEG_BIG = -1e30

_BF = jnp.bfloat16
_F32 = jnp.float32
_NT = (((1,), (1,)), ((), ()))


def _vmem_limit(estimate_bytes):
    return int(min(VMEM_BYTES_V7X - VMEM_HEADROOM, max(32 * 1024 * 1024, estimate_bytes)))


def _resident(shape, index_map):
    return pl.BlockSpec(shape, index_map, pipeline_mode=pl.Buffered(1))


def _layer_weight(layer, rows, cols):
    return pl.BlockSpec((None, rows, cols), lambda i: (layer, 0, 0), pipeline_mode=pl.Buffered(1))


def _rms_rows(x, gain):
    ms = jnp.mean(x * x, axis=-1, keepdims=True)
    return x * lax.rsqrt(ms + EPS) * gain


def _ffn_kernel(x_ref, gain_ref, wg_ref, wu_ref, wd_ref, o_ref, acc_ref):
    x = x_ref[...]
    h = _rms_rows(x, gain_ref[...]).astype(_BF)
    for c in range(D_FF // FF_CHUNK):
        sl = slice(c * FF_CHUNK, (c + 1) * FF_CHUNK)
        g = jnp.dot(h, wg_ref[:, sl], preferred_element_type=_F32)
        u = jnp.dot(h, wu_ref[:, sl], preferred_element_type=_F32)
        a = (g * jax.nn.sigmoid(g) * u).astype(_BF)
        part = jnp.dot(a, wd_ref[sl, :], preferred_element_type=_F32)
        if c == 0:
            acc_ref[...] = part
        else:
            acc_ref[...] += part
    o_ref[...] = x + 0.5 * acc_ref[...]


def _ffn(layer, x, gain, wg, wu, wd):
    t = x.shape[0]
    tm = TOKEN_TILE
    est = 3 * D_MODEL * D_FF * 2 + 5 * tm * D_MODEL * 4 + 4 * tm * FF_CHUNK * 4 + VMEM_TEMPORARIES
    return pl.pallas_call(
        _ffn_kernel,
        grid=(t // tm,),
        in_specs=[
            pl.BlockSpec((tm, D_MODEL), lambda i: (i, 0)),
            _resident((1, D_MODEL), lambda i: (0, 0)),
            _layer_weight(layer, D_MODEL, D_FF),
            _layer_weight(layer, D_MODEL, D_FF),
            _layer_weight(layer, D_FF, D_MODEL),
        ],
        out_specs=pl.BlockSpec((tm, D_MODEL), lambda i: (i, 0)),
        out_shape=jax.ShapeDtypeStruct((t, D_MODEL), _F32),
        scratch_shapes=[pltpu.VMEM((tm, D_MODEL), _F32)],
        compiler_params=pltpu.CompilerParams(
            dimension_semantics=("parallel",), vmem_limit_bytes=_vmem_limit(est)),
        name="ffn",
    )(x, gain, wg, wu, wd)


def _swap_halves(x):
    lane = lax.broadcasted_iota(jnp.int32, (1, LANES), 1)
    upper = (lane & (D_HEAD // 2)) != 0
    outs = []
    for s in range(x.shape[1] // LANES):
        xs = x[:, s * LANES:(s + 1) * LANES]
        from_below = pltpu.roll(xs, D_HEAD // 2, 1)
        from_above = pltpu.roll(xs, LANES - D_HEAD // 2, 1)
        outs.append(jnp.where(upper, from_below, from_above))
    return outs[0] if len(outs) == 1 else jnp.concatenate(outs, axis=1)


def _tile_lanes(t, n):
    reps = n // t.shape[1]
    return t if reps == 1 else jnp.concatenate([t] * reps, axis=1)


def _rope(x, cos, sin):
    n = x.shape[1]
    return x * _tile_lanes(cos, n) + _swap_halves(x) * _tile_lanes(sin, n)


def _head_ms(p, gmat):
    sq = (p * p).astype(_BF)
    outs = [jnp.dot(sq[:, s:s + MXU_DIM], gmat, preferred_element_type=_F32)
            for s in range(0, p.shape[1], MXU_DIM)]
    return outs[0] if len(outs) == 1 else jnp.concatenate(outs, axis=1)


def _in_proj_kernel(x_ref, gain_ref, w_ref, cos_ref, sin_ref, qkg_ref, gmat_ref,
                    aq_ref, ak_ref, avt_ref, bq_ref, bk_ref, bv_ref,
                    cq_ref, ckt_ref, cv_ref, cg_ref, gate_ref):
    h = _rms_rows(x_ref[...], gain_ref[...]).astype(_BF)
    cos = cos_ref[...]
    sin = sin_ref[...]
    gmat = gmat_ref[...]

    def proj(lo, width):
        return jnp.dot(h, w_ref[:, lo:lo + width], preferred_element_type=_F32)

    def normed(p, row):
        return p * lax.rsqrt(_head_ms(p, gmat) + EPS) * qkg_ref[row:row + 1, :]

    aq_ref[...] = _rope(normed(proj(_OFFS[0], A_QK), 0), cos, sin).astype(_BF)
    ak_ref[...] = _rope(normed(proj(_OFFS[1], A_QK), 1), cos, sin).astype(_BF)
    avt_ref[0] = proj(_OFFS[2], A_V).T.astype(_BF)
    bq_ref[...] = normed(proj(_OFFS[3], B_QKV), 2).astype(_BF)
    bk_ref[...] = normed(proj(_OFFS[4], B_QKV), 3).astype(_BF)
    bv_ref[...] = proj(_OFFS[5], B_QKV).astype(_BF)
    cq_ref[...] = _rope(proj(_OFFS[6], C_QK), cos, sin).astype(_BF)
    ckt_ref[0] = (_rope(proj(_OFFS[7], C_QK), cos, sin) * (D_HEAD ** -0.5)).T.astype(_BF)
    cv_ref[...] = proj(_OFFS[8], C_V).astype(_BF)
    cg_ref[...] = proj(_OFFS[9], C_V).astype(_BF)
    for c in range(N_BRANCH * D_MODEL // PROJ_CHUNK):
        lo = c * PROJ_CHUNK
        gate_ref[:, lo:lo + PROJ_CHUNK] = jax.nn.sigmoid(
            proj(_OFFS[10] + lo, PROJ_CHUNK)).astype(_BF)


def _in_proj(layer, x, gain, w, cos_t, sin_t, qk_gains, gmat, seq):
    t = x.shape[0]
    tm = TOKEN_TILE
    nt = t // tm
    pos_blocks = seq // tm
    row = lambda i: (i, 0)
    tok_major = lambda n: pl.BlockSpec((tm, n), row)
    out_shapes = (
        jax.ShapeDtypeStruct((t, A_QK), _BF),
        jax.ShapeDtypeStruct((t, A_QK), _BF),
        jax.ShapeDtypeStruct((nt, A_V, tm), _BF),
        jax.ShapeDtypeStruct((t, B_QKV), _BF),
        jax.ShapeDtypeStruct((t, B_QKV), _BF),
        jax.ShapeDtypeStruct((t, B_QKV), _BF),
        jax.ShapeDtypeStruct((t, C_QK), _BF),
        jax.ShapeDtypeStruct((nt, C_QK, tm), _BF),
        jax.ShapeDtypeStruct((t, C_V), _BF),
        jax.ShapeDtypeStruct((t, C_V), _BF),
        jax.ShapeDtypeStruct((t, N_BRANCH * D_MODEL), _BF),
    )
    out_specs = (
        tok_major(A_QK), tok_major(A_QK),
        pl.BlockSpec((1, A_V, tm), lambda i: (i, 0, 0)),
        tok_major(B_QKV), tok_major(B_QKV), tok_major(B_QKV),
        tok_major(C_QK),
        pl.BlockSpec((1, C_QK, tm), lambda i: (i, 0, 0)),
        tok_major(C_V), tok_major(C_V), tok_major(N_BRANCH * D_MODEL),
    )
    est = D_MODEL * IN_COLS * 2 + 2 * tm * IN_COLS * 2 + 4 * tm * D_MODEL * 4 + VMEM_TEMPORARIES
    return pl.pallas_call(
        _in_proj_kernel,
        grid=(nt,),
        in_specs=[
            pl.BlockSpec((tm, D_MODEL), row),
            _resident((1, D_MODEL), lambda i: (0, 0)),
            _layer_weight(layer, D_MODEL, IN_COLS),
            pl.BlockSpec((tm, LANES), lambda i: (i % pos_blocks, 0)),
            pl.BlockSpec((tm, LANES), lambda i: (i % pos_blocks, 0)),
            _resident((4, A_QK), lambda i: (0, 0)),
            _resident((MXU_DIM, MXU_DIM), lambda i: (0, 0)),
        ],
        out_specs=out_specs,
        out_shape=out_shapes,
        compiler_params=pltpu.CompilerParams(
            dimension_semantics=("parallel",), vmem_limit_bytes=_vmem_limit(est)),
        name="in_proj",
    )(x, gain, w, cos_t, sin_t, qk_gains, gmat)


LOG2E = math.log2(math.e)
SAFE_LOG2_SCORE = 60.0
NORM_MARGIN = 1.05
A_PAIR = 4


def _diff_attn_kernel(bound_ref, q_ref, k_ref, vt_ref, lamv_ref, subln_ref, o_ref, acc_ref, *,
                      lam_init):
    blk = ATTN_BLOCK
    i = pl.program_id(2)
    n_chain = 2 * A_PAIR
    per_group = MXU_DIM // D_HEAD
    lane = lax.broadcasted_iota(jnp.int32, (1, MXU_DIM), 1)
    qs = []
    for n in range(n_chain):
        g, r = divmod(n, per_group)
        qg = q_ref[:, g * MXU_DIM:(g + 1) * MXU_DIM]
        qs.append(jnp.where((lane >= D_HEAD * r) & (lane < D_HEAD * (r + 1)), qg,
                            jnp.zeros_like(qg)))
    half = blk // 2

    def scores(j, n, diagonal):
        g = n // per_group
        kb = k_ref[pl.ds(pl.multiple_of(j * blk, blk), blk), g * MXU_DIM:(g + 1) * MXU_DIM]
        st = lax.dot_general(kb, qs[n], _NT, preferred_element_type=_F32)
        if diagonal:
            key_chunk = lax.broadcasted_iota(jnp.int32, (blk, blk), 0) // CHUNK
            qry_chunk = lax.broadcasted_iota(jnp.int32, (blk, blk), 1) // CHUNK
            st = jnp.where(key_chunk <= qry_chunk, st, NEG_BIG)
        return st

    def diagonal_scores(j, n):
        g = n // per_group
        row0 = pl.multiple_of(j * blk, blk)
        cols = slice(g * MXU_DIM, (g + 1) * MXU_DIM)
        key_chunk = lax.broadcasted_iota(jnp.int32, (half, blk), 0) // CHUNK
        qry_chunk = lax.broadcasted_iota(jnp.int32, (half, blk), 1) // CHUNK
        first = lax.dot_general(k_ref[pl.ds(row0, half), cols], qs[n], _NT,
                                preferred_element_type=_F32)
        first = jnp.where(key_chunk <= qry_chunk, first, NEG_BIG)
        second = lax.dot_general(k_ref[pl.ds(row0 + half, half), cols], qs[n][half:, :], _NT,
                                 preferred_element_type=_F32)
        key_chunk2 = lax.broadcasted_iota(jnp.int32, (half, half), 0) // CHUNK
        qry_chunk2 = lax.broadcasted_iota(jnp.int32, (half, half), 1) // CHUNK
        second = jnp.where(key_chunk2 <= qry_chunk2, second, NEG_BIG)
        return first, second

    def values_t(j, n):
        head = n // 2
        return vt_ref[j, head * A_DV:(head + 1) * A_DV, :]

    def plain_steps(blocks, carry):
        out = list(carry)
        work = [(j, diagonal, list(range(g * per_group, (g + 1) * per_group)))
                for j, diagonal in blocks for g in range(n_chain // per_group)]

        def issue(j, diagonal, chains):
            return {n: (diagonal_scores(j, n) if diagonal else scores(j, n, False))
                    for n in chains}

        pending = issue(*work[0])
        for idx, (j, diagonal, chains) in enumerate(work):
            sts = pending
            if idx + 1 < len(work):
                pending = issue(*work[idx + 1])
            for n in chains:
                if diagonal:
                    first, second = (jnp.exp2(s) for s in sts[n])
                    widened = jnp.concatenate([jnp.zeros_like(second), second], axis=1)
                    out[n] = (out[n] + jnp.sum(first, axis=0, keepdims=True)
                              + jnp.sum(widened, axis=0, keepdims=True))
                    vt = values_t(j, n)
                    acc_ref[n] += jnp.dot(vt[:, :half], first.astype(_BF),
                                          preferred_element_type=_F32)
                    acc_ref[n, :, half:] += jnp.dot(vt[:, half:], second.astype(_BF),
                                                    preferred_element_type=_F32)
                else:
                    p = jnp.exp2(sts[n])
                    out[n] = out[n] + jnp.sum(p, axis=0, keepdims=True)
                    acc_ref[n] += jnp.dot(values_t(j, n), p.astype(_BF),
                                          preferred_element_type=_F32)
        return tuple(out)

    def shifted_step(j, carry, diagonal):
        out = []
        for n in range(n_chain):
            m_old, l_old = carry[n]
            st = scores(j, n, diagonal)
            m_new = jnp.maximum(m_old, jnp.max(st, axis=0, keepdims=True))
            alpha = jnp.exp2(m_old - m_new)
            p = jnp.exp2(st - m_new)
            l_new = alpha * l_old + jnp.sum(p, axis=0, keepdims=True)
            pv = jnp.dot(values_t(j, n), p.astype(_BF), preferred_element_type=_F32)
            acc_ref[n] = alpha * acc_ref[n] + pv
            out.append((m_new, l_new))
        return tuple(out)

    def finish(ls):
        lamv = lamv_ref[...]
        lam = (jnp.exp(jnp.sum(lamv[0:1] * lamv[1:2], axis=-1, keepdims=True))
               - jnp.exp(jnp.sum(lamv[2:3] * lamv[3:4], axis=-1, keepdims=True)) + lam_init)
        for head in range(A_PAIR):
            o = (acc_ref[2 * head] / ls[2 * head]
                 - lam * (acc_ref[2 * head + 1] / ls[2 * head + 1]))
            ms = jnp.mean(o * o, axis=0, keepdims=True)
            y = o * lax.rsqrt(ms + EPS) * subln_ref[...] * (1.0 - lam_init)
            o_ref[:, head * A_DV:(head + 1) * A_DV] = y.T.astype(_BF)

    acc_ref[...] = jnp.zeros_like(acc_ref)
    bounded = bound_ref[0] <= SAFE_LOG2_SCORE

    @pl.when(bounded)
    def _():
        init = tuple(jnp.zeros((1, blk), _F32) for _ in range(n_chain))
        carry = lax.fori_loop(
            0, i // 2, lambda jj, c: plain_steps([(2 * jj, False), (2 * jj + 1, False)], c), init)
        odd = (i % 2) == 1
        pl.when(odd)(lambda: finish(plain_steps([(i - 1, False), (i, True)], carry)))
        pl.when(jnp.logical_not(odd))(lambda: finish(plain_steps([(i, True)], carry)))

    @pl.when(jnp.logical_not(bounded))
    def _():
        init = tuple((jnp.full((1, blk), NEG_BIG, _F32), jnp.zeros((1, blk), _F32))
                     for _ in range(n_chain))
        carry = lax.fori_loop(0, i, lambda j, c: shifted_step(j, c, False), init)
        finish([l for _, l in shifted_step(i, carry, True)])


def _diff_attn(score_bound, aq, ak, avt, lamv, subln_col, batch, seq, lam_init):
    blk = ATTN_BLOCK
    nq = seq // blk
    t = batch * seq
    width = A_PAIR * A_DV
    est = 2 * (seq * width * 2) * 2 + 2 * A_PAIR * (A_DV * blk * 4) + 16 * blk * blk * 4 \
        + VMEM_TEMPORARIES
    return pl.pallas_call(
        functools.partial(_diff_attn_kernel, lam_init=lam_init),
        grid=(batch, A_HEADS // A_PAIR, nq),
        in_specs=[
            pl.BlockSpec(memory_space=pltpu.SMEM),
            pl.BlockSpec((blk, width), lambda b, h, i: (b * nq + i, h)),
            pl.BlockSpec((seq, width), lambda b, h, i: (b, h)),
            pl.BlockSpec((nq, width, blk), lambda b, h, i: (b, h, 0)),
            pl.BlockSpec((4, D_HEAD), lambda b, h, i: (0, 0)),
            pl.BlockSpec((A_DV, 1), lambda b, h, i: (0, 0)),
        ],
        out_specs=pl.BlockSpec((blk, width), lambda b, h, i: (b * nq + i, h)),
        out_shape=jax.ShapeDtypeStruct((t, A_V), _BF),
        scratch_shapes=[pltpu.VMEM((2 * A_PAIR, A_DV, blk), _F32)],
        compiler_params=pltpu.CompilerParams(
            dimension_semantics=("parallel", "parallel", "parallel"),
            vmem_limit_bytes=_vmem_limit(est)),
        name="diff_attn",
    )(score_bound, aq, ak, avt, lamv, subln_col)


def _score_bound(q_gain, k_gain):
    return (D_HEAD * NORM_MARGIN * jnp.max(jnp.abs(q_gain)) * jnp.max(jnp.abs(k_gain))).reshape(1)


B_TABLES = (B_WINDOW - B_QTILE) // B_QTILE + 1
B_GEN_LEN = 2048
B_GROUP = MXU_DIM // D_HEAD


def _build_bias_tables(gen_ref, mb_ref):
    qchunk = lax.broadcasted_iota(jnp.int32, (B_QTILE, B_WINDOW), 0) // CHUNK
    kchunk = lax.broadcasted_iota(jnp.int32, (B_QTILE, B_WINDOW), 1) // CHUNK
    for hh in range(B_GROUP):
        rolled = pltpu.roll(jnp.broadcast_to(gen_ref[hh], (B_QTILE, B_GEN_LEN)), 0, 1,
                            stride=1, stride_axis=0)
        for t in range(B_TABLES):
            lo = B_WINDOW - t * B_QTILE
            dchunk = qchunk - kchunk + (t * B_QTILE) // CHUNK
            visible = (dchunk >= 0) & (dchunk <= B_LOOKBACK)
            mb_ref[hh, t] = jnp.where(visible, rolled[:, lo:lo + B_WINDOW], NEG_BIG)


B_TILES = 2


def _chunk_attn_kernel(bound_ref, q_ref, k_ref, v_ref, gen_ref, o_ref, mb_ref):
    i = pl.program_id(2)

    @pl.when(i == 0)
    def _():
        _build_bias_tables(gen_ref, mb_ref)

    lane = lax.broadcasted_iota(jnp.int32, (1, MXU_DIM), 1)
    in_head = [(lane >= D_HEAD * hh) & (lane < D_HEAD * (hh + 1)) for hh in range(B_GROUP)]
    tiles = []
    for tt in range(B_TILES):
        tile = i * B_TILES + tt
        start = pl.multiple_of(jnp.maximum(tile * B_QTILE - (B_WINDOW - B_QTILE), 0), B_QTILE)
        tiles.append(dict(
            rows=slice(tt * B_QTILE, (tt + 1) * B_QTILE),
            table=jnp.minimum(tile, B_TABLES - 1),
            kw=k_ref[pl.ds(start, B_WINDOW), :],
            vw=v_ref[pl.ds(start, B_WINDOW), :]))

    def logits(tt, hh):
        q = q_ref[tiles[tt]["rows"], :]
        s = lax.dot_general(jnp.where(in_head[hh], q, jnp.zeros_like(q)), tiles[tt]["kw"], _NT,
                            preferred_element_type=_F32)
        return s + mb_ref[hh, tiles[tt]["table"]]

    def attend(shifted):
        work = [(tt, hh) for tt in range(B_TILES) for hh in range(B_GROUP)]
        outs = [None] * B_TILES
        s = logits(*work[0])
        for idx, (tt, hh) in enumerate(work):
            s_next = logits(*work[idx + 1]) if idx + 1 < len(work) else None
            if shifted:
                s = s - jnp.max(s, axis=-1, keepdims=True)
            p = jnp.exp2(s)
            l = jnp.sum(p, axis=-1, keepdims=True)
            o = jnp.dot(p.astype(_BF), tiles[tt]["vw"], preferred_element_type=_F32) / l
            outs[tt] = o if outs[tt] is None else jnp.where(in_head[hh], o, outs[tt])
            s = s_next
        for tt in range(B_TILES):
            o_ref[tiles[tt]["rows"], :] = outs[tt].astype(_BF)

    bounded = bound_ref[0] <= SAFE_LOG2_SCORE
    pl.when(bounded)(lambda: attend(False))
    pl.when(jnp.logical_not(bounded))(lambda: attend(True))


def _chunk_attn(score_bound, bq, bk, bv, bias_gen, batch, seq):
    step_rows = B_TILES * B_QTILE
    nq = seq // step_rows
    t = batch * seq
    est = B_GROUP * B_TABLES * B_QTILE * B_WINDOW * 4 + 4 * B_QTILE * B_GEN_LEN * 4 \
        + 4 * seq * MXU_DIM * 2 + VMEM_TEMPORARIES
    return pl.pallas_call(
        _chunk_attn_kernel,
        grid=(batch, B_HEADS // B_GROUP, nq),
        in_specs=[
            pl.BlockSpec(memory_space=pltpu.SMEM),
            pl.BlockSpec((step_rows, MXU_DIM), lambda b, h, i: (b * nq + i, h)),
            pl.BlockSpec((seq, MXU_DIM), lambda b, h, i: (b, h)),
            pl.BlockSpec((seq, MXU_DIM), lambda b, h, i: (b, h)),
            pl.BlockSpec((B_GROUP, 1, B_GEN_LEN), lambda b, h, i: (h, 0, 0)),
        ],
        out_specs=pl.BlockSpec((step_rows, MXU_DIM), lambda b, h, i: (b * nq + i, h)),
        out_shape=jax.ShapeDtypeStruct((t, B_QKV), _BF),
        scratch_shapes=[pltpu.VMEM((B_GROUP, B_TABLES, B_QTILE, B_WINDOW), _F32)],
        compiler_params=pltpu.CompilerParams(
            dimension_semantics=("parallel", "parallel", "arbitrary"),
            vmem_limit_bytes=_vmem_limit(est)),
        name="chunk_attn",
    )(score_bound, bq, bk, bv, bias_gen)


def _bias_generator(rel_bias):
    b = rel_bias.astype(_F32) * LOG2E
    n_far = B_WINDOW - B_MAX_REL + 1
    mid = b[:, 1:CHUNK - 1 + B_MAX_REL][:, ::-1]
    n_neg = B_GEN_LEN - n_far - mid.shape[1]
    heads = b.shape[0]
    gen = jnp.concatenate([
        jnp.broadcast_to(b[:, -1:], (heads, n_far)), mid,
        jnp.broadcast_to(b[:, :1], (heads, n_neg))], axis=1)
    return gen[:, None, :]


def _retention_kernel(q_ref, kt_ref, v_ref, g_ref, cn_ref, o_ref, decay_ref, *, seq):
    blk = RET_BLOCK
    h = pl.program_id(1)
    hf = jnp.full((1, 1), h, jnp.int32).astype(_F32)
    log_gamma = jnp.log(1.0 - jnp.exp2(-5.0 - hf))
    diff = (lax.broadcasted_iota(jnp.int32, (blk, blk), 0)
            - lax.broadcasted_iota(jnp.int32, (blk, blk), 1)).astype(_F32)
    decay_ref[...] = jnp.where(diff >= 0, jnp.exp(log_gamma * jnp.maximum(diff, 0.0)), 0.0)
    pos = lax.broadcasted_iota(jnp.int32, (blk, LANES), 0).astype(_F32)
    xi = jnp.exp(log_gamma * (pos + 1.0))
    zeta = jnp.exp(log_gamma * (blk - 1.0 - pos))
    block_decay = jnp.exp(log_gamma * float(blk))
    odd = (h % 2) == 1
    q_keep = (lax.broadcasted_iota(jnp.int32, (1, LANES), 1) >= D_HEAD) == odd
    k_keep = (lax.broadcasted_iota(jnp.int32, (LANES, 1), 0) >= D_HEAD) == odd
    gain = cn_ref[0]

    def body(j, state):
        rows = pl.ds(pl.multiple_of(j * blk, blk), blk)
        qb = q_ref[rows, :]
        qb = jnp.where(q_keep, qb, jnp.zeros_like(qb))
        kt = kt_ref[j]
        kt = jnp.where(k_keep, kt, jnp.zeros_like(kt))
        vb = v_ref[rows, :]
        sc = jnp.dot(qb, kt, preferred_element_type=_F32) * decay_ref[...]
        inner = jnp.dot(sc.astype(_BF), vb, preferred_element_type=_F32)
        cross = jnp.dot(qb, state.astype(_BF), preferred_element_type=_F32) * xi
        o = inner + cross
        vz = (vb.astype(_F32) * zeta).astype(_BF)
        new_state = state * block_decay + jnp.dot(kt, vz, preferred_element_type=_F32)
        ms = jnp.mean(o * o, axis=-1, keepdims=True)
        y = o * lax.rsqrt(ms + EPS) * gain
        gg = g_ref[rows, :].astype(_F32)
        o_ref[rows, :] = (gg * jax.nn.sigmoid(gg) * y).astype(_BF)
        return new_state

    lax.fori_loop(0, seq // blk, body, jnp.zeros((LANES, C_DV), _F32), unroll=RET_UNROLL)


def _retention(cq, ckt, cv, cg, cnorm, batch, seq):
    blk = RET_BLOCK
    nb = seq // blk
    t = batch * seq
    est = 2 * 5 * seq * LANES * 2 + 6 * blk * blk * 4 + VMEM_TEMPORARIES
    return pl.pallas_call(
        functools.partial(_retention_kernel, seq=seq),
        grid=(batch, C_HEADS),
        in_specs=[
            pl.BlockSpec((seq, LANES), lambda b, h: (b, h // 2)),
            pl.BlockSpec((nb, LANES, blk), lambda b, h: (b, h // 2, 0)),
            pl.BlockSpec((seq, C_DV), lambda b, h: (b, h)),
            pl.BlockSpec((seq, C_DV), lambda b, h: (b, h)),
            pl.BlockSpec((1, 1, C_DV), lambda b, h: (h, 0, 0)),
        ],
        out_specs=pl.BlockSpec((seq, C_DV), lambda b, h: (b, h)),
        out_shape=jax.ShapeDtypeStruct((t, C_V), _BF),
        scratch_shapes=[pltpu.VMEM((blk, blk), _F32)],
        compiler_params=pltpu.CompilerParams(
            dimension_semantics=("parallel", "parallel"), vmem_limit_bytes=_vmem_limit(est)),
        name="retention",
    )(cq, ckt, cv, cg, cnorm)


def _merge_ffn_kernel(x_ref, ya_ref, yb_ref, yc_ref, gate_ref, wa_ref, wb_ref, wc_ref, wo_ref,
                      gain_ref, wg_ref, wu_ref, wd_ref, o_ref, acc_ref):
    merged = None
    for n, (y_ref, w_ref) in enumerate(((ya_ref, wa_ref), (yb_ref, wb_ref), (yc_ref, wc_ref))):
        br = jnp.dot(y_ref[...], w_ref[...], preferred_element_type=_F32)
        term = gate_ref[:, n * D_MODEL:(n + 1) * D_MODEL].astype(_F32) * br
        merged = term if merged is None else merged + term
    x = x_ref[...] + jnp.dot(merged.astype(_BF), wo_ref[...], preferred_element_type=_F32)
    h = _rms_rows(x, gain_ref[...]).astype(_BF)
    for c in range(D_FF // FF_CHUNK):
        sl = slice(c * FF_CHUNK, (c + 1) * FF_CHUNK)
        g = jnp.dot(h, wg_ref[:, sl], preferred_element_type=_F32)
        u = jnp.dot(h, wu_ref[:, sl], preferred_element_type=_F32)
        a = (g * jax.nn.sigmoid(g) * u).astype(_BF)
        part = jnp.dot(a, wd_ref[sl, :], preferred_element_type=_F32)
        if c == 0:
            acc_ref[...] = part
        else:
            acc_ref[...] += part
    o_ref[...] = x + 0.5 * acc_ref[...]


def _merge_ffn(layer, x, ya, yb, yc, gates, wa, wb, wc, wo, gain, wg, wu, wd):
    t = x.shape[0]
    tm = TOKEN_TILE
    row = lambda i: (i, 0)
    const = lambda i: (0, 0)
    est = (5 * D_MODEL * D_MODEL + 3 * D_MODEL * D_FF) * 2 \
        + 2 * tm * (2 * D_MODEL * 4 + (A_V + B_QKV + C_V) * 2 + N_BRANCH * D_MODEL * 2) \
        + 8 * tm * D_MODEL * 4 + 4 * tm * FF_CHUNK * 4 + VMEM_TEMPORARIES
    return pl.pallas_call(
        _merge_ffn_kernel,
        grid=(t // tm,),
        in_specs=[
            pl.BlockSpec((tm, D_MODEL), row),
            pl.BlockSpec((tm, A_V), row),
            pl.BlockSpec((tm, B_QKV), row),
            pl.BlockSpec((tm, C_V), row),
            pl.BlockSpec((tm, N_BRANCH * D_MODEL), row),
            _layer_weight(layer, A_V, D_MODEL),
            _layer_weight(layer, B_QKV, D_MODEL),
            _layer_weight(layer, C_V, D_MODEL),
            _layer_weight(layer, D_MODEL, D_MODEL),
            _resident((1, D_MODEL), const),
            _layer_weight(layer, D_MODEL, D_FF),
            _layer_weight(layer, D_MODEL, D_FF),
            _layer_weight(layer, D_FF, D_MODEL),
        ],
        out_specs=pl.BlockSpec((tm, D_MODEL), row),
        out_shape=jax.ShapeDtypeStruct((t, D_MODEL), _F32),
        scratch_shapes=[pltpu.VMEM((tm, D_MODEL), _F32)],
        compiler_params=pltpu.CompilerParams(
            dimension_semantics=("parallel",), vmem_limit_bytes=_vmem_limit(est)),
        name="merge_ffn",
    )(x, ya, yb, yc, gates, wa, wb, wc, wo, gain, wg, wu, wd)


def _rope_tables(seq):
    half = D_HEAD // 2
    inv = ROPE_THETA ** (-jnp.arange(half, dtype=_F32) / half)
    ang = jnp.arange(seq, dtype=_F32)[:, None] * inv[None, :]
    cos, sin = jnp.cos(ang), jnp.sin(ang)
    cos_t = jnp.concatenate([cos, cos] * (LANES // D_HEAD), axis=1)
    sin_t = jnp.concatenate([-sin, sin] * (LANES // D_HEAD), axis=1)
    return cos_t, sin_t


def _head_mean_matrix():
    g = np.kron(np.eye(MXU_DIM // D_HEAD), np.full((D_HEAD, D_HEAD), 1.0 / D_HEAD))
    return jnp.asarray(g, dtype=_BF)


def kernel(x, ffn1_norm, ffn1_w_gate, ffn1_w_up, ffn1_w_down, mix_norm, w_in, a_q_norm, a_k_norm, a_lambda_q1, a_lambda_k1, a_lambda_q2, a_lambda_k2, a_subln, b_q_norm, b_k_norm, b_rel_bias, c_out_norm, w_branch_a, w_branch_b, w_branch_c, w_out, ffn2_norm, ffn2_w_gate, ffn2_w_up, ffn2_w_down):
    batch, seq, d = x.shape
    assert d == D_MODEL and seq % ATTN_BLOCK == 0 and seq % RET_BLOCK == 0
    assert seq % TOKEN_TILE == 0 and seq >= B_WINDOW and ATTN_BLOCK == TOKEN_TILE == RET_BLOCK
    assert seq % (B_TILES * B_QTILE) == 0
    scale = D_HEAD ** -0.5
    cos_t, sin_t = _rope_tables(seq)
    gmat = _head_mean_matrix()
    bf = lambda w: w.astype(_BF)
    f1g, f1u, f1d = bf(ffn1_w_gate), bf(ffn1_w_up), bf(ffn1_w_down)
    f2g, f2u, f2d = bf(ffn2_w_gate), bf(ffn2_w_up), bf(ffn2_w_down)
    win, wba, wbb, wbc, wo = bf(w_in), bf(w_branch_a), bf(w_branch_b), bf(w_branch_c), bf(w_out)
    xt = x.reshape(batch * seq, D_MODEL)
    for l in range(DEPTH):
        lam_init = 0.8 - 0.6 * math.exp(-0.3 * l)
        xt = _ffn(l, xt, ffn1_norm[l][None, :], f1g, f1u, f1d)

        heads = A_QK // D_HEAD
        qk_gains = jnp.stack([
            jnp.tile(a_q_norm[l].astype(_F32) * (scale * LOG2E), heads),
            jnp.tile(a_k_norm[l].astype(_F32), heads),
            jnp.tile(b_q_norm[l].astype(_F32) * (scale * LOG2E), heads),
            jnp.tile(b_k_norm[l].astype(_F32), heads),
        ])
        aq, ak, avt, bq, bk, bv, cq, ckt, cv, cg, gates = _in_proj(
            l, xt, mix_norm[l][None, :], win, cos_t, sin_t, qk_gains, gmat, seq)

        lamv = jnp.stack([a_lambda_q1[l], a_lambda_k1[l], a_lambda_q2[l], a_lambda_k2[l]]).astype(_F32)
        a_bound = _score_bound(qk_gains[0, :D_HEAD], qk_gains[1, :D_HEAD])
        ya = _diff_attn(a_bound, aq, ak, avt, lamv, a_subln[l].astype(_F32)[:, None], batch, seq, lam_init)
        bias_gen = _bias_generator(b_rel_bias[l])
        b_bound = _score_bound(qk_gains[2, :D_HEAD], qk_gains[3, :D_HEAD]) + jnp.max(jnp.abs(bias_gen))
        yb = _chunk_attn(b_bound, bq, bk, bv, bias_gen, batch, seq)
        yc = _retention(cq, ckt, cv, cg, c_out_norm[l].astype(_F32)[:, None, :], batch, seq)

        xt = _merge_ffn(l, xt, ya, yb, yc, gates, wba, wbb, wbc, wo, ffn2_norm[l][None, :],
                        f2g, f2u, f2d)
    return xt.reshape(batch, seq, D_MODEL)
```

```python
import functools
import math

import numpy as np
import jax
import jax.numpy as jnp
from jax import lax
from jax.experimental import pallas as pl
from jax.experimental.pallas import tpu as pltpu

D_MODEL = 1024
DEPTH = 2
CHUNK = 64
D_HEAD = 64
ROPE_THETA = 10000.0
EPS = 1e-6
A_HEADS = 4
A_DV = 2 * D_HEAD
B_HEADS = 8
B_LOOKBACK = 8
B_MAX_REL = 256
C_HEADS = 4
C_DV = 2 * D_HEAD
N_BRANCH = 3
D_FF = 2816

A_QK = A_HEADS * 2 * D_HEAD
A_V = A_HEADS * A_DV
B_QKV = B_HEADS * D_HEAD
C_QK = C_HEADS * D_HEAD
C_V = C_HEADS * C_DV
SPLITS = (A_QK, A_QK, A_V, B_QKV, B_QKV, B_QKV, C_QK, C_QK, C_V, C_V, N_BRANCH * D_MODEL)
IN_COLS = sum(SPLITS)
_OFFS = tuple(int(v) for v in np.cumsum((0,) + SPLITS))

LANES = 128
MXU_DIM = 256
VMEM_BYTES_V7X = 64 * 1024 * 1024
VMEM_HEADROOM = 6 * 1024 * 1024
VMEM_TEMPORARIES = 12 * 1024 * 1024

TOKEN_TILE = 512
FF_CHUNK = MXU_DIM
PROJ_CHUNK = 512
ATTN_BLOCK = 512
B_QTILE = 256
B_WINDOW = B_QTILE + B_LOOKBACK * CHUNK
RET_BLOCK = 512
RET_SUB = 256
RET_UNROLL = 16
NEG_BIG = -1e30

_BF = jnp.bfloat16
_F32 = jnp.float32
_NT = (((1,), (1,)), ((), ()))


def _vmem_limit(estimate_bytes):
    return int(min(VMEM_BYTES_V7X - VMEM_HEADROOM, max(32 * 1024 * 1024, estimate_bytes)))


def _resident(shape, index_map):
    return pl.BlockSpec(shape, index_map, pipeline_mode=pl.Buffered(1))


def _layer_weight(layer, rows, cols):
    return pl.BlockSpec((None, rows, cols), lambda i: (layer, 0, 0), pipeline_mode=pl.Buffered(1))


def _rms_rows(x, gain):
    ms = jnp.mean(x * x, axis=-1, keepdims=True)
    return x * lax.rsqrt(ms + EPS) * gain


def _ffn_kernel(x_ref, gain_ref, wg_ref, wu_ref, wd_ref, o_ref, acc_ref):
    x = x_ref[...]
    h = _rms_rows(x, gain_ref[...]).astype(_BF)
    for c in range(D_FF // FF_CHUNK):
        sl = slice(c * FF_CHUNK, (c + 1) * FF_CHUNK)
        g = jnp.dot(h, wg_ref[:, sl], preferred_element_type=_F32)
        u = jnp.dot(h, wu_ref[:, sl], preferred_element_type=_F32)
        a = (g * jax.nn.sigmoid(g) * u).astype(_BF)
        part = jnp.dot(a, wd_ref[sl, :], preferred_element_type=_F32)
        if c == 0:
            acc_ref[...] = part
        else:
            acc_ref[...] += part
    o_ref[...] = x + 0.5 * acc_ref[...]


def _ffn(layer, x, gain, wg, wu, wd):
    t = x.shape[0]
    tm = TOKEN_TILE
    est = 3 * D_MODEL * D_FF * 2 + 5 * tm * D_MODEL * 4 + 4 * tm * FF_CHUNK * 4 + VMEM_TEMPORARIES
    return pl.pallas_call(
        _ffn_kernel,
        grid=(t // tm,),
        in_specs=[
            pl.BlockSpec((tm, D_MODEL), lambda i: (i, 0)),
            _resident((1, D_MODEL), lambda i: (0, 0)),
            _layer_weight(layer, D_MODEL, D_FF),
            _layer_weight(layer, D_MODEL, D_FF),
            _layer_weight(layer, D_FF, D_MODEL),
        ],
        out_specs=pl.BlockSpec((tm, D_MODEL), lambda i: (i, 0)),
        out_shape=jax.ShapeDtypeStruct((t, D_MODEL), _F32),
        scratch_shapes=[pltpu.VMEM((tm, D_MODEL), _F32)],
        compiler_params=pltpu.CompilerParams(
            dimension_semantics=("parallel",), vmem_limit_bytes=_vmem_limit(est)),
        name="ffn",
    )(x, gain, wg, wu, wd)


def _swap_halves(x):
    lane = lax.broadcasted_iota(jnp.int32, (1, LANES), 1)
    upper = (lane & (D_HEAD // 2)) != 0
    outs = []
    for s in range(x.shape[1] // LANES):
        xs = x[:, s * LANES:(s + 1) * LANES]
        from_below = pltpu.roll(xs, D_HEAD // 2, 1)
        from_above = pltpu.roll(xs, LANES - D_HEAD // 2, 1)
        outs.append(jnp.where(upper, from_below, from_above))
    return outs[0] if len(outs) == 1 else jnp.concatenate(outs, axis=1)


def _tile_lanes(t, n):
    reps = n // t.shape[1]
    return t if reps == 1 else jnp.concatenate([t] * reps, axis=1)


def _rope(x, cos, sin):
    n = x.shape[1]
    return x * _tile_lanes(cos, n) + _swap_halves(x) * _tile_lanes(sin, n)


def _head_ms(p, gmat):
    sq = (p * p).astype(_BF)
    outs = [jnp.dot(sq[:, s:s + MXU_DIM], gmat, preferred_element_type=_F32)
            for s in range(0, p.shape[1], MXU_DIM)]
    return outs[0] if len(outs) == 1 else jnp.concatenate(outs, axis=1)


def _in_proj_kernel(x_ref, gain_ref, w_ref, cos_ref, sin_ref, qkg_ref, gmat_ref,
                    aq_ref, ak_ref, avt_ref, bq_ref, bk_ref, bv_ref,
                    cq_ref, ckt_ref, cv_ref, cg_ref, gate_ref):
    h = _rms_rows(x_ref[...], gain_ref[...]).astype(_BF)
    cos = cos_ref[...]
    sin = sin_ref[...]
    gmat = gmat_ref[...]

    def proj(lo, width):
        return jnp.dot(h, w_ref[:, lo:lo + width], preferred_element_type=_F32)

    def normed(p, row):
        return p * lax.rsqrt(_head_ms(p, gmat) + EPS) * qkg_ref[row:row + 1, :]

    aq_ref[...] = _rope(normed(proj(_OFFS[0], A_QK), 0), cos, sin).astype(_BF)
    ak_ref[...] = _rope(normed(proj(_OFFS[1], A_QK), 1), cos, sin).astype(_BF)
    avt_ref[0] = proj(_OFFS[2], A_V).T.astype(_BF)
    bq_ref[...] = normed(proj(_OFFS[3], B_QKV), 2).astype(_BF)
    bk_ref[...] = normed(proj(_OFFS[4], B_QKV), 3).astype(_BF)
    bv_ref[...] = proj(_OFFS[5], B_QKV).astype(_BF)
    cq_ref[...] = _rope(proj(_OFFS[6], C_QK), cos, sin).astype(_BF)
    ckt_ref[0] = (_rope(proj(_OFFS[7], C_QK), cos, sin) * (D_HEAD ** -0.5)).T.astype(_BF)
    cv_ref[...] = proj(_OFFS[8], C_V).astype(_BF)
    cg_ref[...] = proj(_OFFS[9], C_V).astype(_BF)
    for c in range(N_BRANCH * D_MODEL // PROJ_CHUNK):
        lo = c * PROJ_CHUNK
        gate_ref[:, lo:lo + PROJ_CHUNK] = jax.nn.sigmoid(
            proj(_OFFS[10] + lo, PROJ_CHUNK)).astype(_BF)


def _in_proj(layer, x, gain, w, cos_t, sin_t, qk_gains, gmat, seq):
    t = x.shape[0]
    tm = TOKEN_TILE
    nt = t // tm
    pos_blocks = seq // tm
    row = lambda i: (i, 0)
    tok_major = lambda n: pl.BlockSpec((tm, n), row)
    out_shapes = (
        jax.ShapeDtypeStruct((t, A_QK), _BF),
        jax.ShapeDtypeStruct((t, A_QK), _BF),
        jax.ShapeDtypeStruct((nt, A_V, tm), _BF),
        jax.ShapeDtypeStruct((t, B_QKV), _BF),
        jax.ShapeDtypeStruct((t, B_QKV), _BF),
        jax.ShapeDtypeStruct((t, B_QKV), _BF),
        jax.ShapeDtypeStruct((t, C_QK), _BF),
        jax.ShapeDtypeStruct((nt, C_QK, tm), _BF),
        jax.ShapeDtypeStruct((t, C_V), _BF),
        jax.ShapeDtypeStruct((t, C_V), _BF),
        jax.ShapeDtypeStruct((t, N_BRANCH * D_MODEL), _BF),
    )
    out_specs = (
        tok_major(A_QK), tok_major(A_QK),
        pl.BlockSpec((1, A_V, tm), lambda i: (i, 0, 0)),
        tok_major(B_QKV), tok_major(B_QKV), tok_major(B_QKV),
        tok_major(C_QK),
        pl.BlockSpec((1, C_QK, tm), lambda i: (i, 0, 0)),
        tok_major(C_V), tok_major(C_V), tok_major(N_BRANCH * D_MODEL),
    )
    est = D_MODEL * IN_COLS * 2 + 2 * tm * IN_COLS * 2 + 4 * tm * D_MODEL * 4 + VMEM_TEMPORARIES
    return pl.pallas_call(
        _in_proj_kernel,
        grid=(nt,),
        in_specs=[
            pl.BlockSpec((tm, D_MODEL), row),
            _resident((1, D_MODEL), lambda i: (0, 0)),
            _layer_weight(layer, D_MODEL, IN_COLS),
            pl.BlockSpec((tm, LANES), lambda i: (i % pos_blocks, 0)),
            pl.BlockSpec((tm, LANES), lambda i: (i % pos_blocks, 0)),
            _resident((4, A_QK), lambda i: (0, 0)),
            _resident((MXU_DIM, MXU_DIM), lambda i: (0, 0)),
        ],
        out_specs=out_specs,
        out_shape=out_shapes,
        compiler_params=pltpu.CompilerParams(
            dimension_semantics=("parallel",), vmem_limit_bytes=_vmem_limit(est)),
        name="in_proj",
    )(x, gain, w, cos_t, sin_t, qk_gains, gmat)


LOG2E = math.log2(math.e)
SAFE_LOG2_SCORE = 60.0
NORM_MARGIN = 1.05
A_PAIR = 4


def _diff_attn_kernel(bound_ref, q_ref, k_ref, vt_ref, lamv_ref, subln_ref, o_ref, acc_ref, *,
                      lam_init):
    blk = ATTN_BLOCK
    i = pl.program_id(2)
    n_chain = 2 * A_PAIR
    per_group = MXU_DIM // D_HEAD
    lane = lax.broadcasted_iota(jnp.int32, (1, MXU_DIM), 1)
    qs = []
    for n in range(n_chain):
        g, r = divmod(n, per_group)
        qg = q_ref[:, g * MXU_DIM:(g + 1) * MXU_DIM]
        qs.append(jnp.where((lane >= D_HEAD * r) & (lane < D_HEAD * (r + 1)), qg,
                            jnp.zeros_like(qg)))
    half = blk // 2

    def scores(j, n, diagonal):
        g = n // per_group
        kb = k_ref[pl.ds(pl.multiple_of(j * blk, blk), blk), g * MXU_DIM:(g + 1) * MXU_DIM]
        st = lax.dot_general(kb, qs[n], _NT, preferred_element_type=_F32)
        if diagonal:
            key_chunk = lax.broadcasted_iota(jnp.int32, (blk, blk), 0) // CHUNK
            qry_chunk = lax.broadcasted_iota(jnp.int32, (blk, blk), 1) // CHUNK
            st = jnp.where(key_chunk <= qry_chunk, st, NEG_BIG)
        return st

    def diagonal_scores(j, n):
        g = n // per_group
        row0 = pl.multiple_of(j * blk, blk)
        cols = slice(g * MXU_DIM, (g + 1) * MXU_DIM)
        key_chunk = lax.broadcasted_iota(jnp.int32, (half, blk), 0) // CHUNK
        qry_chunk = lax.broadcasted_iota(jnp.int32, (half, blk), 1) // CHUNK
        first = lax.dot_general(k_ref[pl.ds(row0, half), cols], qs[n], _NT,
                                preferred_element_type=_F32)
        first = jnp.where(key_chunk <= qry_chunk, first, NEG_BIG)
        second = lax.dot_general(k_ref[pl.ds(row0 + half, half), cols], qs[n][half:, :], _NT,
                                 preferred_element_type=_F32)
        key_chunk2 = lax.broadcasted_iota(jnp.int32, (half, half), 0) // CHUNK
        qry_chunk2 = lax.broadcasted_iota(jnp.int32, (half, half), 1) // CHUNK
        second = jnp.where(key_chunk2 <= qry_chunk2, second, NEG_BIG)
        return first, second

    def values_t(j, n):
        head = n // 2
        return vt_ref[j, head * A_DV:(head + 1) * A_DV, :]

    def plain_steps(blocks, carry):
        out = list(carry)
        work = [(j, diagonal, list(range(g * per_group, (g + 1) * per_group)))
                for j, diagonal in blocks for g in range(n_chain // per_group)]

        def issue(j, diagonal, chains):
            return {n: (diagonal_scores(j, n) if diagonal else scores(j, n, False))
                    for n in chains}

        pending = issue(*work[0])
        for idx, (j, diagonal, chains) in enumerate(work):
            sts = pending
            if idx + 1 < len(work):
                pending = issue(*work[idx + 1])
            for n in chains:
                if diagonal:
                    first, second = (jnp.exp2(s) for s in sts[n])
                    widened = jnp.concatenate([jnp.zeros_like(second), second], axis=1)
                    out[n] = (out[n] + jnp.sum(first, axis=0, keepdims=True)
                              + jnp.sum(widened, axis=0, keepdims=True))
                    vt = values_t(j, n)
                    acc_ref[n] += jnp.dot(vt[:, :half], first.astype(_BF),
                                          preferred_element_type=_F32)
                    acc_ref[n, :, half:] += jnp.dot(vt[:, half:], second.astype(_BF),
                                                    preferred_element_type=_F32)
                else:
                    p = jnp.exp2(sts[n])
                    out[n] = out[n] + jnp.sum(p, axis=0, keepdims=True)
                    acc_ref[n] += jnp.dot(values_t(j, n), p.astype(_BF),
                                          preferred_element_type=_F32)
        return tuple(out)

    def shifted_step(j, carry, diagonal):
        out = []
        for n in range(n_chain):
            m_old, l_old = carry[n]
            st = scores(j, n, diagonal)
            m_new = jnp.maximum(m_old, jnp.max(st, axis=0, keepdims=True))
            alpha = jnp.exp2(m_old - m_new)
            p = jnp.exp2(st - m_new)
            l_new = alpha * l_old + jnp.sum(p, axis=0, keepdims=True)
            pv = jnp.dot(values_t(j, n), p.astype(_BF), preferred_element_type=_F32)
            acc_ref[n] = alpha * acc_ref[n] + pv
            out.append((m_new, l_new))
        return tuple(out)

    def finish(ls):
        lamv = lamv_ref[...]
        lam = (jnp.exp(jnp.sum(lamv[0:1] * lamv[1:2], axis=-1, keepdims=True))
               - jnp.exp(jnp.sum(lamv[2:3] * lamv[3:4], axis=-1, keepdims=True)) + lam_init)
        for head in range(A_PAIR):
            o = (acc_ref[2 * head] / ls[2 * head]
                 - lam * (acc_ref[2 * head + 1] / ls[2 * head + 1]))
            ms = jnp.mean(o * o, axis=0, keepdims=True)
            y = o * lax.rsqrt(ms + EPS) * subln_ref[...] * (1.0 - lam_init)
            o_ref[:, head * A_DV:(head + 1) * A_DV] = y.T.astype(_BF)

    acc_ref[...] = jnp.zeros_like(acc_ref)
    bounded = bound_ref[0] <= SAFE_LOG2_SCORE

    @pl.when(bounded)
    def _():
        init = tuple(jnp.zeros((1, blk), _F32) for _ in range(n_chain))
        carry = lax.fori_loop(
            0, i // 2, lambda jj, c: plain_steps([(2 * jj, False), (2 * jj + 1, False)], c), init)
        odd = (i % 2) == 1
        pl.when(odd)(lambda: finish(plain_steps([(i - 1, False), (i, True)], carry)))
        pl.when(jnp.logical_not(odd))(lambda: finish(plain_steps([(i, True)], carry)))

    @pl.when(jnp.logical_not(bounded))
    def _():
        init = tuple((jnp.full((1, blk), NEG_BIG, _F32), jnp.zeros((1, blk), _F32))
                     for _ in range(n_chain))
        carry = lax.fori_loop(0, i, lambda j, c: shifted_step(j, c, False), init)
        finish([l for _, l in shifted_step(i, carry, True)])


def _diff_attn(score_bound, aq, ak, avt, lamv, subln_col, batch, seq, lam_init):
    blk = ATTN_BLOCK
    nq = seq // blk
    t = batch * seq
    width = A_PAIR * A_DV
    est = 2 * (seq * width * 2) * 2 + 2 * A_PAIR * (A_DV * blk * 4) + 16 * blk * blk * 4 \
        + VMEM_TEMPORARIES
    return pl.pallas_call(
        functools.partial(_diff_attn_kernel, lam_init=lam_init),
        grid=(batch, A_HEADS // A_PAIR, nq),
        in_specs=[
            pl.BlockSpec(memory_space=pltpu.SMEM),
            pl.BlockSpec((blk, width), lambda b, h, i: (b * nq + i, h)),
            pl.BlockSpec((seq, width), lambda b, h, i: (b, h)),
            pl.BlockSpec((nq, width, blk), lambda b, h, i: (b, h, 0)),
            pl.BlockSpec((4, D_HEAD), lambda b, h, i: (0, 0)),
            pl.BlockSpec((A_DV, 1), lambda b, h, i: (0, 0)),
        ],
        out_specs=pl.BlockSpec((blk, width), lambda b, h, i: (b * nq + i, h)),
        out_shape=jax.ShapeDtypeStruct((t, A_V), _BF),
        scratch_shapes=[pltpu.VMEM((2 * A_PAIR, A_DV, blk), _F32)],
        compiler_params=pltpu.CompilerParams(
            dimension_semantics=("parallel", "parallel", "parallel"),
            vmem_limit_bytes=_vmem_limit(est)),
        name="diff_attn",
    )(score_bound, aq, ak, avt, lamv, subln_col)


def _score_bound(q_gain, k_gain):
    return (D_HEAD * NORM_MARGIN * jnp.max(jnp.abs(q_gain)) * jnp.max(jnp.abs(k_gain))).reshape(1)


B_TABLES = (B_WINDOW - B_QTILE) // B_QTILE + 1
B_GEN_LEN = 2048
B_GROUP = MXU_DIM // D_HEAD


def _build_bias_tables(gen_ref, mb_ref):
    qchunk = lax.broadcasted_iota(jnp.int32, (B_QTILE, B_WINDOW), 0) // CHUNK
    kchunk = lax.broadcasted_iota(jnp.int32, (B_QTILE, B_WINDOW), 1) // CHUNK
    for hh in range(B_GROUP):
        rolled = pltpu.roll(jnp.broadcast_to(gen_ref[hh], (B_QTILE, B_GEN_LEN)), 0, 1,
                            stride=1, stride_axis=0)
        for t in range(B_TABLES):
            lo = B_WINDOW - t * B_QTILE
            dchunk = qchunk - kchunk + (t * B_QTILE) // CHUNK
            visible = (dchunk >= 0) & (dchunk <= B_LOOKBACK)
            mb_ref[hh, t] = jnp.where(visible, rolled[:, lo:lo + B_WINDOW], NEG_BIG)


B_TILES = 4


def _chunk_attn_kernel(bound_ref, q_ref, k_ref, v_ref, gen_ref, o_ref, mb_ref):
    i = pl.program_id(2)

    @pl.when(i == 0)
    def _():
        _build_bias_tables(gen_ref, mb_ref)

    lane = lax.broadcasted_iota(jnp.int32, (1, MXU_DIM), 1)
    in_head = [(lane >= D_HEAD * hh) & (lane < D_HEAD * (hh + 1)) for hh in range(B_GROUP)]
    tiles = []
    for tt in range(B_TILES):
        tile = i * B_TILES + tt
        start = pl.multiple_of(jnp.maximum(tile * B_QTILE - (B_WINDOW - B_QTILE), 0), B_QTILE)
        tiles.append(dict(
            rows=slice(tt * B_QTILE, (tt + 1) * B_QTILE),
            table=jnp.minimum(tile, B_TABLES - 1),
            kw=k_ref[pl.ds(start, B_WINDOW), :],
            vw=v_ref[pl.ds(start, B_WINDOW), :]))

    def logits(tt, hh):
        q = q_ref[tiles[tt]["rows"], :]
        s = lax.dot_general(jnp.where(in_head[hh], q, jnp.zeros_like(q)), tiles[tt]["kw"], _NT,
                            preferred_element_type=_F32)
        return s + mb_ref[hh, tiles[tt]["table"]]

    def attend(shifted):
        work = [(tt, hh) for tt in range(B_TILES) for hh in range(B_GROUP)]
        outs = [None] * B_TILES
        s = logits(*work[0])
        for idx, (tt, hh) in enumerate(work):
            s_next = logits(*work[idx + 1]) if idx + 1 < len(work) else None
            if shifted:
                s = s - jnp.max(s, axis=-1, keepdims=True)
            p = jnp.exp2(s)
            l = jnp.sum(p, axis=-1, keepdims=True)
            o = jnp.dot(p.astype(_BF), tiles[tt]["vw"], preferred_element_type=_F32) / l
            outs[tt] = o if outs[tt] is None else jnp.where(in_head[hh], o, outs[tt])
            s = s_next
        for tt in range(B_TILES):
            o_ref[tiles[tt]["rows"], :] = outs[tt].astype(_BF)

    bounded = bound_ref[0] <= SAFE_LOG2_SCORE
    pl.when(bounded)(lambda: attend(False))
    pl.when(jnp.logical_not(bounded))(lambda: attend(True))


def _chunk_attn(score_bound, bq, bk, bv, bias_gen, batch, seq):
    step_rows = B_TILES * B_QTILE
    nq = seq // step_rows
    t = batch * seq
    est = B_GROUP * B_TABLES * B_QTILE * B_WINDOW * 4 + 4 * B_QTILE * B_GEN_LEN * 4 \
        + 4 * seq * MXU_DIM * 2 + VMEM_TEMPORARIES
    return pl.pallas_call(
        _chunk_attn_kernel,
        grid=(batch, B_HEADS // B_GROUP, nq),
        in_specs=[
            pl.BlockSpec(memory_space=pltpu.SMEM),
            pl.BlockSpec((step_rows, MXU_DIM), lambda b, h, i: (b * nq + i, h)),
            pl.BlockSpec((seq, MXU_DIM), lambda b, h, i: (b, h)),
            pl.BlockSpec((seq, MXU_DIM), lambda b, h, i: (b, h)),
            pl.BlockSpec((B_GROUP, 1, B_GEN_LEN), lambda b, h, i: (h, 0, 0)),
        ],
        out_specs=pl.BlockSpec((step_rows, MXU_DIM), lambda b, h, i: (b * nq + i, h)),
        out_shape=jax.ShapeDtypeStruct((t, B_QKV), _BF),
        scratch_shapes=[pltpu.VMEM((B_GROUP, B_TABLES, B_QTILE, B_WINDOW), _F32)],
        compiler_params=pltpu.CompilerParams(
            dimension_semantics=("parallel", "parallel", "arbitrary"),
            vmem_limit_bytes=_vmem_limit(est)),
        name="chunk_attn",
    )(score_bound, bq, bk, bv, bias_gen)


def _bias_generator(rel_bias):
    b = rel_bias.astype(_F32) * LOG2E
    n_far = B_WINDOW - B_MAX_REL + 1
    mid = b[:, 1:CHUNK - 1 + B_MAX_REL][:, ::-1]
    n_neg = B_GEN_LEN - n_far - mid.shape[1]
    heads = b.shape[0]
    gen = jnp.concatenate([
        jnp.broadcast_to(b[:, -1:], (heads, n_far)), mid,
        jnp.broadcast_to(b[:, :1], (heads, n_neg))], axis=1)
    return gen[:, None, :]


def _retention_kernel(q_ref, kt_ref, v_ref, g_ref, cn_ref, o_ref, decay_ref, *, seq):
    blk = RET_BLOCK
    sub = RET_SUB
    h = pl.program_id(1)
    hf = jnp.full((1, 1), h, jnp.int32).astype(_F32)
    log_gamma = jnp.log(1.0 - jnp.exp2(-5.0 - hf))
    diff = (lax.broadcasted_iota(jnp.int32, (sub, sub), 0)
            - lax.broadcasted_iota(jnp.int32, (sub, sub), 1)).astype(_F32)
    decay_ref[...] = jnp.where(diff >= 0, jnp.exp(log_gamma * jnp.maximum(diff, 0.0)), 0.0)
    pos = lax.broadcasted_iota(jnp.int32, (sub, LANES), 0).astype(_F32)
    xi = jnp.exp(log_gamma * (pos + 1.0))
    zeta = jnp.exp(log_gamma * (sub - 1.0 - pos))
    sub_decay = jnp.exp(log_gamma * float(sub))
    odd = (h % 2) == 1
    q_keep = (lax.broadcasted_iota(jnp.int32, (1, LANES), 1) >= D_HEAD) == odd
    k_keep = (lax.broadcasted_iota(jnp.int32, (LANES, 1), 0) >= D_HEAD) == odd
    gain = cn_ref[0]

    def body(j, state):
        for s in range(blk // sub):
            rows = pl.ds(pl.multiple_of(j * blk + s * sub, sub), sub)
            qb = q_ref[rows, :]
            qb = jnp.where(q_keep, qb, jnp.zeros_like(qb))
            kt = kt_ref[j, :, s * sub:(s + 1) * sub]
            kt = jnp.where(k_keep, kt, jnp.zeros_like(kt))
            vb = v_ref[rows, :]
            sc = jnp.dot(qb, kt, preferred_element_type=_F32) * decay_ref[...]
            inner = jnp.dot(sc.astype(_BF), vb, preferred_element_type=_F32)
            cross = jnp.dot(qb, state.astype(_BF), preferred_element_type=_F32) * xi
            o = inner + cross
            vz = (vb.astype(_F32) * zeta).astype(_BF)
            state = state * sub_decay + jnp.dot(kt, vz, preferred_element_type=_F32)
            ms = jnp.mean(o * o, axis=-1, keepdims=True)
            y = o * lax.rsqrt(ms + EPS) * gain
            gg = g_ref[rows, :].astype(_F32)
            o_ref[rows, :] = (gg * jax.nn.sigmoid(gg) * y).astype(_BF)
        return state

    lax.fori_loop(0, seq // blk, body, jnp.zeros((LANES, C_DV), _F32), unroll=RET_UNROLL)


def _retention(cq, ckt, cv, cg, cnorm, batch, seq):
    blk = RET_BLOCK
    nb = seq // blk
    t = batch * seq
    est = 2 * 5 * seq * LANES * 2 + 6 * blk * blk * 4 + VMEM_TEMPORARIES
    return pl.pallas_call(
        functools.partial(_retention_kernel, seq=seq),
        grid=(batch, C_HEADS),
        in_specs=[
            pl.BlockSpec((seq, LANES), lambda b, h: (b, h // 2)),
            pl.BlockSpec((nb, LANES, blk), lambda b, h: (b, h // 2, 0)),
            pl.BlockSpec((seq, C_DV), lambda b, h: (b, h)),
            pl.BlockSpec((seq, C_DV), lambda b, h: (b, h)),
            pl.BlockSpec((1, 1, C_DV), lambda b, h: (h, 0, 0)),
        ],
        out_specs=pl.BlockSpec((seq, C_DV), lambda b, h: (b, h)),
        out_shape=jax.ShapeDtypeStruct((t, C_V), _BF),
        scratch_shapes=[pltpu.VMEM((RET_SUB, RET_SUB), _F32)],
        compiler_params=pltpu.CompilerParams(
            dimension_semantics=("parallel", "parallel"), vmem_limit_bytes=_vmem_limit(est)),
        name="retention",
    )(cq, ckt, cv, cg, cnorm)


def _merge_ffn_kernel(x_ref, ya_ref, yb_ref, yc_ref, gate_ref, wa_ref, wb_ref, wc_ref, wo_ref,
                      gain_ref, wg_ref, wu_ref, wd_ref, o_ref, acc_ref):
    merged = None
    for n, (y_ref, w_ref) in enumerate(((ya_ref, wa_ref), (yb_ref, wb_ref), (yc_ref, wc_ref))):
        br = jnp.dot(y_ref[...], w_ref[...], preferred_element_type=_F32)
        term = gate_ref[:, n * D_MODEL:(n + 1) * D_MODEL].astype(_F32) * br
        merged = term if merged is None else merged + term
    x = x_ref[...] + jnp.dot(merged.astype(_BF), wo_ref[...], preferred_element_type=_F32)
    h = _rms_rows(x, gain_ref[...]).astype(_BF)
    for c in range(D_FF // FF_CHUNK):
        sl = slice(c * FF_CHUNK, (c + 1) * FF_CHUNK)
        g = jnp.dot(h, wg_ref[:, sl], preferred_element_type=_F32)
        u = jnp.dot(h, wu_ref[:, sl], preferred_element_type=_F32)
        a = (g * jax.nn.sigmoid(g) * u).astype(_BF)
        part = jnp.dot(a, wd_ref[sl, :], preferred_element_type=_F32)
        if c == 0:
            acc_ref[...] = part
        else:
            acc_ref[...] += part
    o_ref[...] = x + 0.5 * acc_ref[...]


def _merge_ffn(layer, x, ya, yb, yc, gates, wa, wb, wc, wo, gain, wg, wu, wd):
    t = x.shape[0]
    tm = TOKEN_TILE
    row = lambda i: (i, 0)
    const = lambda i: (0, 0)
    est = (5 * D_MODEL * D_MODEL + 3 * D_MODEL * D_FF) * 2 \
        + 2 * tm * (2 * D_MODEL * 4 + (A_V + B_QKV + C_V) * 2 + N_BRANCH * D_MODEL * 2) \
        + 8 * tm * D_MODEL * 4 + 4 * tm * FF_CHUNK * 4 + VMEM_TEMPORARIES
    return pl.pallas_call(
        _merge_ffn_kernel,
        grid=(t // tm,),
        in_specs=[
            pl.BlockSpec((tm, D_MODEL), row),
            pl.BlockSpec((tm, A_V), row),
            pl.BlockSpec((tm, B_QKV), row),
            pl.BlockSpec((tm, C_V), row),
            pl.BlockSpec((tm, N_BRANCH * D_MODEL), row),
            _layer_weight(layer, A_V, D_MODEL),
            _layer_weight(layer, B_QKV, D_MODEL),
            _layer_weight(layer, C_V, D_MODEL),
            _layer_weight(layer, D_MODEL, D_MODEL),
            _resident((1, D_MODEL), const),
            _layer_weight(layer, D_MODEL, D_FF),
            _layer_weight(layer, D_MODEL, D_FF),
            _layer_weight(layer, D_FF, D_MODEL),
        ],
        out_specs=pl.BlockSpec((tm, D_MODEL), row),
        out_shape=jax.ShapeDtypeStruct((t, D_MODEL), _F32),
        scratch_shapes=[pltpu.VMEM((tm, D_MODEL), _F32)],
        compiler_params=pltpu.CompilerParams(
            dimension_semantics=("parallel",), vmem_limit_bytes=_vmem_limit(est)),
        name="merge_ffn",
    )(x, ya, yb, yc, gates, wa, wb, wc, wo, gain, wg, wu, wd)


def _rope_tables(seq):
    half = D_HEAD // 2
    inv = ROPE_THETA ** (-jnp.arange(half, dtype=_F32) / half)
    ang = jnp.arange(seq, dtype=_F32)[:, None] * inv[None, :]
    cos, sin = jnp.cos(ang), jnp.sin(ang)
    cos_t = jnp.concatenate([cos, cos] * (LANES // D_HEAD), axis=1)
    sin_t = jnp.concatenate([-sin, sin] * (LANES // D_HEAD), axis=1)
    return cos_t, sin_t


def _head_mean_matrix():
    g = np.kron(np.eye(MXU_DIM // D_HEAD), np.full((D_HEAD, D_HEAD), 1.0 / D_HEAD))
    return jnp.asarray(g, dtype=_BF)


def kernel(x, ffn1_norm, ffn1_w_gate, ffn1_w_up, ffn1_w_down, mix_norm, w_in, a_q_norm, a_k_norm, a_lambda_q1, a_lambda_k1, a_lambda_q2, a_lambda_k2, a_subln, b_q_norm, b_k_norm, b_rel_bias, c_out_norm, w_branch_a, w_branch_b, w_branch_c, w_out, ffn2_norm, ffn2_w_gate, ffn2_w_up, ffn2_w_down):
    batch, seq, d = x.shape
    assert d == D_MODEL and seq % ATTN_BLOCK == 0 and seq % RET_BLOCK == 0
    assert seq % TOKEN_TILE == 0 and seq >= B_WINDOW and ATTN_BLOCK == TOKEN_TILE == RET_BLOCK
    assert seq % (B_TILES * B_QTILE) == 0
    scale = D_HEAD ** -0.5
    cos_t, sin_t = _rope_tables(seq)
    gmat = _head_mean_matrix()
    bf = lambda w: w.astype(_BF)
    f1g, f1u, f1d = bf(ffn1_w_gate), bf(ffn1_w_up), bf(ffn1_w_down)
    f2g, f2u, f2d = bf(ffn2_w_gate), bf(ffn2_w_up), bf(ffn2_w_down)
    win, wba, wbb, wbc, wo = bf(w_in), bf(w_branch_a), bf(w_branch_b), bf(w_branch_c), bf(w_out)
    xt = x.reshape(batch * seq, D_MODEL)
    for l in range(DEPTH):
        lam_init = 0.8 - 0.6 * math.exp(-0.3 * l)
        xt = _ffn(l, xt, ffn1_norm[l][None, :], f1g, f1u, f1d)

        heads = A_QK // D_HEAD
        qk_gains = jnp.stack([
            jnp.tile(a_q_norm[l].astype(_F32) * (scale * LOG2E), heads),
            jnp.tile(a_k_norm[l].astype(_F32), heads),
            jnp.tile(b_q_norm[l].astype(_F32) * (scale * LOG2E), heads),
            jnp.tile(b_k_norm[l].astype(_F32), heads),
        ])
        aq, ak, avt, bq, bk, bv, cq, ckt, cv, cg, gates = _in_proj(
            l, xt, mix_norm[l][None, :], win, cos_t, sin_t, qk_gains, gmat, seq)

        lamv = jnp.stack([a_lambda_q1[l], a_lambda_k1[l], a_lambda_q2[l], a_lambda_k2[l]]).astype(_F32)
        a_bound = _score_bound(qk_gains[0, :D_HEAD], qk_gains[1, :D_HEAD])
        ya = _diff_attn(a_bound, aq, ak, avt, lamv, a_subln[l].astype(_F32)[:, None], batch, seq, lam_init)
        bias_gen = _bias_generator(b_rel_bias[l])
        b_bound = _score_bound(qk_gains[2, :D_HEAD], qk_gains[3, :D_HEAD]) + jnp.max(jnp.abs(bias_gen))
        yb = _chunk_attn(b_bound, bq, bk, bv, bias_gen, batch, seq)
        yc = _retention(cq, ckt, cv, cg, c_out_norm[l].astype(_F32)[:, None, :], batch, seq)

        xt = _merge_ffn(l, xt, ya, yb, yc, gates, wba, wbb, wbc, wo, ffn2_norm[l][None, :],
                        f2g, f2u, f2d)
    return xt.reshape(batch, seq, D_MODEL)
```

```python
import functools
import math

import numpy as np
import jax
import jax.numpy as jnp
from jax import lax
from jax.experimental import pallas as pl
from jax.experimental.pallas import tpu as pltpu

D_MODEL = 1024
DEPTH = 2
CHUNK = 64
D_HEAD = 64
ROPE_THETA = 10000.0
EPS = 1e-6
A_HEADS = 4
A_DV = 2 * D_HEAD
B_HEADS = 8
B_LOOKBACK = 8
B_MAX_REL = 256
C_HEADS = 4
C_DV = 2 * D_HEAD
N_BRANCH = 3
D_FF = 2816

A_QK = A_HEADS * 2 * D_HEAD
A_V = A_HEADS * A_DV
B_QKV = B_HEADS * D_HEAD
C_QK = C_HEADS * D_HEAD
C_V = C_HEADS * C_DV
SPLITS = (A_QK, A_QK, A_V, B_QKV, B_QKV, B_QKV, C_QK, C_QK, C_V, C_V, N_BRANCH * D_MODEL)
IN_COLS = sum(SPLITS)
_OFFS = tuple(int(v) for v in np.cumsum((0,) + SPLITS))

LANES = 128
MXU_DIM = 256
VMEM_BYTES_V7X = 64 * 1024 * 1024
VMEM_HEADROOM = 6 * 1024 * 1024
VMEM_DEFAULT_LIMIT = 32 * 1024 * 1024
VMEM_TEMPORARIES = 12 * 1024 * 1024

TOKEN_TILE = 512
FF_CHUNK = MXU_DIM
PROJ_CHUNK = 512
ATTN_BLOCK = 512
B_QTILE = 256
B_WINDOW = B_QTILE + B_LOOKBACK * CHUNK
RET_BLOCK = 512
RET_SUB = 256
RET_UNROLL = 16
NEG_BIG = -1e30

_BF = jnp.bfloat16
_F32 = jnp.float32
_NT = (((1,), (1,)), ((), ()))


def _vmem_limit(estimate_bytes):
    return int(min(VMEM_BYTES_V7X - VMEM_HEADROOM, max(VMEM_DEFAULT_LIMIT, estimate_bytes)))


def _resident(shape, index_map):
    return pl.BlockSpec(shape, index_map, pipeline_mode=pl.Buffered(1))


def _layer_weight(layer, rows, cols):
    return pl.BlockSpec((None, rows, cols), lambda i: (layer, 0, 0), pipeline_mode=pl.Buffered(1))


def _rms_rows(x, gain):
    ms = jnp.mean(x * x, axis=-1, keepdims=True)
    return x * lax.rsqrt(ms + EPS) * gain


def _ffn_kernel(x_ref, gain_ref, wg_ref, wu_ref, wd_ref, o_ref, acc_ref):
    x = x_ref[...]
    h = _rms_rows(x, gain_ref[...]).astype(_BF)
    for c in range(D_FF // FF_CHUNK):
        sl = slice(c * FF_CHUNK, (c + 1) * FF_CHUNK)
        g = jnp.dot(h, wg_ref[:, sl], preferred_element_type=_F32)
        u = jnp.dot(h, wu_ref[:, sl], preferred_element_type=_F32)
        a = (g * jax.nn.sigmoid(g) * u).astype(_BF)
        part = jnp.dot(a, wd_ref[sl, :], preferred_element_type=_F32)
        if c == 0:
            acc_ref[...] = part
        else:
            acc_ref[...] += part
    o_ref[...] = x + 0.5 * acc_ref[...]


def _ffn(layer, x, gain, wg, wu, wd):
    t = x.shape[0]
    tm = TOKEN_TILE
    est = 3 * D_MODEL * D_FF * 2 + 5 * tm * D_MODEL * 4 + 4 * tm * FF_CHUNK * 4 + VMEM_TEMPORARIES
    return pl.pallas_call(
        _ffn_kernel,
        grid=(t // tm,),
        in_specs=[
            pl.BlockSpec((tm, D_MODEL), lambda i: (i, 0)),
            _resident((1, D_MODEL), lambda i: (0, 0)),
            _layer_weight(layer, D_MODEL, D_FF),
            _layer_weight(layer, D_MODEL, D_FF),
            _layer_weight(layer, D_FF, D_MODEL),
        ],
        out_specs=pl.BlockSpec((tm, D_MODEL), lambda i: (i, 0)),
        out_shape=jax.ShapeDtypeStruct((t, D_MODEL), _F32),
        scratch_shapes=[pltpu.VMEM((tm, D_MODEL), _F32)],
        compiler_params=pltpu.CompilerParams(
            dimension_semantics=("parallel",), vmem_limit_bytes=_vmem_limit(est)),
        name="ffn",
    )(x, gain, wg, wu, wd)


def _swap_halves(x):
    lane = lax.broadcasted_iota(jnp.int32, (1, LANES), 1)
    upper = (lane & (D_HEAD // 2)) != 0
    outs = []
    for s in range(x.shape[1] // LANES):
        xs = x[:, s * LANES:(s + 1) * LANES]
        from_below = pltpu.roll(xs, D_HEAD // 2, 1)
        from_above = pltpu.roll(xs, LANES - D_HEAD // 2, 1)
        outs.append(jnp.where(upper, from_below, from_above))
    return outs[0] if len(outs) == 1 else jnp.concatenate(outs, axis=1)


def _tile_lanes(t, n):
    reps = n // t.shape[1]
    return t if reps == 1 else jnp.concatenate([t] * reps, axis=1)


def _rope(x, cos, sin):
    n = x.shape[1]
    return x * _tile_lanes(cos, n) + _swap_halves(x) * _tile_lanes(sin, n)


def _head_ms(p, gmat):
    sq = (p * p).astype(_BF)
    outs = [jnp.dot(sq[:, s:s + MXU_DIM], gmat, preferred_element_type=_F32)
            for s in range(0, p.shape[1], MXU_DIM)]
    return outs[0] if len(outs) == 1 else jnp.concatenate(outs, axis=1)


def _in_proj_kernel(x_ref, gain_ref, w_ref, cos_ref, sin_ref, qkg_ref, gmat_ref,
                    aq_ref, ak_ref, avt_ref, bq_ref, bk_ref, bv_ref,
                    cq_ref, ckt_ref, cv_ref, cg_ref, gate_ref):
    h = _rms_rows(x_ref[...], gain_ref[...]).astype(_BF)
    cos = cos_ref[...]
    sin = sin_ref[...]
    gmat = gmat_ref[...]

    def proj(lo, width):
        return jnp.dot(h, w_ref[:, lo:lo + width], preferred_element_type=_F32)

    def normed(p, row):
        return p * lax.rsqrt(_head_ms(p, gmat) + EPS) * qkg_ref[row:row + 1, :]

    for c in range(N_BRANCH * D_MODEL // PROJ_CHUNK):
        lo = c * PROJ_CHUNK
        gate_ref[:, lo:lo + PROJ_CHUNK] = jax.nn.sigmoid(
            proj(_OFFS[10] + lo, PROJ_CHUNK)).astype(_BF)
    aq_ref[...] = _rope(normed(proj(_OFFS[0], A_QK), 0), cos, sin).astype(_BF)
    ak_ref[...] = _rope(normed(proj(_OFFS[1], A_QK), 1), cos, sin).astype(_BF)
    bq_ref[...] = normed(proj(_OFFS[3], B_QKV), 2).astype(_BF)
    bk_ref[...] = normed(proj(_OFFS[4], B_QKV), 3).astype(_BF)
    cq_ref[...] = _rope(proj(_OFFS[6], C_QK), cos, sin).astype(_BF)
    ckt_ref[0] = (_rope(proj(_OFFS[7], C_QK), cos, sin) * (D_HEAD ** -0.5)).T.astype(_BF)
    avt_ref[0] = proj(_OFFS[2], A_V).T.astype(_BF)
    bv_ref[...] = proj(_OFFS[5], B_QKV).astype(_BF)
    cv_ref[...] = proj(_OFFS[8], C_V).astype(_BF)
    cg_ref[...] = proj(_OFFS[9], C_V).astype(_BF)


def _in_proj(layer, x, gain, w, cos_t, sin_t, qk_gains, gmat, seq):
    t = x.shape[0]
    tm = TOKEN_TILE
    nt = t // tm
    pos_blocks = seq // tm
    row = lambda i: (i, 0)
    tok_major = lambda n: pl.BlockSpec((tm, n), row)
    out_shapes = (
        jax.ShapeDtypeStruct((t, A_QK), _BF),
        jax.ShapeDtypeStruct((t, A_QK), _BF),
        jax.ShapeDtypeStruct((nt, A_V, tm), _BF),
        jax.ShapeDtypeStruct((t, B_QKV), _BF),
        jax.ShapeDtypeStruct((t, B_QKV), _BF),
        jax.ShapeDtypeStruct((t, B_QKV), _BF),
        jax.ShapeDtypeStruct((t, C_QK), _BF),
        jax.ShapeDtypeStruct((nt, C_QK, tm), _BF),
        jax.ShapeDtypeStruct((t, C_V), _BF),
        jax.ShapeDtypeStruct((t, C_V), _BF),
        jax.ShapeDtypeStruct((t, N_BRANCH * D_MODEL), _BF),
    )
    out_specs = (
        tok_major(A_QK), tok_major(A_QK),
        pl.BlockSpec((1, A_V, tm), lambda i: (i, 0, 0)),
        tok_major(B_QKV), tok_major(B_QKV), tok_major(B_QKV),
        tok_major(C_QK),
        pl.BlockSpec((1, C_QK, tm), lambda i: (i, 0, 0)),
        tok_major(C_V), tok_major(C_V), tok_major(N_BRANCH * D_MODEL),
    )
    est = D_MODEL * IN_COLS * 2 + 2 * tm * IN_COLS * 2 + 4 * tm * D_MODEL * 4 + VMEM_TEMPORARIES
    return pl.pallas_call(
        _in_proj_kernel,
        grid=(nt,),
        in_specs=[
            pl.BlockSpec((tm, D_MODEL), row),
            _resident((1, D_MODEL), lambda i: (0, 0)),
            _layer_weight(layer, D_MODEL, IN_COLS),
            pl.BlockSpec((tm, LANES), lambda i: (i % pos_blocks, 0)),
            pl.BlockSpec((tm, LANES), lambda i: (i % pos_blocks, 0)),
            _resident((4, A_QK), lambda i: (0, 0)),
            _resident((MXU_DIM, MXU_DIM), lambda i: (0, 0)),
        ],
        out_specs=out_specs,
        out_shape=out_shapes,
        compiler_params=pltpu.CompilerParams(
            dimension_semantics=("parallel",), vmem_limit_bytes=_vmem_limit(est)),
        name="in_proj",
    )(x, gain, w, cos_t, sin_t, qk_gains, gmat)


LOG2E = math.log2(math.e)
SAFE_LOG2_SCORE = 60.0
NORM_MARGIN = 1.05
A_PAIR = 4


def _diff_attn_kernel(bound_ref, q_ref, k_ref, vt_ref, lamv_ref, subln_ref, o_ref, acc_ref, *,
                      lam_init):
    blk = ATTN_BLOCK
    i = pl.program_id(2)
    n_chain = 2 * A_PAIR
    per_group = MXU_DIM // D_HEAD
    lane = lax.broadcasted_iota(jnp.int32, (1, MXU_DIM), 1)
    qs = []
    for n in range(n_chain):
        g, r = divmod(n, per_group)
        qg = q_ref[:, g * MXU_DIM:(g + 1) * MXU_DIM]
        qs.append(jnp.where((lane >= D_HEAD * r) & (lane < D_HEAD * (r + 1)), qg,
                            jnp.zeros_like(qg)))
    half = blk // 2

    def scores(j, n, diagonal):
        g = n // per_group
        kb = k_ref[pl.ds(pl.multiple_of(j * blk, blk), blk), g * MXU_DIM:(g + 1) * MXU_DIM]
        st = lax.dot_general(kb, qs[n], _NT, preferred_element_type=_F32)
        if diagonal:
            key_chunk = lax.broadcasted_iota(jnp.int32, (blk, blk), 0) // CHUNK
            qry_chunk = lax.broadcasted_iota(jnp.int32, (blk, blk), 1) // CHUNK
            st = jnp.where(key_chunk <= qry_chunk, st, NEG_BIG)
        return st

    def diagonal_scores(j, n):
        g = n // per_group
        row0 = pl.multiple_of(j * blk, blk)
        cols = slice(g * MXU_DIM, (g + 1) * MXU_DIM)
        key_chunk = lax.broadcasted_iota(jnp.int32, (half, blk), 0) // CHUNK
        qry_chunk = lax.broadcasted_iota(jnp.int32, (half, blk), 1) // CHUNK
        first = lax.dot_general(k_ref[pl.ds(row0, half), cols], qs[n], _NT,
                                preferred_element_type=_F32)
        first = jnp.where(key_chunk <= qry_chunk, first, NEG_BIG)
        second = lax.dot_general(k_ref[pl.ds(row0 + half, half), cols], qs[n][half:, :], _NT,
                                 preferred_element_type=_F32)
        key_chunk2 = lax.broadcasted_iota(jnp.int32, (half, half), 0) // CHUNK
        qry_chunk2 = lax.broadcasted_iota(jnp.int32, (half, half), 1) // CHUNK
        second = jnp.where(key_chunk2 <= qry_chunk2, second, NEG_BIG)
        return first, second

    def values_t(j, n):
        head = n // 2
        return vt_ref[j, head * A_DV:(head + 1) * A_DV, :]

    def plain_steps(blocks, carry):
        out = list(carry)
        work = [(j, diagonal, list(range(g * per_group, (g + 1) * per_group)))
                for j, diagonal in blocks for g in range(n_chain // per_group)]

        def issue(j, diagonal, chains):
            return {n: (diagonal_scores(j, n) if diagonal else scores(j, n, False))
                    for n in chains}

        pending = issue(*work[0])
        for idx, (j, diagonal, chains) in enumerate(work):
            sts = pending
            if idx + 1 < len(work):
                pending = issue(*work[idx + 1])
            for n in chains:
                if diagonal:
                    first, second = (jnp.exp2(s) for s in sts[n])
                    widened = jnp.concatenate([jnp.zeros_like(second), second], axis=1)
                    out[n] = (out[n] + jnp.sum(first, axis=0, keepdims=True)
                              + jnp.sum(widened, axis=0, keepdims=True))
                    vt = values_t(j, n)
                    acc_ref[n] += jnp.dot(vt[:, :half], first.astype(_BF),
                                          preferred_element_type=_F32)
                    acc_ref[n, :, half:] += jnp.dot(vt[:, half:], second.astype(_BF),
                                                    preferred_element_type=_F32)
                else:
                    p = jnp.exp2(sts[n])
                    out[n] = out[n] + jnp.sum(p, axis=0, keepdims=True)
                    acc_ref[n] += jnp.dot(values_t(j, n), p.astype(_BF),
                                          preferred_element_type=_F32)
        return tuple(out)

    def shifted_step(j, carry, diagonal):
        out = []
        for n in range(n_chain):
            m_old, l_old = carry[n]
            st = scores(j, n, diagonal)
            m_new = jnp.maximum(m_old, jnp.max(st, axis=0, keepdims=True))
            alpha = jnp.exp2(m_old - m_new)
            p = jnp.exp2(st - m_new)
            l_new = alpha * l_old + jnp.sum(p, axis=0, keepdims=True)
            pv = jnp.dot(values_t(j, n), p.astype(_BF), preferred_element_type=_F32)
            acc_ref[n] = alpha * acc_ref[n] + pv
            out.append((m_new, l_new))
        return tuple(out)

    def finish(ls):
        lamv = lamv_ref[...]
        lam = (jnp.exp(jnp.sum(lamv[0:1] * lamv[1:2], axis=-1, keepdims=True))
               - jnp.exp(jnp.sum(lamv[2:3] * lamv[3:4], axis=-1, keepdims=True)) + lam_init)
        for head in range(A_PAIR):
            o = (acc_ref[2 * head] / ls[2 * head]
                 - lam * (acc_ref[2 * head + 1] / ls[2 * head + 1]))
            ms = jnp.mean(o * o, axis=0, keepdims=True)
            y = o * lax.rsqrt(ms + EPS) * subln_ref[...] * (1.0 - lam_init)
            o_ref[:, head * A_DV:(head + 1) * A_DV] = y.T.astype(_BF)

    acc_ref[...] = jnp.zeros_like(acc_ref)
    bounded = bound_ref[0] <= SAFE_LOG2_SCORE

    @pl.when(bounded)
    def _():
        init = tuple(jnp.zeros((1, blk), _F32) for _ in range(n_chain))
        carry = lax.fori_loop(
            0, i // 2, lambda jj, c: plain_steps([(2 * jj, False), (2 * jj + 1, False)], c), init)
        odd = (i % 2) == 1
        pl.when(odd)(lambda: finish(plain_steps([(i - 1, False), (i, True)], carry)))
        pl.when(jnp.logical_not(odd))(lambda: finish(plain_steps([(i, True)], carry)))

    @pl.when(jnp.logical_not(bounded))
    def _():
        init = tuple((jnp.full((1, blk), NEG_BIG, _F32), jnp.zeros((1, blk), _F32))
                     for _ in range(n_chain))
        carry = lax.fori_loop(0, i, lambda j, c: shifted_step(j, c, False), init)
        finish([l for _, l in shifted_step(i, carry, True)])


def _diff_attn(score_bound, aq, ak, avt, lamv, subln_col, batch, seq, lam_init):
    blk = ATTN_BLOCK
    nq = seq // blk
    t = batch * seq
    width = A_PAIR * A_DV
    est = 2 * (seq * width * 2) * 2 + 2 * A_PAIR * (A_DV * blk * 4) + 16 * blk * blk * 4 \
        + VMEM_TEMPORARIES
    return pl.pallas_call(
        functools.partial(_diff_attn_kernel, lam_init=lam_init),
        grid=(batch, A_HEADS // A_PAIR, nq),
        in_specs=[
            pl.BlockSpec(memory_space=pltpu.SMEM),
            pl.BlockSpec((blk, width), lambda b, h, i: (b * nq + i, h)),
            pl.BlockSpec((seq, width), lambda b, h, i: (b, h)),
            pl.BlockSpec((nq, width, blk), lambda b, h, i: (b, h, 0)),
            pl.BlockSpec((4, D_HEAD), lambda b, h, i: (0, 0)),
            pl.BlockSpec((A_DV, 1), lambda b, h, i: (0, 0)),
        ],
        out_specs=pl.BlockSpec((blk, width), lambda b, h, i: (b * nq + i, h)),
        out_shape=jax.ShapeDtypeStruct((t, A_V), _BF),
        scratch_shapes=[pltpu.VMEM((2 * A_PAIR, A_DV, blk), _F32)],
        compiler_params=pltpu.CompilerParams(
            dimension_semantics=("parallel", "parallel", "parallel"),
            vmem_limit_bytes=_vmem_limit(est)),
        name="diff_attn",
    )(score_bound, aq, ak, avt, lamv, subln_col)


def _score_bound(q_gain, k_gain):
    return (D_HEAD * NORM_MARGIN * jnp.max(jnp.abs(q_gain)) * jnp.max(jnp.abs(k_gain))).reshape(1)


B_TABLES = (B_WINDOW - B_QTILE) // B_QTILE + 1
B_GEN_LEN = 2048
B_GROUP = MXU_DIM // D_HEAD


def _build_bias_tables(gen_ref, mb_ref):
    qchunk = lax.broadcasted_iota(jnp.int32, (B_QTILE, B_WINDOW), 0) // CHUNK
    kchunk = lax.broadcasted_iota(jnp.int32, (B_QTILE, B_WINDOW), 1) // CHUNK
    for hh in range(B_GROUP):
        rolled = pltpu.roll(jnp.broadcast_to(gen_ref[hh], (B_QTILE, B_GEN_LEN)), 0, 1,
                            stride=1, stride_axis=0)
        for t in range(B_TABLES):
            lo = B_WINDOW - t * B_QTILE
            dchunk = qchunk - kchunk + (t * B_QTILE) // CHUNK
            visible = (dchunk >= 0) & (dchunk <= B_LOOKBACK)
            mb_ref[hh, t] = jnp.where(visible, rolled[:, lo:lo + B_WINDOW], NEG_BIG)


B_TILES = 4


def _chunk_attn_kernel(bound_ref, q_ref, k_ref, v_ref, gen_ref, o_ref, mb_ref):
    i = pl.program_id(2)

    @pl.when(i == 0)
    def _():
        _build_bias_tables(gen_ref, mb_ref)

    lane = lax.broadcasted_iota(jnp.int32, (1, MXU_DIM), 1)
    in_head = [(lane >= D_HEAD * hh) & (lane < D_HEAD * (hh + 1)) for hh in range(B_GROUP)]
    tiles = []
    for tt in range(B_TILES):
        tile = i * B_TILES + tt
        start = pl.multiple_of(jnp.maximum(tile * B_QTILE - (B_WINDOW - B_QTILE), 0), B_QTILE)
        tiles.append(dict(
            rows=slice(tt * B_QTILE, (tt + 1) * B_QTILE),
            table=jnp.minimum(tile, B_TABLES - 1),
            kw=k_ref[pl.ds(start, B_WINDOW), :],
            vw=v_ref[pl.ds(start, B_WINDOW), :]))

    def logits(tt, hh):
        q = q_ref[tiles[tt]["rows"], :]
        s = lax.dot_general(jnp.where(in_head[hh], q, jnp.zeros_like(q)), tiles[tt]["kw"], _NT,
                            preferred_element_type=_F32)
        return s + mb_ref[hh, tiles[tt]["table"]]

    def attend(shifted):
        work = [(tt, hh) for tt in range(B_TILES) for hh in range(B_GROUP)]
        outs = [None] * B_TILES
        s = logits(*work[0])
        for idx, (tt, hh) in enumerate(work):
            s_next = logits(*work[idx + 1]) if idx + 1 < len(work) else None
            if shifted:
                s = s - jnp.max(s, axis=-1, keepdims=True)
            p = jnp.exp2(s)
            l = jnp.sum(p, axis=-1, keepdims=True)
            o = jnp.dot(p.astype(_BF), tiles[tt]["vw"], preferred_element_type=_F32) / l
            outs[tt] = o if outs[tt] is None else jnp.where(in_head[hh], o, outs[tt])
            s = s_next
        for tt in range(B_TILES):
            o_ref[tiles[tt]["rows"], :] = outs[tt].astype(_BF)

    bounded = bound_ref[0] <= SAFE_LOG2_SCORE
    pl.when(bounded)(lambda: attend(False))
    pl.when(jnp.logical_not(bounded))(lambda: attend(True))


def _chunk_attn(score_bound, bq, bk, bv, bias_gen, batch, seq):
    step_rows = B_TILES * B_QTILE
    nq = seq // step_rows
    t = batch * seq
    est = B_GROUP * B_TABLES * B_QTILE * B_WINDOW * 4 + 4 * B_QTILE * B_GEN_LEN * 4 \
        + 4 * seq * MXU_DIM * 2 + VMEM_TEMPORARIES
    return pl.pallas_call(
        _chunk_attn_kernel,
        grid=(batch, B_HEADS // B_GROUP, nq),
        in_specs=[
            pl.BlockSpec(memory_space=pltpu.SMEM),
            pl.BlockSpec((step_rows, MXU_DIM), lambda b, h, i: (b * nq + i, h)),
            pl.BlockSpec((seq, MXU_DIM), lambda b, h, i: (b, h)),
            pl.BlockSpec((seq, MXU_DIM), lambda b, h, i: (b, h)),
            pl.BlockSpec((B_GROUP, 1, B_GEN_LEN), lambda b, h, i: (h, 0, 0)),
        ],
        out_specs=pl.BlockSpec((step_rows, MXU_DIM), lambda b, h, i: (b * nq + i, h)),
        out_shape=jax.ShapeDtypeStruct((t, B_QKV), _BF),
        scratch_shapes=[pltpu.VMEM((B_GROUP, B_TABLES, B_QTILE, B_WINDOW), _F32)],
        compiler_params=pltpu.CompilerParams(
            dimension_semantics=("parallel", "parallel", "arbitrary"),
            vmem_limit_bytes=_vmem_limit(est)),
        name="chunk_attn",
    )(score_bound, bq, bk, bv, bias_gen)


def _bias_generator(rel_bias):
    b = rel_bias.astype(_F32) * LOG2E
    n_far = B_WINDOW - B_MAX_REL + 1
    mid = b[:, 1:CHUNK - 1 + B_MAX_REL][:, ::-1]
    n_neg = B_GEN_LEN - n_far - mid.shape[1]
    heads = b.shape[0]
    gen = jnp.concatenate([
        jnp.broadcast_to(b[:, -1:], (heads, n_far)), mid,
        jnp.broadcast_to(b[:, :1], (heads, n_neg))], axis=1)
    return gen[:, None, :]


def _retention_kernel(q_ref, kt_ref, v_ref, g_ref, cn_ref, o_ref, decay_ref, *, seq):
    blk = RET_BLOCK
    sub = RET_SUB
    h = pl.program_id(1)
    hf = jnp.full((1, 1), h, jnp.int32).astype(_F32)
    log_gamma = jnp.log(1.0 - jnp.exp2(-5.0 - hf))
    diff = (lax.broadcasted_iota(jnp.int32, (sub, sub), 0)
            - lax.broadcasted_iota(jnp.int32, (sub, sub), 1)).astype(_F32)
    decay_ref[...] = jnp.where(diff >= 0, jnp.exp(log_gamma * jnp.maximum(diff, 0.0)), 0.0)
    pos = lax.broadcasted_iota(jnp.int32, (sub, LANES), 0).astype(_F32)
    xi = jnp.exp(log_gamma * (pos + 1.0))
    zeta = jnp.exp(log_gamma * (sub - 1.0 - pos))
    sub_decay = jnp.exp(log_gamma * float(sub))
    odd = (h % 2) == 1
    q_keep = (lax.broadcasted_iota(jnp.int32, (1, LANES), 1) >= D_HEAD) == odd
    k_keep = (lax.broadcasted_iota(jnp.int32, (LANES, 1), 0) >= D_HEAD) == odd
    gain = cn_ref[0]

    def body(j, state):
        for s in range(blk // sub):
            rows = pl.ds(pl.multiple_of(j * blk + s * sub, sub), sub)
            qb = q_ref[rows, :]
            qb = jnp.where(q_keep, qb, jnp.zeros_like(qb))
            kt = kt_ref[j, :, s * sub:(s + 1) * sub]
            kt = jnp.where(k_keep, kt, jnp.zeros_like(kt))
            vb = v_ref[rows, :]
            sc = jnp.dot(qb, kt, preferred_element_type=_F32) * decay_ref[...]
            inner = jnp.dot(sc.astype(_BF), vb, preferred_element_type=_F32)
            cross = jnp.dot(qb, state.astype(_BF), preferred_element_type=_F32) * xi
            o = inner + cross
            vz = (vb.astype(_F32) * zeta).astype(_BF)
            state = state * sub_decay + jnp.dot(kt, vz, preferred_element_type=_F32)
            ms = jnp.mean(o * o, axis=-1, keepdims=True)
            y = o * lax.rsqrt(ms + EPS) * gain
            gg = g_ref[rows, :].astype(_F32)
            o_ref[rows, :] = (gg * jax.nn.sigmoid(gg) * y).astype(_BF)
        return state

    lax.fori_loop(0, seq // blk, body, jnp.zeros((LANES, C_DV), _F32), unroll=RET_UNROLL)


def _retention(cq, ckt, cv, cg, cnorm, batch, seq):
    blk = RET_BLOCK
    nb = seq // blk
    t = batch * seq
    est = 2 * 5 * seq * LANES * 2 + 6 * blk * blk * 4 + VMEM_TEMPORARIES
    return pl.pallas_call(
        functools.partial(_retention_kernel, seq=seq),
        grid=(batch, C_HEADS),
        in_specs=[
            pl.BlockSpec((seq, LANES), lambda b, h: (b, h // 2)),
            pl.BlockSpec((nb, LANES, blk), lambda b, h: (b, h // 2, 0)),
            pl.BlockSpec((seq, C_DV), lambda b, h: (b, h)),
            pl.BlockSpec((seq, C_DV), lambda b, h: (b, h)),
            pl.BlockSpec((1, 1, C_DV), lambda b, h: (h, 0, 0)),
        ],
        out_specs=pl.BlockSpec((seq, C_DV), lambda b, h: (b, h)),
        out_shape=jax.ShapeDtypeStruct((t, C_V), _BF),
        scratch_shapes=[pltpu.VMEM((RET_SUB, RET_SUB), _F32)],
        compiler_params=pltpu.CompilerParams(
            dimension_semantics=("parallel", "parallel"), vmem_limit_bytes=_vmem_limit(est)),
        name="retention",
    )(cq, ckt, cv, cg, cnorm)


def _merge_ffn_kernel(x_ref, ya_ref, yb_ref, yc_ref, gate_ref, wa_ref, wb_ref, wc_ref, wo_ref,
                      gain_ref, wg_ref, wu_ref, wd_ref, o_ref, acc_ref):
    merged = None
    for n, (y_ref, w_ref) in enumerate(((ya_ref, wa_ref), (yb_ref, wb_ref), (yc_ref, wc_ref))):
        br = jnp.dot(y_ref[...], w_ref[...], preferred_element_type=_F32)
        term = gate_ref[:, n * D_MODEL:(n + 1) * D_MODEL].astype(_F32) * br
        merged = term if merged is None else merged + term
    x = x_ref[...] + jnp.dot(merged.astype(_BF), wo_ref[...], preferred_element_type=_F32)
    h = _rms_rows(x, gain_ref[...]).astype(_BF)
    for c in range(D_FF // FF_CHUNK):
        sl = slice(c * FF_CHUNK, (c + 1) * FF_CHUNK)
        g = jnp.dot(h, wg_ref[:, sl], preferred_element_type=_F32)
        u = jnp.dot(h, wu_ref[:, sl], preferred_element_type=_F32)
        a = (g * jax.nn.sigmoid(g) * u).astype(_BF)
        part = jnp.dot(a, wd_ref[sl, :], preferred_element_type=_F32)
        if c == 0:
            acc_ref[...] = part
        else:
            acc_ref[...] += part
    o_ref[...] = x + 0.5 * acc_ref[...]


def _merge_ffn(layer, x, ya, yb, yc, gates, wa, wb, wc, wo, gain, wg, wu, wd):
    t = x.shape[0]
    tm = TOKEN_TILE
    row = lambda i: (i, 0)
    const = lambda i: (0, 0)
    est = (5 * D_MODEL * D_MODEL + 3 * D_MODEL * D_FF) * 2 \
        + 2 * tm * (2 * D_MODEL * 4 + (A_V + B_QKV + C_V) * 2 + N_BRANCH * D_MODEL * 2) \
        + 8 * tm * D_MODEL * 4 + 4 * tm * FF_CHUNK * 4 + VMEM_TEMPORARIES
    return pl.pallas_call(
        _merge_ffn_kernel,
        grid=(t // tm,),
        in_specs=[
            pl.BlockSpec((tm, D_MODEL), row),
            pl.BlockSpec((tm, A_V), row),
            pl.BlockSpec((tm, B_QKV), row),
            pl.BlockSpec((tm, C_V), row),
            pl.BlockSpec((tm, N_BRANCH * D_MODEL), row),
            _layer_weight(layer, A_V, D_MODEL),
            _layer_weight(layer, B_QKV, D_MODEL),
            _layer_weight(layer, C_V, D_MODEL),
            _layer_weight(layer, D_MODEL, D_MODEL),
            _resident((1, D_MODEL), const),
            _layer_weight(layer, D_MODEL, D_FF),
            _layer_weight(layer, D_MODEL, D_FF),
            _layer_weight(layer, D_FF, D_MODEL),
        ],
        out_specs=pl.BlockSpec((tm, D_MODEL), row),
        out_shape=jax.ShapeDtypeStruct((t, D_MODEL), _F32),
        scratch_shapes=[pltpu.VMEM((tm, D_MODEL), _F32)],
        compiler_params=pltpu.CompilerParams(
            dimension_semantics=("parallel",), vmem_limit_bytes=_vmem_limit(est)),
        name="merge_ffn",
    )(x, ya, yb, yc, gates, wa, wb, wc, wo, gain, wg, wu, wd)


def _rope_tables(seq):
    half = D_HEAD // 2
    inv = ROPE_THETA ** (-jnp.arange(half, dtype=_F32) / half)
    ang = jnp.arange(seq, dtype=_F32)[:, None] * inv[None, :]
    cos, sin = jnp.cos(ang), jnp.sin(ang)
    cos_t = jnp.concatenate([cos, cos] * (LANES // D_HEAD), axis=1)
    sin_t = jnp.concatenate([-sin, sin] * (LANES // D_HEAD), axis=1)
    return cos_t, sin_t


def _head_mean_matrix():
    g = np.kron(np.eye(MXU_DIM // D_HEAD), np.full((D_HEAD, D_HEAD), 1.0 / D_HEAD))
    return jnp.asarray(g, dtype=_BF)


def kernel(x, ffn1_norm, ffn1_w_gate, ffn1_w_up, ffn1_w_down, mix_norm, w_in, a_q_norm, a_k_norm, a_lambda_q1, a_lambda_k1, a_lambda_q2, a_lambda_k2, a_subln, b_q_norm, b_k_norm, b_rel_bias, c_out_norm, w_branch_a, w_branch_b, w_branch_c, w_out, ffn2_norm, ffn2_w_gate, ffn2_w_up, ffn2_w_down):
    batch, seq, d = x.shape
    assert d == D_MODEL and seq % ATTN_BLOCK == 0 and seq % RET_BLOCK == 0
    assert seq % TOKEN_TILE == 0 and seq >= B_WINDOW and ATTN_BLOCK == TOKEN_TILE == RET_BLOCK
    assert seq % (B_TILES * B_QTILE) == 0
    scale = D_HEAD ** -0.5
    cos_t, sin_t = _rope_tables(seq)
    gmat = _head_mean_matrix()
    bf = lambda w: w.astype(_BF)
    f1g, f1u, f1d = bf(ffn1_w_gate), bf(ffn1_w_up), bf(ffn1_w_down)
    f2g, f2u, f2d = bf(ffn2_w_gate), bf(ffn2_w_up), bf(ffn2_w_down)
    win, wba, wbb, wbc, wo = bf(w_in), bf(w_branch_a), bf(w_branch_b), bf(w_branch_c), bf(w_out)
    xt = x.reshape(batch * seq, D_MODEL)
    for l in range(DEPTH):
        lam_init = 0.8 - 0.6 * math.exp(-0.3 * l)
        xt = _ffn(l, xt, ffn1_norm[l][None, :], f1g, f1u, f1d)

        heads = A_QK // D_HEAD
        qk_gains = jnp.stack([
            jnp.tile(a_q_norm[l].astype(_F32) * (scale * LOG2E), heads),
            jnp.tile(a_k_norm[l].astype(_F32), heads),
            jnp.tile(b_q_norm[l].astype(_F32) * (scale * LOG2E), heads),
            jnp.tile(b_k_norm[l].astype(_F32), heads),
        ])
        aq, ak, avt, bq, bk, bv, cq, ckt, cv, cg, gates = _in_proj(
            l, xt, mix_norm[l][None, :], win, cos_t, sin_t, qk_gains, gmat, seq)

        lamv = jnp.stack([a_lambda_q1[l], a_lambda_k1[l], a_lambda_q2[l], a_lambda_k2[l]]).astype(_F32)
        a_bound = _score_bound(qk_gains[0, :D_HEAD], qk_gains[1, :D_HEAD])
        ya = _diff_attn(a_bound, aq, ak, avt, lamv, a_subln[l].astype(_F32)[:, None], batch, seq, lam_init)
        bias_gen = _bias_generator(b_rel_bias[l])
        b_bound = _score_bound(qk_gains[2, :D_HEAD], qk_gains[3, :D_HEAD]) + jnp.max(jnp.abs(bias_gen))
        yb = _chunk_attn(b_bound, bq, bk, bv, bias_gen, batch, seq)
        yc = _retention(cq, ckt, cv, cg, c_out_norm[l].astype(_F32)[:, None, :], batch, seq)

        xt = _merge_ffn(l, xt, ya, yb, yc, gates, wba, wbb, wbc, wo, ffn2_norm[l][None, :],
                        f2g, f2u, f2d)
    return xt.reshape(batch, seq, D_MODEL)
```

```python
import functools
import math

import numpy as np
import jax
import jax.numpy as jnp
from jax import lax
from jax.experimental import pallas as pl
from jax.experimental.pallas import tpu as pltpu

D_MODEL = 1024
DEPTH = 2
CHUNK = 64
D_HEAD = 64
ROPE_THETA = 10000.0
EPS = 1e-6
A_HEADS = 4
A_DV = 2 * D_HEAD
B_HEADS = 8
B_LOOKBACK = 8
B_MAX_REL = 256
C_HEADS = 4
C_DV = 2 * D_HEAD
N_BRANCH = 3
D_FF = 2816

A_QK = A_HEADS * 2 * D_HEAD
A_V = A_HEADS * A_DV
B_QKV = B_HEADS * D_HEAD
C_QK = C_HEADS * D_HEAD
C_V = C_HEADS * C_DV
SPLITS = (A_QK, A_QK, A_V, B_QKV, B_QKV, B_QKV, C_QK, C_QK, C_V, C_V, N_BRANCH * D_MODEL)
IN_COLS = sum(SPLITS)
_OFFS = tuple(int(v) for v in np.cumsum((0,) + SPLITS))

LANES = 128
MXU_DIM = 256
VMEM_BYTES_V7X = 64 * 1024 * 1024
VMEM_HEADROOM = 6 * 1024 * 1024
VMEM_DEFAULT_LIMIT = 32 * 1024 * 1024
VMEM_TEMPORARIES = 12 * 1024 * 1024

TOKEN_TILE = 512
FF_CHUNK = MXU_DIM
PROJ_CHUNK = 512
ATTN_BLOCK = 512
B_QTILE = 256
B_WINDOW = B_QTILE + B_LOOKBACK * CHUNK
RET_BLOCK = 512
RET_SUB = 256
RET_UNROLL = 16
NEG_BIG = -1e30

_BF = jnp.bfloat16
_F32 = jnp.float32
_NT = (((1,), (1,)), ((), ()))


def _vmem_limit(estimate_bytes):
    return int(min(VMEM_BYTES_V7X - VMEM_HEADROOM, max(VMEM_DEFAULT_LIMIT, estimate_bytes)))


def _resident(shape, index_map):
    return pl.BlockSpec(shape, index_map, pipeline_mode=pl.Buffered(1))


def _layer_weight(layer, rows, cols):
    return pl.BlockSpec((None, rows, cols), lambda i: (layer, 0, 0), pipeline_mode=pl.Buffered(1))


def _rms_rows(x, gain):
    ms = jnp.mean(x * x, axis=-1, keepdims=True)
    return x * lax.rsqrt(ms + EPS) * gain


def _ffn_kernel(x_ref, gain_ref, wg_ref, wu_ref, wd_ref, o_ref, acc_ref):
    x = x_ref[...]
    h = _rms_rows(x, gain_ref[...]).astype(_BF)
    for c in range(D_FF // FF_CHUNK):
        sl = slice(c * FF_CHUNK, (c + 1) * FF_CHUNK)
        g = jnp.dot(h, wg_ref[:, sl].astype(_BF), preferred_element_type=_F32)
        u = jnp.dot(h, wu_ref[:, sl].astype(_BF), preferred_element_type=_F32)
        a = (g * jax.nn.sigmoid(g) * u).astype(_BF)
        part = jnp.dot(a, wd_ref[sl, :].astype(_BF), preferred_element_type=_F32)
        if c == 0:
            acc_ref[...] = part
        else:
            acc_ref[...] += part
    o_ref[...] = x + 0.5 * acc_ref[...]


def _ffn(layer, x, gain, wg, wu, wd):
    t = x.shape[0]
    tm = TOKEN_TILE
    est = 3 * D_MODEL * D_FF * wg.dtype.itemsize + 5 * tm * D_MODEL * 4 + 4 * tm * FF_CHUNK * 4 \
        + VMEM_TEMPORARIES
    return pl.pallas_call(
        _ffn_kernel,
        grid=(t // tm,),
        in_specs=[
            pl.BlockSpec((tm, D_MODEL), lambda i: (i, 0)),
            _resident((1, D_MODEL), lambda i: (0, 0)),
            _layer_weight(layer, D_MODEL, D_FF),
            _layer_weight(layer, D_MODEL, D_FF),
            _layer_weight(layer, D_FF, D_MODEL),
        ],
        out_specs=pl.BlockSpec((tm, D_MODEL), lambda i: (i, 0)),
        out_shape=jax.ShapeDtypeStruct((t, D_MODEL), _F32),
        scratch_shapes=[pltpu.VMEM((tm, D_MODEL), _F32)],
        compiler_params=pltpu.CompilerParams(
            dimension_semantics=("parallel",), vmem_limit_bytes=_vmem_limit(est)),
        name="ffn",
    )(x, gain, wg, wu, wd)


def _swap_halves(x):
    lane = lax.broadcasted_iota(jnp.int32, (1, LANES), 1)
    upper = (lane & (D_HEAD // 2)) != 0
    outs = []
    for s in range(x.shape[1] // LANES):
        xs = x[:, s * LANES:(s + 1) * LANES]
        from_below = pltpu.roll(xs, D_HEAD // 2, 1)
        from_above = pltpu.roll(xs, LANES - D_HEAD // 2, 1)
        outs.append(jnp.where(upper, from_below, from_above))
    return outs[0] if len(outs) == 1 else jnp.concatenate(outs, axis=1)


def _tile_lanes(t, n):
    reps = n // t.shape[1]
    return t if reps == 1 else jnp.concatenate([t] * reps, axis=1)


def _rope(x, cos, sin):
    n = x.shape[1]
    return x * _tile_lanes(cos, n) + _swap_halves(x) * _tile_lanes(sin, n)


def _head_ms(p, gmat):
    sq = (p * p).astype(_BF)
    outs = [jnp.dot(sq[:, s:s + MXU_DIM], gmat, preferred_element_type=_F32)
            for s in range(0, p.shape[1], MXU_DIM)]
    return outs[0] if len(outs) == 1 else jnp.concatenate(outs, axis=1)


def _in_proj_kernel(x_ref, gain_ref, w_ref, cos_ref, sin_ref, qkg_ref, gmat_ref,
                    aq_ref, ak_ref, avt_ref, bq_ref, bk_ref, bv_ref,
                    cq_ref, ckt_ref, cv_ref, cg_ref, gate_ref):
    h = _rms_rows(x_ref[...], gain_ref[...]).astype(_BF)
    cos = cos_ref[...]
    sin = sin_ref[...]
    gmat = gmat_ref[...]

    def proj(lo, width):
        return jnp.dot(h, w_ref[:, lo:lo + width], preferred_element_type=_F32)

    def normed(p, row):
        return p * lax.rsqrt(_head_ms(p, gmat) + EPS) * qkg_ref[row:row + 1, :]

    for c in range(N_BRANCH * D_MODEL // PROJ_CHUNK):
        lo = c * PROJ_CHUNK
        gate_ref[:, lo:lo + PROJ_CHUNK] = jax.nn.sigmoid(
            proj(_OFFS[10] + lo, PROJ_CHUNK)).astype(_BF)
    aq_ref[...] = _rope(normed(proj(_OFFS[0], A_QK), 0), cos, sin).astype(_BF)
    ak_ref[...] = _rope(normed(proj(_OFFS[1], A_QK), 1), cos, sin).astype(_BF)
    bq_ref[...] = normed(proj(_OFFS[3], B_QKV), 2).astype(_BF)
    bk_ref[...] = normed(proj(_OFFS[4], B_QKV), 3).astype(_BF)
    cq_ref[...] = _rope(proj(_OFFS[6], C_QK), cos, sin).astype(_BF)
    ckt_ref[0] = (_rope(proj(_OFFS[7], C_QK), cos, sin) * (D_HEAD ** -0.5)).T.astype(_BF)
    avt_ref[0] = proj(_OFFS[2], A_V).T.astype(_BF)
    bv_ref[...] = proj(_OFFS[5], B_QKV).astype(_BF)
    cv_ref[...] = proj(_OFFS[8], C_V).astype(_BF)
    cg_ref[...] = proj(_OFFS[9], C_V).astype(_BF)


def _in_proj(layer, x, gain, w, cos_t, sin_t, qk_gains, gmat, seq):
    t = x.shape[0]
    tm = TOKEN_TILE
    nt = t // tm
    pos_blocks = seq // tm
    row = lambda i: (i, 0)
    tok_major = lambda n: pl.BlockSpec((tm, n), row)
    out_shapes = (
        jax.ShapeDtypeStruct((t, A_QK), _BF),
        jax.ShapeDtypeStruct((t, A_QK), _BF),
        jax.ShapeDtypeStruct((nt, A_V, tm), _BF),
        jax.ShapeDtypeStruct((t, B_QKV), _BF),
        jax.ShapeDtypeStruct((t, B_QKV), _BF),
        jax.ShapeDtypeStruct((t, B_QKV), _BF),
        jax.ShapeDtypeStruct((t, C_QK), _BF),
        jax.ShapeDtypeStruct((nt, C_QK, tm), _BF),
        jax.ShapeDtypeStruct((t, C_V), _BF),
        jax.ShapeDtypeStruct((t, C_V), _BF),
        jax.ShapeDtypeStruct((t, N_BRANCH * D_MODEL), _BF),
    )
    out_specs = (
        tok_major(A_QK), tok_major(A_QK),
        pl.BlockSpec((1, A_V, tm), lambda i: (i, 0, 0)),
        tok_major(B_QKV), tok_major(B_QKV), tok_major(B_QKV),
        tok_major(C_QK),
        pl.BlockSpec((1, C_QK, tm), lambda i: (i, 0, 0)),
        tok_major(C_V), tok_major(C_V), tok_major(N_BRANCH * D_MODEL),
    )
    est = D_MODEL * IN_COLS * 2 + 2 * tm * IN_COLS * 2 + 4 * tm * D_MODEL * 4 + VMEM_TEMPORARIES
    return pl.pallas_call(
        _in_proj_kernel,
        grid=(nt,),
        in_specs=[
            pl.BlockSpec((tm, D_MODEL), row),
            _resident((1, D_MODEL), lambda i: (0, 0)),
            _layer_weight(layer, D_MODEL, IN_COLS),
            pl.BlockSpec((tm, LANES), lambda i: (i % pos_blocks, 0)),
            pl.BlockSpec((tm, LANES), lambda i: (i % pos_blocks, 0)),
            _resident((4, A_QK), lambda i: (0, 0)),
            _resident((MXU_DIM, MXU_DIM), lambda i: (0, 0)),
        ],
        out_specs=out_specs,
        out_shape=out_shapes,
        compiler_params=pltpu.CompilerParams(
            dimension_semantics=("parallel",), vmem_limit_bytes=_vmem_limit(est)),
        name="in_proj",
    )(x, gain, w, cos_t, sin_t, qk_gains, gmat)


LOG2E = math.log2(math.e)
SAFE_LOG2_SCORE = 60.0
NORM_MARGIN = 1.05
A_PAIR = 4


def _diff_attn_kernel(bound_ref, q_ref, k_ref, vt_ref, lamv_ref, subln_ref, o_ref, acc_ref, *,
                      lam_init):
    blk = ATTN_BLOCK
    i = pl.program_id(2)
    n_chain = 2 * A_PAIR
    per_group = MXU_DIM // D_HEAD
    lane = lax.broadcasted_iota(jnp.int32, (1, MXU_DIM), 1)
    qs = []
    for n in range(n_chain):
        g, r = divmod(n, per_group)
        qg = q_ref[:, g * MXU_DIM:(g + 1) * MXU_DIM]
        qs.append(jnp.where((lane >= D_HEAD * r) & (lane < D_HEAD * (r + 1)), qg,
                            jnp.zeros_like(qg)))
    half = blk // 2

    def scores(j, n, diagonal):
        g = n // per_group
        kb = k_ref[pl.ds(pl.multiple_of(j * blk, blk), blk), g * MXU_DIM:(g + 1) * MXU_DIM]
        st = lax.dot_general(kb, qs[n], _NT, preferred_element_type=_F32)
        if diagonal:
            key_chunk = lax.broadcasted_iota(jnp.int32, (blk, blk), 0) // CHUNK
            qry_chunk = lax.broadcasted_iota(jnp.int32, (blk, blk), 1) // CHUNK
            st = jnp.where(key_chunk <= qry_chunk, st, NEG_BIG)
        return st

    def diagonal_scores(j, n):
        g = n // per_group
        row0 = pl.multiple_of(j * blk, blk)
        cols = slice(g * MXU_DIM, (g + 1) * MXU_DIM)
        key_chunk = lax.broadcasted_iota(jnp.int32, (half, blk), 0) // CHUNK
        qry_chunk = lax.broadcasted_iota(jnp.int32, (half, blk), 1) // CHUNK
        first = lax.dot_general(k_ref[pl.ds(row0, half), cols], qs[n], _NT,
                                preferred_element_type=_F32)
        first = jnp.where(key_chunk <= qry_chunk, first, NEG_BIG)
        second = lax.dot_general(k_ref[pl.ds(row0 + half, half), cols], qs[n][half:, :], _NT,
                                 preferred_element_type=_F32)
        key_chunk2 = lax.broadcasted_iota(jnp.int32, (half, half), 0) // CHUNK
        qry_chunk2 = lax.broadcasted_iota(jnp.int32, (half, half), 1) // CHUNK
        second = jnp.where(key_chunk2 <= qry_chunk2, second, NEG_BIG)
        return first, second

    def values_t(j, n):
        head = n // 2
        return vt_ref[j, head * A_DV:(head + 1) * A_DV, :]

    def plain_steps(blocks, carry):
        out = list(carry)
        work = [(j, diagonal, list(range(g * per_group, (g + 1) * per_group)))
                for j, diagonal in blocks for g in range(n_chain // per_group)]

        def issue(j, diagonal, chains):
            return {n: (diagonal_scores(j, n) if diagonal else scores(j, n, False))
                    for n in chains}

        pending = issue(*work[0])
        for idx, (j, diagonal, chains) in enumerate(work):
            sts = pending
            if idx + 1 < len(work):
                pending = issue(*work[idx + 1])
            for n in chains:
                if diagonal:
                    first, second = (jnp.exp2(s) for s in sts[n])
                    widened = jnp.concatenate([jnp.zeros_like(second), second], axis=1)
                    out[n] = (out[n] + jnp.sum(first, axis=0, keepdims=True)
                              + jnp.sum(widened, axis=0, keepdims=True))
                    vt = values_t(j, n)
                    acc_ref[n] += jnp.dot(vt[:, :half], first.astype(_BF),
                                          preferred_element_type=_F32)
                    acc_ref[n, :, half:] += jnp.dot(vt[:, half:], second.astype(_BF),
                                                    preferred_element_type=_F32)
                else:
                    p = jnp.exp2(sts[n])
                    out[n] = out[n] + jnp.sum(p, axis=0, keepdims=True)
                    acc_ref[n] += jnp.dot(values_t(j, n), p.astype(_BF),
                                          preferred_element_type=_F32)
        return tuple(out)

    def shifted_step(j, carry, diagonal):
        out = []
        for n in range(n_chain):
            m_old, l_old = carry[n]
            st = scores(j, n, diagonal)
            m_new = jnp.maximum(m_old, jnp.max(st, axis=0, keepdims=True))
            alpha = jnp.exp2(m_old - m_new)
            p = jnp.exp2(st - m_new)
            l_new = alpha * l_old + jnp.sum(p, axis=0, keepdims=True)
            pv = jnp.dot(values_t(j, n), p.astype(_BF), preferred_element_type=_F32)
            acc_ref[n] = alpha * acc_ref[n] + pv
            out.append((m_new, l_new))
        return tuple(out)

    def finish(ls):
        lamv = lamv_ref[...]
        lam = (jnp.exp(jnp.sum(lamv[0:1] * lamv[1:2], axis=-1, keepdims=True))
               - jnp.exp(jnp.sum(lamv[2:3] * lamv[3:4], axis=-1, keepdims=True)) + lam_init)
        for head in range(A_PAIR):
            o = (acc_ref[2 * head] / ls[2 * head]
                 - lam * (acc_ref[2 * head + 1] / ls[2 * head + 1]))
            ms = jnp.mean(o * o, axis=0, keepdims=True)
            y = o * lax.rsqrt(ms + EPS) * subln_ref[...] * (1.0 - lam_init)
            o_ref[:, head * A_DV:(head + 1) * A_DV] = y.T.astype(_BF)

    acc_ref[...] = jnp.zeros_like(acc_ref)
    bounded = bound_ref[0] <= SAFE_LOG2_SCORE

    @pl.when(bounded)
    def _():
        init = tuple(jnp.zeros((1, blk), _F32) for _ in range(n_chain))
        carry = lax.fori_loop(
            0, i // 2, lambda jj, c: plain_steps([(2 * jj, False), (2 * jj + 1, False)], c), init)
        odd = (i % 2) == 1
        pl.when(odd)(lambda: finish(plain_steps([(i - 1, False), (i, True)], carry)))
        pl.when(jnp.logical_not(odd))(lambda: finish(plain_steps([(i, True)], carry)))

    @pl.when(jnp.logical_not(bounded))
    def _():
        init = tuple((jnp.full((1, blk), NEG_BIG, _F32), jnp.zeros((1, blk), _F32))
                     for _ in range(n_chain))
        carry = lax.fori_loop(0, i, lambda j, c: shifted_step(j, c, False), init)
        finish([l for _, l in shifted_step(i, carry, True)])


def _diff_attn(score_bound, aq, ak, avt, lamv, subln_col, batch, seq, lam_init):
    blk = ATTN_BLOCK
    nq = seq // blk
    t = batch * seq
    width = A_PAIR * A_DV
    est = 2 * (seq * width * 2) * 2 + 2 * A_PAIR * (A_DV * blk * 4) + 16 * blk * blk * 4 \
        + VMEM_TEMPORARIES
    return pl.pallas_call(
        functools.partial(_diff_attn_kernel, lam_init=lam_init),
        grid=(batch, A_HEADS // A_PAIR, nq),
        in_specs=[
            pl.BlockSpec(memory_space=pltpu.SMEM),
            pl.BlockSpec((blk, width), lambda b, h, i: (b * nq + i, h)),
            pl.BlockSpec((seq, width), lambda b, h, i: (b, h)),
            pl.BlockSpec((nq, width, blk), lambda b, h, i: (b, h, 0)),
            pl.BlockSpec((4, D_HEAD), lambda b, h, i: (0, 0)),
            pl.BlockSpec((A_DV, 1), lambda b, h, i: (0, 0)),
        ],
        out_specs=pl.BlockSpec((blk, width), lambda b, h, i: (b * nq + i, h)),
        out_shape=jax.ShapeDtypeStruct((t, A_V), _BF),
        scratch_shapes=[pltpu.VMEM((2 * A_PAIR, A_DV, blk), _F32)],
        compiler_params=pltpu.CompilerParams(
            dimension_semantics=("parallel", "parallel", "parallel"),
            vmem_limit_bytes=_vmem_limit(est)),
        name="diff_attn",
    )(score_bound, aq, ak, avt, lamv, subln_col)


def _score_bound(q_gain, k_gain):
    return (D_HEAD * NORM_MARGIN * jnp.max(jnp.abs(q_gain)) * jnp.max(jnp.abs(k_gain))).reshape(1)


B_TABLES = (B_WINDOW - B_QTILE) // B_QTILE + 1
B_GEN_LEN = 2048
B_GROUP = MXU_DIM // D_HEAD


def _build_bias_tables(gen_ref, mb_ref):
    qchunk = lax.broadcasted_iota(jnp.int32, (B_QTILE, B_WINDOW), 0) // CHUNK
    kchunk = lax.broadcasted_iota(jnp.int32, (B_QTILE, B_WINDOW), 1) // CHUNK
    for hh in range(B_GROUP):
        rolled = pltpu.roll(jnp.broadcast_to(gen_ref[hh], (B_QTILE, B_GEN_LEN)), 0, 1,
                            stride=1, stride_axis=0)
        for t in range(B_TABLES):
            lo = B_WINDOW - t * B_QTILE
            dchunk = qchunk - kchunk + (t * B_QTILE) // CHUNK
            visible = (dchunk >= 0) & (dchunk <= B_LOOKBACK)
            mb_ref[hh, t] = jnp.where(visible, rolled[:, lo:lo + B_WINDOW], NEG_BIG)


B_TILES = 4


def _chunk_attn_kernel(bound_ref, q_ref, k_ref, v_ref, gen_ref, o_ref, mb_ref):
    i = pl.program_id(2)

    @pl.when(i == 0)
    def _():
        _build_bias_tables(gen_ref, mb_ref)

    lane = lax.broadcasted_iota(jnp.int32, (1, MXU_DIM), 1)
    in_head = [(lane >= D_HEAD * hh) & (lane < D_HEAD * (hh + 1)) for hh in range(B_GROUP)]
    tiles = []
    for tt in range(B_TILES):
        tile = i * B_TILES + tt
        start = pl.multiple_of(jnp.maximum(tile * B_QTILE - (B_WINDOW - B_QTILE), 0), B_QTILE)
        tiles.append(dict(
            rows=slice(tt * B_QTILE, (tt + 1) * B_QTILE),
            table=jnp.minimum(tile, B_TABLES - 1),
            kw=k_ref[pl.ds(start, B_WINDOW), :],
            vw=v_ref[pl.ds(start, B_WINDOW), :]))

    def logits(tt, hh):
        q = q_ref[tiles[tt]["rows"], :]
        s = lax.dot_general(jnp.where(in_head[hh], q, jnp.zeros_like(q)), tiles[tt]["kw"], _NT,
                            preferred_element_type=_F32)
        return s + mb_ref[hh, tiles[tt]["table"]]

    def attend(shifted):
        work = [(tt, hh) for tt in range(B_TILES) for hh in range(B_GROUP)]
        outs = [None] * B_TILES
        s = logits(*work[0])
        for idx, (tt, hh) in enumerate(work):
            s_next = logits(*work[idx + 1]) if idx + 1 < len(work) else None
            if shifted:
                s = s - jnp.max(s, axis=-1, keepdims=True)
            p = jnp.exp2(s)
            l = jnp.sum(p, axis=-1, keepdims=True)
            o = jnp.dot(p.astype(_BF), tiles[tt]["vw"], preferred_element_type=_F32) / l
            outs[tt] = o if outs[tt] is None else jnp.where(in_head[hh], o, outs[tt])
            s = s_next
        for tt in range(B_TILES):
            o_ref[tiles[tt]["rows"], :] = outs[tt].astype(_BF)

    bounded = bound_ref[0] <= SAFE_LOG2_SCORE
    pl.when(bounded)(lambda: attend(False))
    pl.when(jnp.logical_not(bounded))(lambda: attend(True))


def _chunk_attn(score_bound, bq, bk, bv, bias_gen, batch, seq):
    step_rows = B_TILES * B_QTILE
    nq = seq // step_rows
    t = batch * seq
    est = B_GROUP * B_TABLES * B_QTILE * B_WINDOW * 4 + 4 * B_QTILE * B_GEN_LEN * 4 \
        + 4 * seq * MXU_DIM * 2 + VMEM_TEMPORARIES
    return pl.pallas_call(
        _chunk_attn_kernel,
        grid=(batch, B_HEADS // B_GROUP, nq),
        in_specs=[
            pl.BlockSpec(memory_space=pltpu.SMEM),
            pl.BlockSpec((step_rows, MXU_DIM), lambda b, h, i: (b * nq + i, h)),
            pl.BlockSpec((seq, MXU_DIM), lambda b, h, i: (b, h)),
            pl.BlockSpec((seq, MXU_DIM), lambda b, h, i: (b, h)),
            pl.BlockSpec((B_GROUP, 1, B_GEN_LEN), lambda b, h, i: (h, 0, 0)),
        ],
        out_specs=pl.BlockSpec((step_rows, MXU_DIM), lambda b, h, i: (b * nq + i, h)),
        out_shape=jax.ShapeDtypeStruct((t, B_QKV), _BF),
        scratch_shapes=[pltpu.VMEM((B_GROUP, B_TABLES, B_QTILE, B_WINDOW), _F32)],
        compiler_params=pltpu.CompilerParams(
            dimension_semantics=("parallel", "parallel", "arbitrary"),
            vmem_limit_bytes=_vmem_limit(est)),
        name="chunk_attn",
    )(score_bound, bq, bk, bv, bias_gen)


def _bias_generator(rel_bias):
    b = rel_bias.astype(_F32) * LOG2E
    n_far = B_WINDOW - B_MAX_REL + 1
    mid = b[:, 1:CHUNK - 1 + B_MAX_REL][:, ::-1]
    n_neg = B_GEN_LEN - n_far - mid.shape[1]
    heads = b.shape[0]
    gen = jnp.concatenate([
        jnp.broadcast_to(b[:, -1:], (heads, n_far)), mid,
        jnp.broadcast_to(b[:, :1], (heads, n_neg))], axis=1)
    return gen[:, None, :]


def _retention_kernel(q_ref, kt_ref, v_ref, g_ref, cn_ref, o_ref, decay_ref, *, seq):
    blk = RET_BLOCK
    sub = RET_SUB
    h = pl.program_id(1)
    hf = jnp.full((1, 1), h, jnp.int32).astype(_F32)
    log_gamma = jnp.log(1.0 - jnp.exp2(-5.0 - hf))
    diff = (lax.broadcasted_iota(jnp.int32, (sub, sub), 0)
            - lax.broadcasted_iota(jnp.int32, (sub, sub), 1)).astype(_F32)
    decay_ref[...] = jnp.where(diff >= 0, jnp.exp(log_gamma * jnp.maximum(diff, 0.0)), 0.0)
    pos = lax.broadcasted_iota(jnp.int32, (sub, LANES), 0).astype(_F32)
    xi = jnp.exp(log_gamma * (pos + 1.0))
    zeta = jnp.exp(log_gamma * (sub - 1.0 - pos))
    sub_decay = jnp.exp(log_gamma * float(sub))
    odd = (h % 2) == 1
    q_keep = (lax.broadcasted_iota(jnp.int32, (1, LANES), 1) >= D_HEAD) == odd
    k_keep = (lax.broadcasted_iota(jnp.int32, (LANES, 1), 0) >= D_HEAD) == odd
    gain = cn_ref[0]

    def body(j, state):
        for s in range(blk // sub):
            rows = pl.ds(pl.multiple_of(j * blk + s * sub, sub), sub)
            qb = q_ref[rows, :]
            qb = jnp.where(q_keep, qb, jnp.zeros_like(qb))
            kt = kt_ref[j, :, s * sub:(s + 1) * sub]
            kt = jnp.where(k_keep, kt, jnp.zeros_like(kt))
            vb = v_ref[rows, :]
            sc = jnp.dot(qb, kt, preferred_element_type=_F32) * decay_ref[...]
            inner = jnp.dot(sc.astype(_BF), vb, preferred_element_type=_F32)
            cross = jnp.dot(qb, state.astype(_BF), preferred_element_type=_F32) * xi
            o = inner + cross
            vz = (vb.astype(_F32) * zeta).astype(_BF)
            state = state * sub_decay + jnp.dot(kt, vz, preferred_element_type=_F32)
            ms = jnp.mean(o * o, axis=-1, keepdims=True)
            y = o * lax.rsqrt(ms + EPS) * gain
            gg = g_ref[rows, :].astype(_F32)
            o_ref[rows, :] = (gg * jax.nn.sigmoid(gg) * y).astype(_BF)
        return state

    lax.fori_loop(0, seq // blk, body, jnp.zeros((LANES, C_DV), _F32), unroll=RET_UNROLL)


def _retention(cq, ckt, cv, cg, cnorm, batch, seq):
    blk = RET_BLOCK
    nb = seq // blk
    t = batch * seq
    est = 2 * 5 * seq * LANES * 2 + 6 * blk * blk * 4 + VMEM_TEMPORARIES
    return pl.pallas_call(
        functools.partial(_retention_kernel, seq=seq),
        grid=(batch, C_HEADS),
        in_specs=[
            pl.BlockSpec((seq, LANES), lambda b, h: (b, h // 2)),
            pl.BlockSpec((nb, LANES, blk), lambda b, h: (b, h // 2, 0)),
            pl.BlockSpec((seq, C_DV), lambda b, h: (b, h)),
            pl.BlockSpec((seq, C_DV), lambda b, h: (b, h)),
            pl.BlockSpec((1, 1, C_DV), lambda b, h: (h, 0, 0)),
        ],
        out_specs=pl.BlockSpec((seq, C_DV), lambda b, h: (b, h)),
        out_shape=jax.ShapeDtypeStruct((t, C_V), _BF),
        scratch_shapes=[pltpu.VMEM((RET_SUB, RET_SUB), _F32)],
        compiler_params=pltpu.CompilerParams(
            dimension_semantics=("parallel", "parallel"), vmem_limit_bytes=_vmem_limit(est)),
        name="retention",
    )(cq, ckt, cv, cg, cnorm)


def _merge_ffn_kernel(x_ref, ya_ref, yb_ref, yc_ref, gate_ref, wa_ref, wb_ref, wc_ref, wo_ref,
                      gain_ref, wg_ref, wu_ref, wd_ref, o_ref, acc_ref):
    merged = None
    for n, (y_ref, w_ref) in enumerate(((ya_ref, wa_ref), (yb_ref, wb_ref), (yc_ref, wc_ref))):
        br = jnp.dot(y_ref[...], w_ref[...], preferred_element_type=_F32)
        term = gate_ref[:, n * D_MODEL:(n + 1) * D_MODEL].astype(_F32) * br
        merged = term if merged is None else merged + term
    x = x_ref[...] + jnp.dot(merged.astype(_BF), wo_ref[...], preferred_element_type=_F32)
    h = _rms_rows(x, gain_ref[...]).astype(_BF)
    for c in range(D_FF // FF_CHUNK):
        sl = slice(c * FF_CHUNK, (c + 1) * FF_CHUNK)
        g = jnp.dot(h, wg_ref[:, sl], preferred_element_type=_F32)
        u = jnp.dot(h, wu_ref[:, sl], preferred_element_type=_F32)
        a = (g * jax.nn.sigmoid(g) * u).astype(_BF)
        part = jnp.dot(a, wd_ref[sl, :], preferred_element_type=_F32)
        if c == 0:
            acc_ref[...] = part
        else:
            acc_ref[...] += part
    o_ref[...] = x + 0.5 * acc_ref[...]


def _merge_ffn(layer, x, ya, yb, yc, gates, wa, wb, wc, wo, gain, wg, wu, wd):
    t = x.shape[0]
    tm = TOKEN_TILE
    row = lambda i: (i, 0)
    const = lambda i: (0, 0)
    est = (5 * D_MODEL * D_MODEL + 3 * D_MODEL * D_FF) * 2 \
        + 2 * tm * (2 * D_MODEL * 4 + (A_V + B_QKV + C_V) * 2 + N_BRANCH * D_MODEL * 2) \
        + 8 * tm * D_MODEL * 4 + 4 * tm * FF_CHUNK * 4 + VMEM_TEMPORARIES
    return pl.pallas_call(
        _merge_ffn_kernel,
        grid=(t // tm,),
        in_specs=[
            pl.BlockSpec((tm, D_MODEL), row),
            pl.BlockSpec((tm, A_V), row),
            pl.BlockSpec((tm, B_QKV), row),
            pl.BlockSpec((tm, C_V), row),
            pl.BlockSpec((tm, N_BRANCH * D_MODEL), row),
            _layer_weight(layer, A_V, D_MODEL),
            _layer_weight(layer, B_QKV, D_MODEL),
            _layer_weight(layer, C_V, D_MODEL),
            _layer_weight(layer, D_MODEL, D_MODEL),
            _resident((1, D_MODEL), const),
            _layer_weight(layer, D_MODEL, D_FF),
            _layer_weight(layer, D_MODEL, D_FF),
            _layer_weight(layer, D_FF, D_MODEL),
        ],
        out_specs=pl.BlockSpec((tm, D_MODEL), row),
        out_shape=jax.ShapeDtypeStruct((t, D_MODEL), _F32),
        scratch_shapes=[pltpu.VMEM((tm, D_MODEL), _F32)],
        compiler_params=pltpu.CompilerParams(
            dimension_semantics=("parallel",), vmem_limit_bytes=_vmem_limit(est)),
        name="merge_ffn",
    )(x, ya, yb, yc, gates, wa, wb, wc, wo, gain, wg, wu, wd)


def _rope_tables(seq):
    half = D_HEAD // 2
    inv = ROPE_THETA ** (-jnp.arange(half, dtype=_F32) / half)
    ang = jnp.arange(seq, dtype=_F32)[:, None] * inv[None, :]
    cos, sin = jnp.cos(ang), jnp.sin(ang)
    cos_t = jnp.concatenate([cos, cos] * (LANES // D_HEAD), axis=1)
    sin_t = jnp.concatenate([-sin, sin] * (LANES // D_HEAD), axis=1)
    return cos_t, sin_t


def _head_mean_matrix():
    g = np.kron(np.eye(MXU_DIM // D_HEAD), np.full((D_HEAD, D_HEAD), 1.0 / D_HEAD))
    return jnp.asarray(g, dtype=_BF)


def kernel(x, ffn1_norm, ffn1_w_gate, ffn1_w_up, ffn1_w_down, mix_norm, w_in, a_q_norm, a_k_norm, a_lambda_q1, a_lambda_k1, a_lambda_q2, a_lambda_k2, a_subln, b_q_norm, b_k_norm, b_rel_bias, c_out_norm, w_branch_a, w_branch_b, w_branch_c, w_out, ffn2_norm, ffn2_w_gate, ffn2_w_up, ffn2_w_down):
    batch, seq, d = x.shape
    assert d == D_MODEL and seq % ATTN_BLOCK == 0 and seq % RET_BLOCK == 0
    assert seq % TOKEN_TILE == 0 and seq >= B_WINDOW and ATTN_BLOCK == TOKEN_TILE == RET_BLOCK
    assert seq % (B_TILES * B_QTILE) == 0
    scale = D_HEAD ** -0.5
    cos_t, sin_t = _rope_tables(seq)
    gmat = _head_mean_matrix()
    bf = lambda w: w.astype(_BF)
    f2g, f2u, f2d = bf(ffn2_w_gate), bf(ffn2_w_up), bf(ffn2_w_down)
    win, wba, wbb, wbc, wo = bf(w_in), bf(w_branch_a), bf(w_branch_b), bf(w_branch_c), bf(w_out)
    xt = x.reshape(batch * seq, D_MODEL)
    for l in range(DEPTH):
        lam_init = 0.8 - 0.6 * math.exp(-0.3 * l)
        xt = _ffn(l, xt, ffn1_norm[l][None, :], ffn1_w_gate, ffn1_w_up, ffn1_w_down)

        heads = A_QK // D_HEAD
        qk_gains = jnp.stack([
            jnp.tile(a_q_norm[l].astype(_F32) * (scale * LOG2E), heads),
            jnp.tile(a_k_norm[l].astype(_F32), heads),
            jnp.tile(b_q_norm[l].astype(_F32) * (scale * LOG2E), heads),
            jnp.tile(b_k_norm[l].astype(_F32), heads),
        ])
        aq, ak, avt, bq, bk, bv, cq, ckt, cv, cg, gates = _in_proj(
            l, xt, mix_norm[l][None, :], win, cos_t, sin_t, qk_gains, gmat, seq)

        lamv = jnp.stack([a_lambda_q1[l], a_lambda_k1[l], a_lambda_q2[l], a_lambda_k2[l]]).astype(_F32)
        a_bound = _score_bound(qk_gains[0, :D_HEAD], qk_gains[1, :D_HEAD])
        ya = _diff_attn(a_bound, aq, ak, avt, lamv, a_subln[l].astype(_F32)[:, None], batch, seq, lam_init)
        bias_gen = _bias_generator(b_rel_bias[l])
        b_bound = _score_bound(qk_gains[2, :D_HEAD], qk_gains[3, :D_HEAD]) + jnp.max(jnp.abs(bias_gen))
        yb = _chunk_attn(b_bound, bq, bk, bv, bias_gen, batch, seq)
        yc = _retention(cq, ckt, cv, cg, c_out_norm[l].astype(_F32)[:, None, :], batch, seq)

        xt = _merge_ffn(l, xt, ya, yb, yc, gates, wba, wbb, wbc, wo, ffn2_norm[l][None, :],
                        f2g, f2u, f2d)
    return xt.reshape(batch, seq, D_MODEL)
```

```python
import functools
import math

import numpy as np
import jax
import jax.numpy as jnp
from jax import lax
from jax.experimental import pallas as pl
from jax.experimental.pallas import tpu as pltpu

D_MODEL = 1024
DEPTH = 2
CHUNK = 64
D_HEAD = 64
ROPE_THETA = 10000.0
EPS = 1e-6
A_HEADS = 4
A_DV = 2 * D_HEAD
B_HEADS = 8
B_LOOKBACK = 8
B_MAX_REL = 256
C_HEADS = 4
C_DV = 2 * D_HEAD
N_BRANCH = 3
D_FF = 2816

A_QK = A_HEADS * 2 * D_HEAD
A_V = A_HEADS * A_DV
B_QKV = B_HEADS * D_HEAD
C_QK = C_HEADS * D_HEAD
C_V = C_HEADS * C_DV
SPLITS = (A_QK, A_QK, A_V, B_QKV, B_QKV, B_QKV, C_QK, C_QK, C_V, C_V, N_BRANCH * D_MODEL)
IN_COLS = sum(SPLITS)
_OFFS = tuple(int(v) for v in np.cumsum((0,) + SPLITS))

LANES = 128
MXU_DIM = 256
VMEM_BYTES_V7X = 64 * 1024 * 1024
VMEM_HEADROOM = 6 * 1024 * 1024
VMEM_DEFAULT_LIMIT = 32 * 1024 * 1024
VMEM_TEMPORARIES = 12 * 1024 * 1024

TOKEN_TILE = 512
FF_CHUNK = MXU_DIM
PROJ_CHUNK = 512
ATTN_BLOCK = 512
B_QTILE = 256
B_WINDOW = B_QTILE + B_LOOKBACK * CHUNK
RET_BLOCK = 512
RET_SUB = 256
RET_UNROLL = 16
NEG_BIG = -1e30

_BF = jnp.bfloat16
_F32 = jnp.float32
_NT = (((1,), (1,)), ((), ()))


def _vmem_limit(estimate_bytes):
    return int(min(VMEM_BYTES_V7X - VMEM_HEADROOM, max(VMEM_DEFAULT_LIMIT, estimate_bytes)))


def _resident(shape, index_map):
    return pl.BlockSpec(shape, index_map, pipeline_mode=pl.Buffered(1))


def _layer_weight(layer, rows, cols):
    return pl.BlockSpec((None, rows, cols), lambda i: (layer, 0, 0), pipeline_mode=pl.Buffered(1))


def _rms_rows(x, gain):
    ms = jnp.mean(x * x, axis=-1, keepdims=True)
    return x * lax.rsqrt(ms + EPS) * gain


def _ffn_kernel(x_ref, gain_ref, wg_ref, wu_ref, wd_ref, o_ref, acc_ref):
    x = x_ref[...]
    h = _rms_rows(x, gain_ref[...]).astype(_BF)
    for c in range(D_FF // FF_CHUNK):
        sl = slice(c * FF_CHUNK, (c + 1) * FF_CHUNK)
        g = jnp.dot(h, wg_ref[:, sl].astype(_BF), preferred_element_type=_F32)
        u = jnp.dot(h, wu_ref[:, sl].astype(_BF), preferred_element_type=_F32)
        a = (g * jax.nn.sigmoid(g) * u).astype(_BF)
        part = jnp.dot(a, wd_ref[sl, :].astype(_BF), preferred_element_type=_F32)
        if c == 0:
            acc_ref[...] = part
        else:
            acc_ref[...] += part
    o_ref[...] = x + 0.5 * acc_ref[...]


def _ffn(layer, x, gain, wg, wu, wd):
    t = x.shape[0]
    tm = TOKEN_TILE
    est = 3 * D_MODEL * D_FF * wg.dtype.itemsize + 5 * tm * D_MODEL * 4 + 4 * tm * FF_CHUNK * 4 \
        + VMEM_TEMPORARIES
    return pl.pallas_call(
        _ffn_kernel,
        grid=(t // tm,),
        in_specs=[
            pl.BlockSpec((tm, D_MODEL), lambda i: (i, 0)),
            _resident((1, D_MODEL), lambda i: (0, 0)),
            _layer_weight(layer, D_MODEL, D_FF),
            _layer_weight(layer, D_MODEL, D_FF),
            _layer_weight(layer, D_FF, D_MODEL),
        ],
        out_specs=pl.BlockSpec((tm, D_MODEL), lambda i: (i, 0)),
        out_shape=jax.ShapeDtypeStruct((t, D_MODEL), _F32),
        scratch_shapes=[pltpu.VMEM((tm, D_MODEL), _F32)],
        compiler_params=pltpu.CompilerParams(
            dimension_semantics=("parallel",), vmem_limit_bytes=_vmem_limit(est)),
        name="ffn",
    )(x, gain, wg, wu, wd)


def _swap_halves(x):
    lane = lax.broadcasted_iota(jnp.int32, (1, LANES), 1)
    upper = (lane & (D_HEAD // 2)) != 0
    outs = []
    for s in range(x.shape[1] // LANES):
        xs = x[:, s * LANES:(s + 1) * LANES]
        from_below = pltpu.roll(xs, D_HEAD // 2, 1)
        from_above = pltpu.roll(xs, LANES - D_HEAD // 2, 1)
        outs.append(jnp.where(upper, from_below, from_above))
    return outs[0] if len(outs) == 1 else jnp.concatenate(outs, axis=1)


def _tile_lanes(t, n):
    reps = n // t.shape[1]
    return t if reps == 1 else jnp.concatenate([t] * reps, axis=1)


def _rope(x, cos, sin):
    n = x.shape[1]
    return x * _tile_lanes(cos, n) + _swap_halves(x) * _tile_lanes(sin, n)


def _head_ms(p, gmat):
    sq = (p * p).astype(_BF)
    outs = [jnp.dot(sq[:, s:s + MXU_DIM], gmat, preferred_element_type=_F32)
            for s in range(0, p.shape[1], MXU_DIM)]
    return outs[0] if len(outs) == 1 else jnp.concatenate(outs, axis=1)


def _in_proj_kernel(x_ref, gain_ref, w_ref, cos_ref, sin_ref, qkg_ref, gmat_ref,
                    aq_ref, ak_ref, avt_ref, bq_ref, bk_ref, bv_ref,
                    cq_ref, ckt_ref, cv_ref, cg_ref, gate_ref):
    h = _rms_rows(x_ref[...], gain_ref[...]).astype(_BF)
    cos = cos_ref[...]
    sin = sin_ref[...]
    gmat = gmat_ref[...]

    def proj(lo, width):
        return jnp.dot(h, w_ref[:, lo:lo + width].astype(_BF), preferred_element_type=_F32)

    def normed(p, row):
        return p * lax.rsqrt(_head_ms(p, gmat) + EPS) * qkg_ref[row:row + 1, :]

    for c in range(N_BRANCH * D_MODEL // PROJ_CHUNK):
        lo = c * PROJ_CHUNK
        gate_ref[:, lo:lo + PROJ_CHUNK] = jax.nn.sigmoid(
            proj(_OFFS[10] + lo, PROJ_CHUNK)).astype(_BF)
    aq_ref[...] = _rope(normed(proj(_OFFS[0], A_QK), 0), cos, sin).astype(_BF)
    ak_ref[...] = _rope(normed(proj(_OFFS[1], A_QK), 1), cos, sin).astype(_BF)
    bq_ref[...] = normed(proj(_OFFS[3], B_QKV), 2).astype(_BF)
    bk_ref[...] = normed(proj(_OFFS[4], B_QKV), 3).astype(_BF)
    cq_ref[...] = _rope(proj(_OFFS[6], C_QK), cos, sin).astype(_BF)
    ckt_ref[0] = (_rope(proj(_OFFS[7], C_QK), cos, sin) * (D_HEAD ** -0.5)).T.astype(_BF)
    avt_ref[0] = proj(_OFFS[2], A_V).T.astype(_BF)
    bv_ref[...] = proj(_OFFS[5], B_QKV).astype(_BF)
    cv_ref[...] = proj(_OFFS[8], C_V).astype(_BF)
    cg_ref[...] = proj(_OFFS[9], C_V).astype(_BF)


def _in_proj(layer, x, gain, w, cos_t, sin_t, qk_gains, gmat, seq):
    t = x.shape[0]
    tm = TOKEN_TILE
    nt = t // tm
    pos_blocks = seq // tm
    row = lambda i: (i, 0)
    tok_major = lambda n: pl.BlockSpec((tm, n), row)
    out_shapes = (
        jax.ShapeDtypeStruct((t, A_QK), _BF),
        jax.ShapeDtypeStruct((t, A_QK), _BF),
        jax.ShapeDtypeStruct((nt, A_V, tm), _BF),
        jax.ShapeDtypeStruct((t, B_QKV), _BF),
        jax.ShapeDtypeStruct((t, B_QKV), _BF),
        jax.ShapeDtypeStruct((t, B_QKV), _BF),
        jax.ShapeDtypeStruct((t, C_QK), _BF),
        jax.ShapeDtypeStruct((nt, C_QK, tm), _BF),
        jax.ShapeDtypeStruct((t, C_V), _BF),
        jax.ShapeDtypeStruct((t, C_V), _BF),
        jax.ShapeDtypeStruct((t, N_BRANCH * D_MODEL), _BF),
    )
    out_specs = (
        tok_major(A_QK), tok_major(A_QK),
        pl.BlockSpec((1, A_V, tm), lambda i: (i, 0, 0)),
        tok_major(B_QKV), tok_major(B_QKV), tok_major(B_QKV),
        tok_major(C_QK),
        pl.BlockSpec((1, C_QK, tm), lambda i: (i, 0, 0)),
        tok_major(C_V), tok_major(C_V), tok_major(N_BRANCH * D_MODEL),
    )
    est = D_MODEL * IN_COLS * w.dtype.itemsize + 2 * tm * IN_COLS * 2 + 4 * tm * D_MODEL * 4 \
        + VMEM_TEMPORARIES
    return pl.pallas_call(
        _in_proj_kernel,
        grid=(nt,),
        in_specs=[
            pl.BlockSpec((tm, D_MODEL), row),
            _resident((1, D_MODEL), lambda i: (0, 0)),
            _layer_weight(layer, D_MODEL, IN_COLS),
            pl.BlockSpec((tm, LANES), lambda i: (i % pos_blocks, 0)),
            pl.BlockSpec((tm, LANES), lambda i: (i % pos_blocks, 0)),
            _resident((4, A_QK), lambda i: (0, 0)),
            _resident((MXU_DIM, MXU_DIM), lambda i: (0, 0)),
        ],
        out_specs=out_specs,
        out_shape=out_shapes,
        compiler_params=pltpu.CompilerParams(
            dimension_semantics=("parallel",), vmem_limit_bytes=_vmem_limit(est)),
        name="in_proj",
    )(x, gain, w, cos_t, sin_t, qk_gains, gmat)


LOG2E = math.log2(math.e)
SAFE_LOG2_SCORE = 60.0
NORM_MARGIN = 1.05
A_PAIR = 4


def _diff_attn_kernel(bound_ref, q_ref, k_ref, vt_ref, lamv_ref, subln_ref, o_ref, acc_ref, *,
                      lam_init):
    blk = ATTN_BLOCK
    i = pl.program_id(2)
    n_chain = 2 * A_PAIR
    per_group = MXU_DIM // D_HEAD
    lane = lax.broadcasted_iota(jnp.int32, (1, MXU_DIM), 1)
    qs = []
    for n in range(n_chain):
        g, r = divmod(n, per_group)
        qg = q_ref[:, g * MXU_DIM:(g + 1) * MXU_DIM]
        qs.append(jnp.where((lane >= D_HEAD * r) & (lane < D_HEAD * (r + 1)), qg,
                            jnp.zeros_like(qg)))
    half = blk // 2

    def scores(j, n, diagonal):
        g = n // per_group
        kb = k_ref[pl.ds(pl.multiple_of(j * blk, blk), blk), g * MXU_DIM:(g + 1) * MXU_DIM]
        st = lax.dot_general(kb, qs[n], _NT, preferred_element_type=_F32)
        if diagonal:
            key_chunk = lax.broadcasted_iota(jnp.int32, (blk, blk), 0) // CHUNK
            qry_chunk = lax.broadcasted_iota(jnp.int32, (blk, blk), 1) // CHUNK
            st = jnp.where(key_chunk <= qry_chunk, st, NEG_BIG)
        return st

    def diagonal_scores(j, n):
        g = n // per_group
        row0 = pl.multiple_of(j * blk, blk)
        cols = slice(g * MXU_DIM, (g + 1) * MXU_DIM)
        key_chunk = lax.broadcasted_iota(jnp.int32, (half, blk), 0) // CHUNK
        qry_chunk = lax.broadcasted_iota(jnp.int32, (half, blk), 1) // CHUNK
        first = lax.dot_general(k_ref[pl.ds(row0, half), cols], qs[n], _NT,
                                preferred_element_type=_F32)
        first = jnp.where(key_chunk <= qry_chunk, first, NEG_BIG)
        second = lax.dot_general(k_ref[pl.ds(row0 + half, half), cols], qs[n][half:, :], _NT,
                                 preferred_element_type=_F32)
        key_chunk2 = lax.broadcasted_iota(jnp.int32, (half, half), 0) // CHUNK
        qry_chunk2 = lax.broadcasted_iota(jnp.int32, (half, half), 1) // CHUNK
        second = jnp.where(key_chunk2 <= qry_chunk2, second, NEG_BIG)
        return first, second

    def values_t(j, n):
        head = n // 2
        return vt_ref[j, head * A_DV:(head + 1) * A_DV, :]

    def plain_steps(blocks, carry):
        out = list(carry)
        work = [(j, diagonal, list(range(g * per_group, (g + 1) * per_group)))
                for j, diagonal in blocks for g in range(n_chain // per_group)]

        def issue(j, diagonal, chains):
            return {n: (diagonal_scores(j, n) if diagonal else scores(j, n, False))
                    for n in chains}

        pending = issue(*work[0])
        for idx, (j, diagonal, chains) in enumerate(work):
            sts = pending
            if idx + 1 < len(work):
                pending = issue(*work[idx + 1])
            for n in chains:
                if diagonal:
                    first, second = (jnp.exp2(s) for s in sts[n])
                    widened = jnp.concatenate([jnp.zeros_like(second), second], axis=1)
                    out[n] = (out[n] + jnp.sum(first, axis=0, keepdims=True)
                              + jnp.sum(widened, axis=0, keepdims=True))
                    vt = values_t(j, n)
                    acc_ref[n] += jnp.dot(vt[:, :half], first.astype(_BF),
                                          preferred_element_type=_F32)
                    acc_ref[n, :, half:] += jnp.dot(vt[:, half:], second.astype(_BF),
                                                    preferred_element_type=_F32)
                else:
                    p = jnp.exp2(sts[n])
                    out[n] = out[n] + jnp.sum(p, axis=0, keepdims=True)
                    acc_ref[n] += jnp.dot(values_t(j, n), p.astype(_BF),
                                          preferred_element_type=_F32)
        return tuple(out)

    def shifted_step(j, carry, diagonal):
        out = []
        for n in range(n_chain):
            m_old, l_old = carry[n]
            st = scores(j, n, diagonal)
            m_new = jnp.maximum(m_old, jnp.max(st, axis=0, keepdims=True))
            alpha = jnp.exp2(m_old - m_new)
            p = jnp.exp2(st - m_new)
            l_new = alpha * l_old + jnp.sum(p, axis=0, keepdims=True)
            pv = jnp.dot(values_t(j, n), p.astype(_BF), preferred_element_type=_F32)
            acc_ref[n] = alpha * acc_ref[n] + pv
            out.append((m_new, l_new))
        return tuple(out)

    def finish(ls):
        lamv = lamv_ref[...]
        lam = (jnp.exp(jnp.sum(lamv[0:1] * lamv[1:2], axis=-1, keepdims=True))
               - jnp.exp(jnp.sum(lamv[2:3] * lamv[3:4], axis=-1, keepdims=True)) + lam_init)
        for head in range(A_PAIR):
            o = (acc_ref[2 * head] / ls[2 * head]
                 - lam * (acc_ref[2 * head + 1] / ls[2 * head + 1]))
            ms = jnp.mean(o * o, axis=0, keepdims=True)
            y = o * lax.rsqrt(ms + EPS) * subln_ref[...] * (1.0 - lam_init)
            o_ref[:, head * A_DV:(head + 1) * A_DV] = y.T.astype(_BF)

    acc_ref[...] = jnp.zeros_like(acc_ref)
    bounded = bound_ref[0] <= SAFE_LOG2_SCORE

    @pl.when(bounded)
    def _():
        init = tuple(jnp.zeros((1, blk), _F32) for _ in range(n_chain))
        carry = lax.fori_loop(
            0, i // 2, lambda jj, c: plain_steps([(2 * jj, False), (2 * jj + 1, False)], c), init)
        odd = (i % 2) == 1
        pl.when(odd)(lambda: finish(plain_steps([(i - 1, False), (i, True)], carry)))
        pl.when(jnp.logical_not(odd))(lambda: finish(plain_steps([(i, True)], carry)))

    @pl.when(jnp.logical_not(bounded))
    def _():
        init = tuple((jnp.full((1, blk), NEG_BIG, _F32), jnp.zeros((1, blk), _F32))
                     for _ in range(n_chain))
        carry = lax.fori_loop(0, i, lambda j, c: shifted_step(j, c, False), init)
        finish([l for _, l in shifted_step(i, carry, True)])


def _diff_attn(score_bound, aq, ak, avt, lamv, subln_col, batch, seq, lam_init):
    blk = ATTN_BLOCK
    nq = seq // blk
    t = batch * seq
    width = A_PAIR * A_DV
    est = 2 * (seq * width * 2) * 2 + 2 * A_PAIR * (A_DV * blk * 4) + 16 * blk * blk * 4 \
        + VMEM_TEMPORARIES
    return pl.pallas_call(
        functools.partial(_diff_attn_kernel, lam_init=lam_init),
        grid=(batch, A_HEADS // A_PAIR, nq),
        in_specs=[
            pl.BlockSpec(memory_space=pltpu.SMEM),
            pl.BlockSpec((blk, width), lambda b, h, i: (b * nq + i, h)),
            pl.BlockSpec((seq, width), lambda b, h, i: (b, h)),
            pl.BlockSpec((nq, width, blk), lambda b, h, i: (b, h, 0)),
            pl.BlockSpec((4, D_HEAD), lambda b, h, i: (0, 0)),
            pl.BlockSpec((A_DV, 1), lambda b, h, i: (0, 0)),
        ],
        out_specs=pl.BlockSpec((blk, width), lambda b, h, i: (b * nq + i, h)),
        out_shape=jax.ShapeDtypeStruct((t, A_V), _BF),
        scratch_shapes=[pltpu.VMEM((2 * A_PAIR, A_DV, blk), _F32)],
        compiler_params=pltpu.CompilerParams(
            dimension_semantics=("parallel", "parallel", "parallel"),
            vmem_limit_bytes=_vmem_limit(est)),
        name="diff_attn",
    )(score_bound, aq, ak, avt, lamv, subln_col)


def _score_bound(q_gain, k_gain):
    return (D_HEAD * NORM_MARGIN * jnp.max(jnp.abs(q_gain)) * jnp.max(jnp.abs(k_gain))).reshape(1)


B_TABLES = (B_WINDOW - B_QTILE) // B_QTILE + 1
B_GEN_LEN = 2048
B_GROUP = MXU_DIM // D_HEAD


def _build_bias_tables(gen_ref, mb_ref):
    qchunk = lax.broadcasted_iota(jnp.int32, (B_QTILE, B_WINDOW), 0) // CHUNK
    kchunk = lax.broadcasted_iota(jnp.int32, (B_QTILE, B_WINDOW), 1) // CHUNK
    for hh in range(B_GROUP):
        rolled = pltpu.roll(jnp.broadcast_to(gen_ref[hh], (B_QTILE, B_GEN_LEN)), 0, 1,
                            stride=1, stride_axis=0)
        for t in range(B_TABLES):
            lo = B_WINDOW - t * B_QTILE
            dchunk = qchunk - kchunk + (t * B_QTILE) // CHUNK
            visible = (dchunk >= 0) & (dchunk <= B_LOOKBACK)
            mb_ref[hh, t] = jnp.where(visible, rolled[:, lo:lo + B_WINDOW], NEG_BIG)


B_TILES = 4


def _chunk_attn_kernel(bound_ref, q_ref, k_ref, v_ref, gen_ref, o_ref, mb_ref):
    i = pl.program_id(2)

    @pl.when(i == 0)
    def _():
        _build_bias_tables(gen_ref, mb_ref)

    lane = lax.broadcasted_iota(jnp.int32, (1, MXU_DIM), 1)
    in_head = [(lane >= D_HEAD * hh) & (lane < D_HEAD * (hh + 1)) for hh in range(B_GROUP)]
    tiles = []
    for tt in range(B_TILES):
        tile = i * B_TILES + tt
        start = pl.multiple_of(jnp.maximum(tile * B_QTILE - (B_WINDOW - B_QTILE), 0), B_QTILE)
        tiles.append(dict(
            rows=slice(tt * B_QTILE, (tt + 1) * B_QTILE),
            table=jnp.minimum(tile, B_TABLES - 1),
            kw=k_ref[pl.ds(start, B_WINDOW), :],
            vw=v_ref[pl.ds(start, B_WINDOW), :]))

    def logits(tt, hh):
        q = q_ref[tiles[tt]["rows"], :]
        s = lax.dot_general(jnp.where(in_head[hh], q, jnp.zeros_like(q)), tiles[tt]["kw"], _NT,
                            preferred_element_type=_F32)
        return s + mb_ref[hh, tiles[tt]["table"]]

    def attend(shifted):
        work = [(tt, hh) for tt in range(B_TILES) for hh in range(B_GROUP)]
        outs = [None] * B_TILES
        s = logits(*work[0])
        for idx, (tt, hh) in enumerate(work):
            s_next = logits(*work[idx + 1]) if idx + 1 < len(work) else None
            if shifted:
                s = s - jnp.max(s, axis=-1, keepdims=True)
            p = jnp.exp2(s)
            l = jnp.sum(p, axis=-1, keepdims=True)
            o = jnp.dot(p.astype(_BF), tiles[tt]["vw"], preferred_element_type=_F32) / l
            outs[tt] = o if outs[tt] is None else jnp.where(in_head[hh], o, outs[tt])
            s = s_next
        for tt in range(B_TILES):
            o_ref[tiles[tt]["rows"], :] = outs[tt].astype(_BF)

    bounded = bound_ref[0] <= SAFE_LOG2_SCORE
    pl.when(bounded)(lambda: attend(False))
    pl.when(jnp.logical_not(bounded))(lambda: attend(True))


def _chunk_attn(score_bound, bq, bk, bv, bias_gen, batch, seq):
    step_rows = B_TILES * B_QTILE
    nq = seq // step_rows
    t = batch * seq
    est = B_GROUP * B_TABLES * B_QTILE * B_WINDOW * 4 + 4 * B_QTILE * B_GEN_LEN * 4 \
        + 4 * seq * MXU_DIM * 2 + VMEM_TEMPORARIES
    return pl.pallas_call(
        _chunk_attn_kernel,
        grid=(batch, B_HEADS // B_GROUP, nq),
        in_specs=[
            pl.BlockSpec(memory_space=pltpu.SMEM),
            pl.BlockSpec((step_rows, MXU_DIM), lambda b, h, i: (b * nq + i, h)),
            pl.BlockSpec((seq, MXU_DIM), lambda b, h, i: (b, h)),
            pl.BlockSpec((seq, MXU_DIM), lambda b, h, i: (b, h)),
            pl.BlockSpec((B_GROUP, 1, B_GEN_LEN), lambda b, h, i: (h, 0, 0)),
        ],
        out_specs=pl.BlockSpec((step_rows, MXU_DIM), lambda b, h, i: (b * nq + i, h)),
        out_shape=jax.ShapeDtypeStruct((t, B_QKV), _BF),
        scratch_shapes=[pltpu.VMEM((B_GROUP, B_TABLES, B_QTILE, B_WINDOW), _F32)],
        compiler_params=pltpu.CompilerParams(
            dimension_semantics=("parallel", "parallel", "arbitrary"),
            vmem_limit_bytes=_vmem_limit(est)),
        name="chunk_attn",
    )(score_bound, bq, bk, bv, bias_gen)


def _bias_generator(rel_bias):
    b = rel_bias.astype(_F32) * LOG2E
    n_far = B_WINDOW - B_MAX_REL + 1
    mid = b[:, 1:CHUNK - 1 + B_MAX_REL][:, ::-1]
    n_neg = B_GEN_LEN - n_far - mid.shape[1]
    heads = b.shape[0]
    gen = jnp.concatenate([
        jnp.broadcast_to(b[:, -1:], (heads, n_far)), mid,
        jnp.broadcast_to(b[:, :1], (heads, n_neg))], axis=1)
    return gen[:, None, :]


def _retention_kernel(q_ref, kt_ref, v_ref, g_ref, cn_ref, o_ref, decay_ref, *, seq):
    blk = RET_BLOCK
    sub = RET_SUB
    h = pl.program_id(1)
    hf = jnp.full((1, 1), h, jnp.int32).astype(_F32)
    log_gamma = jnp.log(1.0 - jnp.exp2(-5.0 - hf))
    diff = (lax.broadcasted_iota(jnp.int32, (sub, sub), 0)
            - lax.broadcasted_iota(jnp.int32, (sub, sub), 1)).astype(_F32)
    decay_ref[...] = jnp.where(diff >= 0, jnp.exp(log_gamma * jnp.maximum(diff, 0.0)), 0.0)
    pos = lax.broadcasted_iota(jnp.int32, (sub, LANES), 0).astype(_F32)
    xi = jnp.exp(log_gamma * (pos + 1.0))
    zeta = jnp.exp(log_gamma * (sub - 1.0 - pos))
    sub_decay = jnp.exp(log_gamma * float(sub))
    odd = (h % 2) == 1
    q_keep = (lax.broadcasted_iota(jnp.int32, (1, LANES), 1) >= D_HEAD) == odd
    k_keep = (lax.broadcasted_iota(jnp.int32, (LANES, 1), 0) >= D_HEAD) == odd
    gain = cn_ref[0]

    def body(j, state):
        for s in range(blk // sub):
            rows = pl.ds(pl.multiple_of(j * blk + s * sub, sub), sub)
            qb = q_ref[rows, :]
            qb = jnp.where(q_keep, qb, jnp.zeros_like(qb))
            kt = kt_ref[j, :, s * sub:(s + 1) * sub]
            kt = jnp.where(k_keep, kt, jnp.zeros_like(kt))
            vb = v_ref[rows, :]
            sc = jnp.dot(qb, kt, preferred_element_type=_F32) * decay_ref[...]
            inner = jnp.dot(sc.astype(_BF), vb, preferred_element_type=_F32)
            cross = jnp.dot(qb, state.astype(_BF), preferred_element_type=_F32) * xi
            o = inner + cross
            vz = (vb.astype(_F32) * zeta).astype(_BF)
            state = state * sub_decay + jnp.dot(kt, vz, preferred_element_type=_F32)
            ms = jnp.mean(o * o, axis=-1, keepdims=True)
            y = o * lax.rsqrt(ms + EPS) * gain
            gg = g_ref[rows, :].astype(_F32)
            o_ref[rows, :] = (gg * jax.nn.sigmoid(gg) * y).astype(_BF)
        return state

    lax.fori_loop(0, seq // blk, body, jnp.zeros((LANES, C_DV), _F32), unroll=RET_UNROLL)


def _retention(cq, ckt, cv, cg, cnorm, batch, seq):
    blk = RET_BLOCK
    nb = seq // blk
    t = batch * seq
    est = 2 * 5 * seq * LANES * 2 + 6 * blk * blk * 4 + VMEM_TEMPORARIES
    return pl.pallas_call(
        functools.partial(_retention_kernel, seq=seq),
        grid=(batch, C_HEADS),
        in_specs=[
            pl.BlockSpec((seq, LANES), lambda b, h: (b, h // 2)),
            pl.BlockSpec((nb, LANES, blk), lambda b, h: (b, h // 2, 0)),
            pl.BlockSpec((seq, C_DV), lambda b, h: (b, h)),
            pl.BlockSpec((seq, C_DV), lambda b, h: (b, h)),
            pl.BlockSpec((1, 1, C_DV), lambda b, h: (h, 0, 0)),
        ],
        out_specs=pl.BlockSpec((seq, C_DV), lambda b, h: (b, h)),
        out_shape=jax.ShapeDtypeStruct((t, C_V), _BF),
        scratch_shapes=[pltpu.VMEM((RET_SUB, RET_SUB), _F32)],
        compiler_params=pltpu.CompilerParams(
            dimension_semantics=("parallel", "parallel"), vmem_limit_bytes=_vmem_limit(est)),
        name="retention",
    )(cq, ckt, cv, cg, cnorm)


def _merge_ffn_kernel(x_ref, ya_ref, yb_ref, yc_ref, gate_ref, wa_ref, wb_ref, wc_ref, wo_ref,
                      gain_ref, wg_ref, wu_ref, wd_ref, o_ref, acc_ref):
    merged = None
    for n, (y_ref, w_ref) in enumerate(((ya_ref, wa_ref), (yb_ref, wb_ref), (yc_ref, wc_ref))):
        br = jnp.dot(y_ref[...], w_ref[...], preferred_element_type=_F32)
        term = gate_ref[:, n * D_MODEL:(n + 1) * D_MODEL].astype(_F32) * br
        merged = term if merged is None else merged + term
    x = x_ref[...] + jnp.dot(merged.astype(_BF), wo_ref[...], preferred_element_type=_F32)
    h = _rms_rows(x, gain_ref[...]).astype(_BF)
    for c in range(D_FF // FF_CHUNK):
        sl = slice(c * FF_CHUNK, (c + 1) * FF_CHUNK)
        g = jnp.dot(h, wg_ref[:, sl], preferred_element_type=_F32)
        u = jnp.dot(h, wu_ref[:, sl], preferred_element_type=_F32)
        a = (g * jax.nn.sigmoid(g) * u).astype(_BF)
        part = jnp.dot(a, wd_ref[sl, :], preferred_element_type=_F32)
        if c == 0:
            acc_ref[...] = part
        else:
            acc_ref[...] += part
    o_ref[...] = x + 0.5 * acc_ref[...]


def _merge_ffn(layer, x, ya, yb, yc, gates, wa, wb, wc, wo, gain, wg, wu, wd):
    t = x.shape[0]
    tm = TOKEN_TILE
    row = lambda i: (i, 0)
    const = lambda i: (0, 0)
    est = (5 * D_MODEL * D_MODEL + 3 * D_MODEL * D_FF) * 2 \
        + 2 * tm * (2 * D_MODEL * 4 + (A_V + B_QKV + C_V) * 2 + N_BRANCH * D_MODEL * 2) \
        + 8 * tm * D_MODEL * 4 + 4 * tm * FF_CHUNK * 4 + VMEM_TEMPORARIES
    return pl.pallas_call(
        _merge_ffn_kernel,
        grid=(t // tm,),
        in_specs=[
            pl.BlockSpec((tm, D_MODEL), row),
            pl.BlockSpec((tm, A_V), row),
            pl.BlockSpec((tm, B_QKV), row),
            pl.BlockSpec((tm, C_V), row),
            pl.BlockSpec((tm, N_BRANCH * D_MODEL), row),
            _layer_weight(layer, A_V, D_MODEL),
            _layer_weight(layer, B_QKV, D_MODEL),
            _layer_weight(layer, C_V, D_MODEL),
            _layer_weight(layer, D_MODEL, D_MODEL),
            _resident((1, D_MODEL), const),
            _layer_weight(layer, D_MODEL, D_FF),
            _layer_weight(layer, D_MODEL, D_FF),
            _layer_weight(layer, D_FF, D_MODEL),
        ],
        out_specs=pl.BlockSpec((tm, D_MODEL), row),
        out_shape=jax.ShapeDtypeStruct((t, D_MODEL), _F32),
        scratch_shapes=[pltpu.VMEM((tm, D_MODEL), _F32)],
        compiler_params=pltpu.CompilerParams(
            dimension_semantics=("parallel",), vmem_limit_bytes=_vmem_limit(est)),
        name="merge_ffn",
    )(x, ya, yb, yc, gates, wa, wb, wc, wo, gain, wg, wu, wd)


def _rope_tables(seq):
    half = D_HEAD // 2
    inv = ROPE_THETA ** (-jnp.arange(half, dtype=_F32) / half)
    ang = jnp.arange(seq, dtype=_F32)[:, None] * inv[None, :]
    cos, sin = jnp.cos(ang), jnp.sin(ang)
    cos_t = jnp.concatenate([cos, cos] * (LANES // D_HEAD), axis=1)
    sin_t = jnp.concatenate([-sin, sin] * (LANES // D_HEAD), axis=1)
    return cos_t, sin_t


def _head_mean_matrix():
    g = np.kron(np.eye(MXU_DIM // D_HEAD), np.full((D_HEAD, D_HEAD), 1.0 / D_HEAD))
    return jnp.asarray(g, dtype=_BF)


def kernel(x, ffn1_norm, ffn1_w_gate, ffn1_w_up, ffn1_w_down, mix_norm, w_in, a_q_norm, a_k_norm, a_lambda_q1, a_lambda_k1, a_lambda_q2, a_lambda_k2, a_subln, b_q_norm, b_k_norm, b_rel_bias, c_out_norm, w_branch_a, w_branch_b, w_branch_c, w_out, ffn2_norm, ffn2_w_gate, ffn2_w_up, ffn2_w_down):
    batch, seq, d = x.shape
    assert d == D_MODEL and seq % ATTN_BLOCK == 0 and seq % RET_BLOCK == 0
    assert seq % TOKEN_TILE == 0 and seq >= B_WINDOW and ATTN_BLOCK == TOKEN_TILE == RET_BLOCK
    assert seq % (B_TILES * B_QTILE) == 0
    scale = D_HEAD ** -0.5
    cos_t, sin_t = _rope_tables(seq)
    gmat = _head_mean_matrix()
    bf = lambda w: w.astype(_BF)
    f2g, f2u, f2d = bf(ffn2_w_gate), bf(ffn2_w_up), bf(ffn2_w_down)
    wba, wbb, wbc, wo = bf(w_branch_a), bf(w_branch_b), bf(w_branch_c), bf(w_out)
    xt = x.reshape(batch * seq, D_MODEL)
    for l in range(DEPTH):
        lam_init = 0.8 - 0.6 * math.exp(-0.3 * l)
        xt = _ffn(l, xt, ffn1_norm[l][None, :], ffn1_w_gate, ffn1_w_up, ffn1_w_down)

        heads = A_QK // D_HEAD
        qk_gains = jnp.stack([
            jnp.tile(a_q_norm[l].astype(_F32) * (scale * LOG2E), heads),
            jnp.tile(a_k_norm[l].astype(_F32), heads),
            jnp.tile(b_q_norm[l].astype(_F32) * (scale * LOG2E), heads),
            jnp.tile(b_k_norm[l].astype(_F32), heads),
        ])
        aq, ak, avt, bq, bk, bv, cq, ckt, cv, cg, gates = _in_proj(
            l, xt, mix_norm[l][None, :], w_in, cos_t, sin_t, qk_gains, gmat, seq)

        lamv = jnp.stack([a_lambda_q1[l], a_lambda_k1[l], a_lambda_q2[l], a_lambda_k2[l]]).astype(_F32)
        a_bound = _score_bound(qk_gains[0, :D_HEAD], qk_gains[1, :D_HEAD])
        ya = _diff_attn(a_bound, aq, ak, avt, lamv, a_subln[l].astype(_F32)[:, None], batch, seq, lam_init)
        bias_gen = _bias_generator(b_rel_bias[l])
        b_bound = _score_bound(qk_gains[2, :D_HEAD], qk_gains[3, :D_HEAD]) + jnp.max(jnp.abs(bias_gen))
        yb = _chunk_attn(b_bound, bq, bk, bv, bias_gen, batch, seq)
        yc = _retention(cq, ckt, cv, cg, c_out_norm[l].astype(_F32)[:, None, :], batch, seq)

        xt = _merge_ffn(l, xt, ya, yb, yc, gates, wba, wbb, wbc, wo, ffn2_norm[l][None, :],
                        f2g, f2u, f2d)
    return xt.reshape(batch, seq, D_MODEL)
```
